```python
import math
import jax
import jax.numpy as jnp
from jax import lax
import numpy as np

D_MODEL = 1024
BATCH = 32
SEQ = 256
DEPTH = 2
DEC_BATCH = 8
DEC_SEQ = 4096
PAST_LEN = 512

GRID_W = 64
HEAD_DIM = 64
D_MIX = D_MODEL
GROUP_W = D_MIX // 4
NORM_EPS = 1e-6
NEG_INF = -1e30
GQA_HEADS = GROUP_W // HEAD_DIM
GQA_KV_HEADS = GQA_HEADS // 2
GQA_GROUP = GQA_HEADS // GQA_KV_HEADS
ROPE_THETA = 10000.0
Q_BLOCK = 128
RW_HEADS = GROUP_W // HEAD_DIM
RW_DECAY_RANK = 32
RW_AAA_RANK = 32
RW_GATE_RANK = 64
RW_LN_EPS = 64e-5
NA_HEADS = GROUP_W // HEAD_DIM
NA_ROWS = 8
NA_COLS = 16
NA_QCOLS = 16
NA_SPAN = NA_QCOLS + NA_COLS
DN_HEADS = GROUP_W // HEAD_DIM
DN_CONV = 5
DN_CHUNK = 64
PEER_HEADS = 8
PEER_NKEYS = 128
PEER_EXPERTS = PEER_NKEYS * PEER_NKEYS
PEER_DQ = 256
PEER_TOPK = 16
PEER_BLOCK = 128
GQA_PROJ = GROUP_W + 2 * GQA_KV_HEADS * HEAD_DIM
RW_SHIFT_COLS = 3 * GROUP_W + RW_DECAY_RANK + RW_AAA_RANK
RW_PROJ = 3 * GROUP_W + 2 * RW_DECAY_RANK + 2 * RW_AAA_RANK + RW_GATE_RANK
NA_PROJ = 3 * GROUP_W
DN_PROJ = 3 * GROUP_W + 4 * DN_HEADS + GROUP_W
IN_COLS = GQA_PROJ + RW_PROJ + NA_PROJ + DN_PROJ

kernel_name = "hybrid_diffusion_prefix_trunk_step"


def rms_norm(x, g):
    xf = x.astype(jnp.float32)
    y = xf * lax.rsqrt(jnp.mean(xf * xf, axis=-1, keepdims=True) + NORM_EPS)
    return (y * g.astype(jnp.float32)).astype(x.dtype)


def l2_norm(x):
    return x * lax.rsqrt(jnp.sum(x * x, axis=-1, keepdims=True) + NORM_EPS)


def axial_rope_tables(T):
    t = np.arange(T)
    row, col = t // GRID_W, t % GRID_W
    n_freq = HEAD_DIM // 4
    inv = ROPE_THETA ** (-2.0 * np.arange(n_freq) / (HEAD_DIM // 2))
    ang = np.concatenate([row[:, None] * inv[None], col[:, None] * inv[None]], axis=1)
    return jnp.asarray(np.cos(ang), jnp.float32), jnp.asarray(np.sin(ang), jnp.float32)


def apply_rope(x, cos, sin):
    xf = x.astype(jnp.float32)
    half = HEAD_DIM // 2
    x1, x2 = xf[..., :half], xf[..., half:]
    c, s = cos[None, :, None, :], sin[None, :, None, :]
    return jnp.concatenate([x1 * c - x2 * s, x2 * c + x1 * s], axis=-1).astype(x.dtype)


def blocked_attention(q, k, v):
    B, KVH, G, T, dh = q.shape
    nb = T // Q_BLOCK
    qb = jnp.moveaxis(q.reshape(B, KVH, G, nb, Q_BLOCK, dh), 3, 0)
    scale = HEAD_DIM ** -0.5

    def one_block(qblk):
        s = jnp.einsum("bkgqd,bkld->bkgql", qblk, k).astype(jnp.float32) * scale
        p = jax.nn.softmax(s, axis=-1).astype(v.dtype)
        return jnp.einsum("bkgql,bkld->bkgqd", p, v)

    o = lax.map(one_block, qb)
    return jnp.moveaxis(o, 0, 3).reshape(B, KVH, G, T, dh)


def neighbourhood_attention(q, k, v, bias, k_ctx, v_ctx):
    B, H, T, dh = q.shape
    rows = T // GRID_W
    wr = min(NA_ROWS, rows)
    r = np.arange(rows)
    row_idx = np.clip(r - wr // 2, 0, rows - wr)[:, None] + np.arange(wr)[None, :]
    dr_idx = row_idx - r[:, None] + (NA_ROWS - 1)
    qg = q.reshape(B, H, rows, GRID_W, dh)
    kg = k.reshape(B, H, rows, GRID_W, dh)[:, :, row_idx]
    vg = v.reshape(B, H, rows, GRID_W, dh)[:, :, row_idx]
    k_ctx = k_ctx.astype(q.dtype)
    v_ctx = v_ctx.astype(v.dtype)
    scale = HEAD_DIM ** -0.5
    n_loc = wr * NA_SPAN
    outs = []
    for qc0 in range(0, GRID_W, NA_QCOLS):
        kc0 = min(max(qc0 - NA_COLS // 2, 0), GRID_W - NA_SPAN)
        qcols = np.arange(qc0, qc0 + NA_QCOLS)
        kcols = np.arange(kc0, kc0 + NA_SPAN)
        cstart = np.clip(qcols - NA_COLS // 2, 0, GRID_W - NA_COLS)
        valid = (kcols[None, :] >= cstart[:, None]) & (kcols[None, :] < cstart[:, None] + NA_COLS)
        dc_idx = np.clip(kcols[None, :] - qcols[:, None], -(NA_COLS - 1), NA_COLS - 1) + (NA_COLS - 1)
        rel = bias[:, dr_idx[:, None, :, None], dc_idx[None, :, None, :]].astype(jnp.float32)
        qj = qg[:, :, :, qc0:qc0 + NA_QCOLS]
        kj = kg[:, :, :, :, kc0:kc0 + NA_SPAN]
        vj = vg[:, :, :, :, kc0:kc0 + NA_SPAN]
        s_loc = jnp.einsum("bhrqd,bhrwcd->bhrqwc", qj, kj).astype(jnp.float32) * scale + rel[None]
        s_loc = jnp.where(valid[:, None, :], s_loc, NEG_INF)
        s_ctx = jnp.einsum("bhrqd,bhpd->bhrqp", qj, k_ctx).astype(jnp.float32) * scale
        s_all = jnp.concatenate([s_loc.reshape(B, H, rows, NA_QCOLS, n_loc), s_ctx], axis=-1)
        prob = jax.nn.softmax(s_all, axis=-1).astype(v.dtype)
        p_loc = prob[..., :n_loc].reshape(B, H, rows, NA_QCOLS, wr, NA_SPAN)
        o = jnp.einsum("bhrqwc,bhrwcd->bhrqd", p_loc, vj) + jnp.einsum("bhrqp,bhpd->bhrqd", prob[..., n_loc:], v_ctx)
        outs.append(o)
    return jnp.concatenate(outs, axis=3).reshape(B, H, T, dh)


def gqa_mixer(p, lp, ctx_kv):
    B, T, _ = p.shape
    kvw = GQA_KV_HEADS * HEAD_DIM
    q = rms_norm(p[..., :GROUP_W].reshape(B, T, GQA_HEADS, HEAD_DIM), lp["gqa_q_norm"])
    k = rms_norm(p[..., GROUP_W:GROUP_W + kvw].reshape(B, T, GQA_KV_HEADS, HEAD_DIM), lp["gqa_k_norm"])
    v = p[..., GROUP_W + kvw:].reshape(B, T, GQA_KV_HEADS, HEAD_DIM)
    if ctx_kv is not None:
        cos, sin = axial_rope_tables(T)
        q = apply_rope(q, cos, sin)
        k = apply_rope(k, cos, sin)
    qh = q.reshape(B, T, GQA_KV_HEADS, GQA_GROUP, HEAD_DIM).transpose(0, 2, 3, 1, 4)
    kh = k.transpose(0, 2, 1, 3)
    vh = v.transpose(0, 2, 1, 3)
    if ctx_kv is None:
        o = blocked_attention(qh, kh, vh)
    else:
        k_ctx, v_ctx = ctx_kv
        o = blocked_attention(qh, jnp.concatenate([kh, k_ctx.astype(kh.dtype)], axis=2),
                              jnp.concatenate([vh, v_ctx.astype(vh.dtype)], axis=2))
    return o.transpose(0, 3, 1, 2, 4).reshape(B, T, GROUP_W), (kh, vh)


def rwkv_scan(r, w, k, v, a, b, S0, reverse):
    def step(S, inp):
        r_t, w_t, k_t, v_t, a_t, b_t = inp
        sa = jnp.einsum("bhvk,bhk->bhv", S, a_t)
        S = S * w_t[:, :, None, :] + sa[..., None] * b_t[:, :, None, :] + v_t[..., None] * k_t[:, :, None, :]
        return S, jnp.einsum("bhvk,bhk->bhv", S, r_t)

    S, o = lax.scan(step, S0, (r, w, k, v, a, b), reverse=reverse)
    return o, S


def rwkv_mixer(p, lp, S0):
    B, T, _ = p.shape
    f32 = jnp.float32
    W, H = GROUP_W, RW_HEADS
    o0 = 3 * W
    o1 = o0 + 2 * RW_DECAY_RANK
    o2 = o1 + 2 * RW_AAA_RANK
    rkv = p[..., :o0]
    wd = p[..., o0:o1].reshape(B, T, 2, RW_DECAY_RANK)
    ad = p[..., o1:o2].reshape(B, T, 2, RW_AAA_RANK)
    gate = jax.nn.sigmoid(p[..., o2:]) @ lp["rw_g2"]
    heads = lambda a: a.astype(f32).reshape(B, T, H, HEAD_DIM)
    tm = lambda a: jnp.swapaxes(a, 0, 1)
    S0 = S0.astype(f32)
    outs, bonuses, finals = [], [], []
    for d in range(2):
        base = jnp.concatenate([rkv, wd[:, :, d], ad[:, :, d]], axis=-1)
        if d == 0:
            shifted = jnp.pad(base[:, :-1], ((0, 0), (1, 0), (0, 0)))
        else:
            shifted = jnp.pad(base[:, 1:], ((0, 0), (0, 1), (0, 0)))
        xd = base + (shifted - base) * lp["rw_mu"][d]
        r, k, v = xd[..., :W], xd[..., W:2 * W], xd[..., 2 * W:3 * W]
        wl, al = xd[..., 3 * W:3 * W + RW_DECAY_RANK], xd[..., 3 * W + RW_DECAY_RANK:]
        w = -jax.nn.softplus(-(lp["rw_w0"][d] + jnp.tanh(wl) @ lp["rw_w2"][d])) - 0.5
        decay = jnp.exp(-jnp.exp(w.astype(f32)))
        a = jax.nn.sigmoid((lp["rw_a0"][d] + al @ lp["rw_a2"][d]).astype(f32))
        kf = k.astype(f32)
        kk = l2_norm(heads(kf * lp["rw_k_k"].astype(f32)))
        k_eff = heads(kf * (1.0 + (a - 1.0) * lp["rw_k_a"].astype(f32)))
        rh, vh, ah = heads(r), heads(v), heads(a)
        o, S = rwkv_scan(tm(rh), tm(heads(decay)), tm(k_eff), tm(vh), tm(-kk), tm(kk * ah), S0[:, d], d == 1)
        outs.append(tm(o))
        bonuses.append(jnp.sum(rh * k_eff * lp["rw_r_k"].astype(f32), axis=-1, keepdims=True) * vh)
        finals.append(S)
    o = outs[0] + outs[1]
    mu = jnp.mean(o, axis=-1, keepdims=True)
    var = jnp.mean(jnp.square(o - mu), axis=-1, keepdims=True)
    y = ((o - mu) * lax.rsqrt(var + RW_LN_EPS)).reshape(B, T, W) * lp["rw_ln_g"].astype(f32) + lp["rw_ln_b"].astype(f32)
    y = y + (bonuses[0] + bonuses[1]).reshape(B, T, W)
    return y.astype(p.dtype) * gate, jnp.stack(finals, axis=1)


def na_mixer(p, lp, ctx_kv):
    B, T, _ = p.shape
    q, k, v = [p[..., i * GROUP_W:(i + 1) * GROUP_W].reshape(B, T, NA_HEADS, HEAD_DIM).transpose(0, 2, 1, 3) for i in range(3)]
    if ctx_kv is None:
        o = blocked_attention(q[:, :, None], k, v)[:, :, 0]
    else:
        o = neighbourhood_attention(q, k, v, lp["na_bias"], ctx_kv[0], ctx_kv[1])
    return o.transpose(0, 2, 1, 3).reshape(B, T, GROUP_W), (k, v)


def gated_delta_chunked(q, k, v, g, beta, S0):
    B, T, H, dk = q.shape
    dv = v.shape[-1]
    n = T // DN_CHUNK

    def to_chunks(a):
        a = a.reshape((B, n, DN_CHUNK, H) + a.shape[3:])
        return jnp.moveaxis(a, (1, 3), (0, 2))

    qc, kc, vc, bc = to_chunks(q), to_chunks(k), to_chunks(v), to_chunks(beta)
    gc = jnp.cumsum(to_chunks(g), axis=-1)
    idx = np.arange(DN_CHUNK)
    incl = idx[:, None] >= idx[None, :]
    strict = idx[:, None] > idx[None, :]
    diff = gc[..., :, None] - gc[..., None, :]
    dmat = jnp.where(incl, jnp.exp(jnp.where(incl, diff, 0.0)), 0.0)
    kb = kc * bc[..., None]
    A = jnp.where(strict, jnp.einsum("...id,...jd->...ij", kb, kc) * dmat, 0.0)
    eye = jnp.eye(DN_CHUNK, dtype=A.dtype)
    Tm = lax.linalg.triangular_solve(A + eye, jnp.broadcast_to(eye, A.shape), left_side=True, lower=True, unit_diagonal=True)
    u = Tm @ (vc * bc[..., None])
    w = Tm @ (kb * jnp.exp(gc)[..., None])
    qk = jnp.where(incl, jnp.einsum("...id,...jd->...ij", qc, kc) * dmat, 0.0)

    def step(S, inp):
        q_i, k_i, u_i, w_i, g_i, qk_i = inp
        v_new = u_i - jnp.einsum("bhck,bhkv->bhcv", w_i, S)
        o = jnp.einsum("bhck,bhkv->bhcv", q_i * jnp.exp(g_i)[..., None], S) + jnp.einsum("bhij,bhjv->bhiv", qk_i, v_new)
        g_last = g_i[..., -1]
        S = S * jnp.exp(g_last)[..., None, None] + jnp.einsum(
            "bhck,bhcv->bhkv", k_i * jnp.exp(g_last[..., None] - g_i)[..., None], v_new)
        return S, o

    S, o = lax.scan(step, S0, (qc, kc, u, w, gc, qk))
    return jnp.moveaxis(o, (0, 2), (1, 3)).reshape(B, T, H, dv), S


def deltanet_mixer(p, lp, S0):
    B, T, _ = p.shape
    f32 = jnp.float32
    W, H = GROUP_W, DN_HEADS
    qkv = lax.conv_general_dilated(p[..., :3 * W], lp["dn_conv"][:, None, :], window_strides=(1,),
                                   padding=[(DN_CONV // 2, DN_CONV // 2)],
                                   dimension_numbers=("NWC", "WIO", "NWC"), feature_group_count=3 * W)
    qkv = jax.nn.silu(qkv)
    heads = lambda a: a.astype(f32).reshape(B, T, H, HEAD_DIM)
    q = l2_norm(heads(qkv[..., :W])) * (HEAD_DIM ** -0.5)
    k = l2_norm(heads(qkv[..., W:2 * W]))
    v = heads(qkv[..., 2 * W:])
    beta = jax.nn.sigmoid(p[..., 3 * W:3 * W + 2 * H].astype(f32)).reshape(B, T, 2, H)
    alpha = p[..., 3 * W + 2 * H:3 * W + 4 * H].astype(f32).reshape(B, T, 2, H)
    g = -jnp.exp(lp["dn_a_log"].astype(f32)) * jax.nn.softplus(alpha + lp["dn_dt_bias"].astype(f32))
    z = heads(p[..., 3 * W + 4 * H:])
    S0 = S0.astype(f32)
    o_f, s_f = gated_delta_chunked(q, k, v, g[:, :, 0], beta[:, :, 0], S0[:, 0])
    rev = lambda a: jnp.flip(a, axis=1)
    o_b, s_b = gated_delta_chunked(rev(q), rev(k), rev(v), rev(g[:, :, 1]), rev(beta[:, :, 1]), S0[:, 1])
    o = rms_norm(o_f + rev(o_b), lp["dn_norm_g"]) * jax.nn.silu(z)
    return o.reshape(B, T, W).astype(p.dtype), jnp.stack([s_f, s_b], axis=1)


def peer_ffn(h, lp):
    B, T, D = h.shape
    nb = (B * T) // PEER_BLOCK
    wq, keys, u, v = lp["peer_wq"], lp["peer_keys"], lp["peer_u"], lp["peer_v"]

    def block(xb):
        q = (xb @ wq).reshape(PEER_BLOCK, PEER_HEADS, 2, PEER_DQ // 2)
        s = jnp.einsum("nhpd,hpkd->nhpk", q, keys).astype(jnp.float32)
        s1, i1 = lax.top_k(s[:, :, 0], PEER_TOPK)
        s2, i2 = lax.top_k(s[:, :, 1], PEER_TOPK)
        cand = (s1[..., :, None] + s2[..., None, :]).reshape(PEER_BLOCK, PEER_HEADS, PEER_TOPK * PEER_TOPK)
        cidx = (i1[..., :, None] * PEER_NKEYS + i2[..., None, :]).reshape(PEER_BLOCK, PEER_HEADS, PEER_TOPK * PEER_TOPK)
        top_s, pos = lax.top_k(cand, PEER_TOPK)
        idx = jnp.take_along_axis(cidx, pos, axis=-1).reshape(PEER_BLOCK, PEER_HEADS * PEER_TOPK)
        gate = jax.nn.softmax(top_s, axis=-1).reshape(PEER_BLOCK, PEER_HEADS * PEER_TOPK).astype(xb.dtype)
        act = jax.nn.gelu(jnp.einsum("nd,ned->ne", xb, u[idx]), approximate=False)
        return jnp.einsum("ne,ned->nd", gate * act, v[idx])

    return lax.map(block, h.reshape(nb, PEER_BLOCK, D)).reshape(B, T, D)


def trunk_layer(x, cond, lp, ctx):
    B, T, _ = x.shape
    mod = jax.nn.silu(cond) @ lp["w_mod"] + lp["b_mod"]
    sh1, sc1, g1, sh2, sc2, g2 = [m[:, None, :] for m in jnp.split(mod, 6, axis=-1)]
    h = rms_norm(x, lp["norm1_g"]) * (1 + sc1) + sh1
    proj = h @ lp["w_in"]
    pa, pb, pc, pd = jnp.split(proj, [GQA_PROJ, GQA_PROJ + RW_PROJ, GQA_PROJ + RW_PROJ + NA_PROJ], axis=-1)
    if ctx is None:
        ctx_a = None
        ctx_c = None
        s_rw0 = jnp.zeros((B, 2, RW_HEADS, HEAD_DIM, HEAD_DIM), jnp.float32)
        s_dn0 = jnp.zeros((B, 2, DN_HEADS, HEAD_DIM, HEAD_DIM), jnp.float32)
    else:
        ka_c, va_c, kc_c, vc_c, s_rw0, s_dn0 = ctx
        ctx_a = (ka_c, va_c)
        ctx_c = (kc_c, vc_c)
    oa, (ka, va) = gqa_mixer(pa, lp, ctx_a)
    ob, s_rw = rwkv_mixer(pb, lp, s_rw0)
    oc, (kc, vc) = na_mixer(pc, lp, ctx_c)
    od, s_dn = deltanet_mixer(pd, lp, s_dn0)
    mix = jnp.concatenate([oa, ob, oc, od], axis=-1) @ lp["w_out"]
    x = x + g1 * mix
    h2 = rms_norm(x, lp["norm2_g"]) * (1 + sc2) + sh2
    x = x + g2 * peer_ffn(h2, lp)
    return x, (ka, va, kc, vc, s_rw, s_dn)


def setup_inputs(seed: int = 0) -> dict:
    key = jax.random.key(seed)
    keys = iter(jax.random.split(key, 64))
    f32 = jnp.float32

    def nrm(shape, scale):
        return jax.random.normal(next(keys), shape, f32) * scale

    def gain(shape):
        return 1.0 + 0.02 * jax.random.normal(next(keys), shape, f32)

    def unif(shape, lo, hi):
        return jax.random.uniform(next(keys), shape, f32, lo, hi)

    W = GROUP_W
    dt = jnp.exp(unif((DEPTH, 2, DN_HEADS), math.log(1e-3), math.log(1e-1)))
    return {
        "x_prompt": nrm((BATCH, SEQ, D_MODEL), 1.0),
        "x_sample": nrm((DEC_BATCH, DEC_SEQ, D_MODEL), 1.0),
        "c": nrm((DEC_BATCH, D_MODEL), 1.0),
        "cache_gqa_k": nrm((DEC_BATCH, DEPTH, GQA_KV_HEADS, PAST_LEN, HEAD_DIM), 1.0),
        "cache_gqa_v": nrm((DEC_BATCH, DEPTH, GQA_KV_HEADS, PAST_LEN, HEAD_DIM), 1.0),
        "cache_na_k": nrm((DEC_BATCH, DEPTH, NA_HEADS, PAST_LEN, HEAD_DIM), 1.0),
        "cache_na_v": nrm((DEC_BATCH, DEPTH, NA_HEADS, PAST_LEN, HEAD_DIM), 1.0),
        "state_rwkv": nrm((DEC_BATCH, DEPTH, 2, RW_HEADS, HEAD_DIM, HEAD_DIM), 0.1),
        "state_delta": nrm((DEC_BATCH, DEPTH, 2, DN_HEADS, HEAD_DIM, HEAD_DIM), 0.1),
        "c_ctx": nrm((D_MODEL,), 1.0),
        "norm1_g": gain((DEPTH, D_MODEL)),
        "norm2_g": gain((DEPTH, D_MODEL)),
        "w_mod": nrm((DEPTH, D_MODEL, 6 * D_MODEL), 0.5 * D_MODEL ** -0.5),
        "b_mod": nrm((DEPTH, 6 * D_MODEL), 0.02),
        "w_in": nrm((DEPTH, D_MODEL, IN_COLS), D_MODEL ** -0.5),
        "w_out": nrm((DEPTH, D_MIX, D_MODEL), D_MIX ** -0.5),
        "gqa_q_norm": gain((DEPTH, HEAD_DIM)),
        "gqa_k_norm": gain((DEPTH, HEAD_DIM)),
        "rw_mu": unif((DEPTH, 2, RW_SHIFT_COLS), 0.0, 1.0),
        "rw_w0": unif((DEPTH, 2, W), -5.0, -0.5),
        "rw_w2": nrm((DEPTH, 2, RW_DECAY_RANK, W), 0.1),
        "rw_a0": nrm((DEPTH, 2, W), 0.1),
        "rw_a2": nrm((DEPTH, 2, RW_AAA_RANK, W), RW_AAA_RANK ** -0.5),
        "rw_g2": nrm((DEPTH, RW_GATE_RANK, W), RW_GATE_RANK ** -0.5),
        "rw_k_k": 0.85 * gain((DEPTH, W)),
        "rw_k_a": gain((DEPTH, W)),
        "rw_r_k": nrm((DEPTH, RW_HEADS, HEAD_DIM), 0.1),
        "rw_ln_g": gain((DEPTH, W)),
        "rw_ln_b": nrm((DEPTH, W), 0.02),
        "na_bias": nrm((DEPTH, NA_HEADS, 2 * NA_ROWS - 1, 2 * NA_COLS - 1), 0.1),
        "dn_conv": nrm((DEPTH, DN_CONV, 3 * W), DN_CONV ** -0.5),
        "dn_a_log": jnp.log(unif((DEPTH, 2, DN_HEADS), 1.0, 16.0)),
        "dn_dt_bias": dt + jnp.log(-jnp.expm1(-dt)),
        "dn_norm_g": gain((DEPTH, HEAD_DIM)),
        "peer_wq": nrm((DEPTH, D_MODEL, PEER_HEADS * PEER_DQ), D_MODEL ** -0.5),
        "peer_keys": nrm((DEPTH, PEER_HEADS, 2, PEER_NKEYS, PEER_DQ // 2), (PEER_DQ // 2) ** -0.5),
        "peer_u": nrm((DEPTH, PEER_EXPERTS, D_MODEL), D_MODEL ** -0.5),
        "peer_v": nrm((DEPTH, PEER_EXPERTS, D_MODEL), (PEER_HEADS * PEER_TOPK) ** -0.5),
        "final_norm_g": gain((D_MODEL,)),
    }


def reference(x_prompt, x_sample, c, cache_gqa_k, cache_gqa_v, cache_na_k, cache_na_v, state_rwkv, state_delta,
              c_ctx, norm1_g, norm2_g, w_mod, b_mod, w_in, w_out, gqa_q_norm, gqa_k_norm,
              rw_mu, rw_w0, rw_w2, rw_a0, rw_a2, rw_g2, rw_k_k, rw_k_a, rw_r_k, rw_ln_g, rw_ln_b,
              na_bias, dn_conv, dn_a_log, dn_dt_bias, dn_norm_g,
              peer_wq, peer_keys, peer_u, peer_v, final_norm_g):
    def layer_params(l):
        return {
            "norm1_g": norm1_g[l], "norm2_g": norm2_g[l], "w_mod": w_mod[l], "b_mod": b_mod[l],
            "w_in": w_in[l], "w_out": w_out[l], "gqa_q_norm": gqa_q_norm[l], "gqa_k_norm": gqa_k_norm[l],
            "rw_mu": rw_mu[l], "rw_w0": rw_w0[l], "rw_w2": rw_w2[l], "rw_a0": rw_a0[l], "rw_a2": rw_a2[l],
            "rw_g2": rw_g2[l], "rw_k_k": rw_k_k[l], "rw_k_a": rw_k_a[l], "rw_r_k": rw_r_k[l],
            "rw_ln_g": rw_ln_g[l], "rw_ln_b": rw_ln_b[l], "na_bias": na_bias[l],
            "dn_conv": dn_conv[l], "dn_a_log": dn_a_log[l], "dn_dt_bias": dn_dt_bias[l], "dn_norm_g": dn_norm_g[l],
            "peer_wq": peer_wq[l], "peer_keys": peer_keys[l], "peer_u": peer_u[l], "peer_v": peer_v[l],
        }

    xp = x_prompt
    cond_ctx = c_ctx[None, :]
    per_layer = []
    for l in range(DEPTH):
        xp, st = trunk_layer(xp, cond_ctx, layer_params(l), None)
        per_layer.append(st)
    y_prompt = rms_norm(xp, final_norm_g)

    xs = x_sample
    for l in range(DEPTH):
        ctx = (cache_gqa_k[:, l], cache_gqa_v[:, l], cache_na_k[:, l], cache_na_v[:, l], state_rwkv[:, l], state_delta[:, l])
        xs, _ = trunk_layer(xs, c, layer_params(l), ctx)
    y_sample = rms_norm(xs, final_norm_g)

    dt = x_prompt.dtype
    new_gqa_k = jnp.stack([st[0] for st in per_layer], axis=1).astype(dt)
    new_gqa_v = jnp.stack([st[1] for st in per_layer], axis=1).astype(dt)
    new_na_k = jnp.stack([st[2] for st in per_layer], axis=1).astype(dt)
    new_na_v = jnp.stack([st[3] for st in per_layer], axis=1).astype(dt)
    new_state_rwkv = jnp.stack([st[4] for st in per_layer], axis=1).astype(dt)
    new_state_delta = jnp.stack([st[5] for st in per_layer], axis=1).astype(dt)
    return (y_prompt, y_sample, new_gqa_k, new_gqa_v, new_na_k, new_na_v, new_state_rwkv, new_state_delta)
```

```python
import functools
import math

import jax
import jax.numpy as jnp
import numpy as np
from jax import lax
from jax.experimental import pallas as pl
from jax.experimental.pallas import tpu as pltpu

F32 = jnp.float32
BF16 = jnp.bfloat16

D_MODEL = 1024
GRID_W = 64
HEAD_DIM = 64
GROUP_W = D_MODEL // 4
NORM_EPS = 1e-6
NEG_INF = -1e30
GQA_HEADS = GROUP_W // HEAD_DIM
GQA_KV_HEADS = GQA_HEADS // 2
GQA_GROUP = GQA_HEADS // GQA_KV_HEADS
ROPE_THETA = 10000.0
Q_BLOCK = 128
RW_HEADS = GROUP_W // HEAD_DIM
RW_DECAY_RANK = 32
RW_AAA_RANK = 32
RW_GATE_RANK = 64
RW_LN_EPS = 64e-5
NA_HEADS = GROUP_W // HEAD_DIM
NA_ROWS = 8
NA_COLS = 16
NA_QCOLS = 16
NA_SPAN = NA_QCOLS + NA_COLS
DN_HEADS = GROUP_W // HEAD_DIM
DN_CONV = 5
DN_CHUNK = 64
PEER_HEADS = 8
PEER_NKEYS = 128
PEER_EXPERTS = PEER_NKEYS * PEER_NKEYS
PEER_DQ = 256
PEER_TOPK = 16
GQA_PROJ = GROUP_W + 2 * GQA_KV_HEADS * HEAD_DIM
RW_PROJ = 3 * GROUP_W + 2 * RW_DECAY_RANK + 2 * RW_AAA_RANK + RW_GATE_RANK
NA_PROJ = 3 * GROUP_W
DN_PROJ = 3 * GROUP_W + 4 * DN_HEADS + GROUP_W

LANES = 128
VMEM_LIMIT_BYTES = 56 * 1024 * 1024

PEER_SEL_TOKENS = 128
PEER_TOKENS = 256
PEER_ROWS_PER_STEP = 8
PEER_CHUNK = PEER_ROWS_PER_STEP * PEER_NKEYS
PEER_STEPS = PEER_EXPERTS // PEER_CHUNK
INV_SQRT2 = 1.0 / math.sqrt(2.0)


def _top_k_rows(s, k):
    rows = s.shape[0]
    iota = lax.broadcasted_iota(jnp.int32, s.shape, 0)
    vals = []
    work = s
    for _ in range(k):
        m = jnp.max(work, axis=0, keepdims=True)
        first = jnp.min(jnp.where(work == m, iota, rows), axis=0, keepdims=True)
        work = jnp.where(iota == first, -jnp.inf, work)
        vals.append(m)
    return vals, work


def _peer_select_kernel(h_ref, wq_ref, keys_ref, s1m_ref, e1_ref, s2m_ref, e2_ref, t3_ref):
    x = h_ref[...].astype(BF16)
    q = jnp.dot(x, wq_ref[...], preferred_element_type=F32)
    half = PEER_DQ // 2
    t3_rows = []
    for h in range(PEER_HEADS):
        scores, tops, sel = [], [], []
        for p in range(2):
            c0 = (2 * h + p) * half
            qhp = q[:, c0:c0 + half].astype(BF16)
            s = lax.dot_general(keys_ref[2 * h + p], qhp, (((1,), (1,)), ((), ())),
                                preferred_element_type=F32)
            vals, work = _top_k_rows(s, PEER_TOPK)
            scores.append(s)
            tops.append(vals)
            sel.append(work == -jnp.inf)
        m1, m2 = tops
        m2_all = jnp.concatenate(m2, axis=0)
        cand = jnp.concatenate([m1[i] + m2_all for i in range(8)]
                               + [jnp.concatenate(m1[8:], axis=0) + m2[0]], axis=0)
        top_s, _ = _top_k_rows(cand, PEER_TOPK)
        mx = top_s[0]
        z = top_s[0] - mx
        z = jnp.exp(z)
        for r in range(1, PEER_TOPK):
            z = z + jnp.exp(top_s[r] - mx)
        inv_z = 1.0 / z
        s1m_ref[h] = jnp.where(sel[0], scores[0], -jnp.inf)
        e1_ref[h] = jnp.where(sel[0], jnp.exp(scores[0] - m1[0]) * inv_z, 0.0)
        s2m_ref[h] = jnp.where(sel[1], scores[1], -jnp.inf)
        e2_ref[h] = jnp.exp(scores[1] - m2[0])
        t3_rows.append(top_s[PEER_TOPK - 1])
    t3_ref[...] = jnp.concatenate(t3_rows, axis=0)


def _peer_select(h2, wq_bf, keys_bf):
    n = h2.shape[0]
    tt = PEER_SEL_TOKENS
    big = jax.ShapeDtypeStruct((PEER_HEADS, PEER_NKEYS, n), F32)
    big_spec = pl.BlockSpec((PEER_HEADS, PEER_NKEYS, tt), lambda i: (0, 0, i))
    return pl.pallas_call(
        _peer_select_kernel,
        grid=(n // tt,),
        in_specs=[
            pl.BlockSpec((tt, D_MODEL), lambda i: (i, 0)),
            pl.BlockSpec((D_MODEL, PEER_HEADS * PEER_DQ), lambda i: (0, 0)),
            pl.BlockSpec((2 * PEER_HEADS, PEER_NKEYS, PEER_DQ // 2), lambda i: (0, 0, 0)),
        ],
        out_specs=[big_spec, big_spec, big_spec, big_spec,
                   pl.BlockSpec((PEER_HEADS, tt), lambda i: (0, i))],
        out_shape=[big, big, big, big, jax.ShapeDtypeStruct((PEER_HEADS, n), F32)],
        compiler_params=pltpu.CompilerParams(dimension_semantics=("arbitrary",),
                                             vmem_limit_bytes=VMEM_LIMIT_BYTES),
        name="peer_select",
    )(h2, wq_bf, keys_bf)


def _peer_dense_kernel(h_ref, u_ref, vt_ref, s1m_ref, e1_ref, s2m_ref, e2_ref, t3_ref, xres_ref, g2_ref,
                       out_ref, ht_ref, acc_ref, p_ref):
    j = pl.program_id(1)
    tt = h_ref.shape[0]

    @pl.when(j == 0)
    def _():
        ht_ref[...] = h_ref[...].T.astype(BF16)
        acc_ref[...] = jnp.zeros_like(acc_ref)

    act = jnp.dot(u_ref[...], ht_ref[...], preferred_element_type=F32)
    for a in range(PEER_ROWS_PER_STEP):
        rows = slice(a * PEER_NKEYS, (a + 1) * PEER_NKEYS)
        for c in range(tt // LANES):
            cs = slice(c * LANES, (c + 1) * LANES)
            zz = act[rows, cs]
            gelu = 0.5 * zz * (1.0 + lax.erf(zz * INV_SQRT2))
            w = jnp.zeros((PEER_NKEYS, LANES), F32)
            for h in range(PEER_HEADS):
                cand = s1m_ref[h, a:a + 1, cs] + s2m_ref[h, :, cs]
                w = w + jnp.where(cand >= t3_ref[h:h + 1, cs], e2_ref[h, :, cs], 0.0) * e1_ref[h, a:a + 1, cs]
            p_ref[rows, cs] = (w * gelu).astype(BF16)
    acc_ref[...] += jnp.dot(vt_ref[...], p_ref[...], preferred_element_type=F32)

    @pl.when(j == PEER_STEPS - 1)
    def _():
        out_ref[...] = xres_ref[...] + g2_ref[0] * acc_ref[...].T


def _peer_dense(h2, u_bf, vt_bf, sel, xres, g2, tokens_per_cond):
    n = h2.shape[0]
    tt = PEER_TOKENS
    s1m, e1, s2m, e2, t3 = sel
    tiles_per_cond = tokens_per_cond // tt
    g2_map = (lambda i, j: (0, 0, 0)) if g2.shape[0] == 1 else (lambda i, j: (i // tiles_per_cond, 0, 0))
    row_spec = pl.BlockSpec((PEER_HEADS, PEER_ROWS_PER_STEP, tt), lambda i, j: (0, j, i))
    col_spec = pl.BlockSpec((PEER_HEADS, PEER_NKEYS, tt), lambda i, j: (0, 0, i))
    tok_spec = pl.BlockSpec((tt, D_MODEL), lambda i, j: (i, 0))
    return pl.pallas_call(
        _peer_dense_kernel,
        grid=(n // tt, PEER_STEPS),
        in_specs=[
            tok_spec,
            pl.BlockSpec((PEER_CHUNK, D_MODEL), lambda i, j: (j, 0)),
            pl.BlockSpec((D_MODEL, PEER_CHUNK), lambda i, j: (0, j)),
            row_spec, row_spec, col_spec, col_spec,
            pl.BlockSpec((PEER_HEADS, tt), lambda i, j: (0, i)),
            tok_spec,
            pl.BlockSpec((1, 1, D_MODEL), g2_map),
        ],
        out_specs=tok_spec,
        out_shape=jax.ShapeDtypeStruct((n, D_MODEL), F32),
        scratch_shapes=[pltpu.VMEM((D_MODEL, tt), BF16), pltpu.VMEM((D_MODEL, tt), F32),
                        pltpu.VMEM((PEER_CHUNK, tt), BF16)],
        compiler_params=pltpu.CompilerParams(dimension_semantics=("arbitrary", "arbitrary"),
                                             vmem_limit_bytes=VMEM_LIMIT_BYTES),
        name="peer_dense",
    )(h2, u_bf, vt_bf, s1m, e1, s2m, e2, t3, xres, g2)


def _peer_residual(x, h2, g2, lp, tokens_per_cond):
    sel = _peer_select(h2, lp["peer_wq_bf"], lp["peer_keys_bf"])
    return _peer_dense(h2, lp["peer_u_bf"], lp["peer_vt_bf"], sel, x, g2, tokens_per_cond)


def _rms_norm(x, g):
    xf = x.astype(F32)
    y = xf * lax.rsqrt(jnp.mean(xf * xf, axis=-1, keepdims=True) + NORM_EPS)
    return (y * g.astype(F32)).astype(x.dtype)


def _l2_norm(x):
    return x * lax.rsqrt(jnp.sum(x * x, axis=-1, keepdims=True) + NORM_EPS)


def _axial_rope_tables(T):
    t = np.arange(T)
    row, col = t // GRID_W, t % GRID_W
    n_freq = HEAD_DIM // 4
    inv = ROPE_THETA ** (-2.0 * np.arange(n_freq) / (HEAD_DIM // 2))
    ang = np.concatenate([row[:, None] * inv[None], col[:, None] * inv[None]], axis=1)
    return jnp.asarray(np.cos(ang), F32), jnp.asarray(np.sin(ang), F32)


def _apply_rope(x, cos, sin):
    xf = x.astype(F32)
    half = HEAD_DIM // 2
    x1, x2 = xf[..., :half], xf[..., half:]
    c, s = cos[None, :, None, :], sin[None, :, None, :]
    return jnp.concatenate([x1 * c - x2 * s, x2 * c + x1 * s], axis=-1).astype(x.dtype)


def _blocked_attention(q, k, v):
    B, KVH, G, T, dh = q.shape
    nb = T // Q_BLOCK
    qb = jnp.moveaxis(q.reshape(B, KVH, G, nb, Q_BLOCK, dh), 3, 0)
    scale = HEAD_DIM ** -0.5

    def one_block(qblk):
        s = jnp.einsum("bkgqd,bkld->bkgql", qblk, k).astype(F32) * scale
        p = jax.nn.softmax(s, axis=-1).astype(v.dtype)
        return jnp.einsum("bkgql,bkld->bkgqd", p, v)

    o = lax.map(one_block, qb)
    return jnp.moveaxis(o, 0, 3).reshape(B, KVH, G, T, dh)


def _neighbourhood_attention(q, k, v, bias, k_ctx, v_ctx):
    B, H, T, dh = q.shape
    rows = T // GRID_W
    wr = min(NA_ROWS, rows)
    r = np.arange(rows)
    row_idx = np.clip(r - wr // 2, 0, rows - wr)[:, None] + np.arange(wr)[None, :]
    dr_idx = row_idx - r[:, None] + (NA_ROWS - 1)
    qg = q.reshape(B, H, rows, GRID_W, dh)
    kg = k.reshape(B, H, rows, GRID_W, dh)[:, :, row_idx]
    vg = v.reshape(B, H, rows, GRID_W, dh)[:, :, row_idx]
    scale = HEAD_DIM ** -0.5
    n_loc = wr * NA_SPAN
    outs = []
    for qc0 in range(0, GRID_W, NA_QCOLS):
        kc0 = min(max(qc0 - NA_COLS // 2, 0), GRID_W - NA_SPAN)
        qcols = np.arange(qc0, qc0 + NA_QCOLS)
        kcols = np.arange(kc0, kc0 + NA_SPAN)
        cstart = np.clip(qcols - NA_COLS // 2, 0, GRID_W - NA_COLS)
        valid = (kcols[None, :] >= cstart[:, None]) & (kcols[None, :] < cstart[:, None] + NA_COLS)
        dc_idx = np.clip(kcols[None, :] - qcols[:, None], -(NA_COLS - 1), NA_COLS - 1) + (NA_COLS - 1)
        rel = bias[:, dr_idx[:, None, :, None], dc_idx[None, :, None, :]].astype(F32)
        qj = qg[:, :, :, qc0:qc0 + NA_QCOLS]
        kj = kg[:, :, :, :, kc0:kc0 + NA_SPAN]
        vj = vg[:, :, :, :, kc0:kc0 + NA_SPAN]
        s_loc = jnp.einsum("bhrqd,bhrwcd->bhrqwc", qj, kj).astype(F32) * scale + rel[None]
        s_loc = jnp.where(valid[:, None, :], s_loc, NEG_INF)
        s_ctx = jnp.einsum("bhrqd,bhpd->bhrqp", qj, k_ctx).astype(F32) * scale
        s_all = jnp.concatenate([s_loc.reshape(B, H, rows, NA_QCOLS, n_loc), s_ctx], axis=-1)
        prob = jax.nn.softmax(s_all, axis=-1).astype(v.dtype)
        p_loc = prob[..., :n_loc].reshape(B, H, rows, NA_QCOLS, wr, NA_SPAN)
        o = jnp.einsum("bhrqwc,bhrwcd->bhrqd", p_loc, vj) + jnp.einsum("bhrqp,bhpd->bhrqd", prob[..., n_loc:], v_ctx)
        outs.append(o)
    return jnp.concatenate(outs, axis=3).reshape(B, H, T, dh)


def _gqa_mixer(p, lp, ctx_kv):
    B, T, _ = p.shape
    kvw = GQA_KV_HEADS * HEAD_DIM
    q = _rms_norm(p[..., :GROUP_W].reshape(B, T, GQA_HEADS, HEAD_DIM), lp["gqa_q_norm"])
    k = _rms_norm(p[..., GROUP_W:GROUP_W + kvw].reshape(B, T, GQA_KV_HEADS, HEAD_DIM), lp["gqa_k_norm"])
    v = p[..., GROUP_W + kvw:].reshape(B, T, GQA_KV_HEADS, HEAD_DIM)
    if ctx_kv is not None:
        cos, sin = _axial_rope_tables(T)
        q = _apply_rope(q, cos, sin)
        k = _apply_rope(k, cos, sin)
    qh = q.reshape(B, T, GQA_KV_HEADS, GQA_GROUP, HEAD_DIM).transpose(0, 2, 3, 1, 4)
    kh = k.transpose(0, 2, 1, 3)
    vh = v.transpose(0, 2, 1, 3)
    if ctx_kv is None:
        o = _blocked_attention(qh, kh, vh)
    else:
        k_ctx, v_ctx = ctx_kv
        o = _blocked_attention(qh, jnp.concatenate([kh, k_ctx], axis=2), jnp.concatenate([vh, v_ctx], axis=2))
    return o.transpose(0, 3, 1, 2, 4).reshape(B, T, GROUP_W), (kh, vh)


def _rwkv_scan(r, w, k, v, a, b, S0, reverse):
    def step(S, inp):
        r_t, w_t, k_t, v_t, a_t, b_t = inp
        sa = jnp.einsum("bhvk,bhk->bhv", S, a_t)
        S = S * w_t[:, :, None, :] + sa[..., None] * b_t[:, :, None, :] + v_t[..., None] * k_t[:, :, None, :]
        return S, jnp.einsum("bhvk,bhk->bhv", S, r_t)

    S, o = lax.scan(step, S0, (r, w, k, v, a, b), reverse=reverse)
    return o, S


def _rwkv_mixer(p, lp, S0):
    B, T, _ = p.shape
    W, H = GROUP_W, RW_HEADS
    o0 = 3 * W
    o1 = o0 + 2 * RW_DECAY_RANK
    o2 = o1 + 2 * RW_AAA_RANK
    rkv = p[..., :o0]
    wd = p[..., o0:o1].reshape(B, T, 2, RW_DECAY_RANK)
    ad = p[..., o1:o2].reshape(B, T, 2, RW_AAA_RANK)
    gate = jax.nn.sigmoid(p[..., o2:]) @ lp["rw_g2"]
    heads = lambda a: a.astype(F32).reshape(B, T, H, HEAD_DIM)
    tm = lambda a: jnp.swapaxes(a, 0, 1)
    S0 = S0.astype(F32)
    outs, bonuses, finals = [], [], []
    for d in range(2):
        base = jnp.concatenate([rkv, wd[:, :, d], ad[:, :, d]], axis=-1)
        if d == 0:
            shifted = jnp.pad(base[:, :-1], ((0, 0), (1, 0), (0, 0)))
        else:
            shifted = jnp.pad(base[:, 1:], ((0, 0), (0, 1), (0, 0)))
        xd = base + (shifted - base) * lp["rw_mu"][d]
        r, k, v = xd[..., :W], xd[..., W:2 * W], xd[..., 2 * W:3 * W]
        wl, al = xd[..., 3 * W:3 * W + RW_DECAY_RANK], xd[..., 3 * W + RW_DECAY_RANK:]
        w = -jax.nn.softplus(-(lp["rw_w0"][d] + jnp.tanh(wl) @ lp["rw_w2"][d])) - 0.5
        decay = jnp.exp(-jnp.exp(w.astype(F32)))
        a = jax.nn.sigmoid((lp["rw_a0"][d] + al @ lp["rw_a2"][d]).astype(F32))
        kf = k.astype(F32)
        kk = _l2_norm(heads(kf * lp["rw_k_k"].astype(F32)))
        k_eff = heads(kf * (1.0 + (a - 1.0) * lp["rw_k_a"].astype(F32)))
        rh, vh, ah = heads(r), heads(v), heads(a)
        o, S = _rwkv_scan(tm(rh), tm(heads(decay)), tm(k_eff), tm(vh), tm(-kk), tm(kk * ah), S0[:, d], d == 1)
        outs.append(tm(o))
        bonuses.append(jnp.sum(rh * k_eff * lp["rw_r_k"].astype(F32), axis=-1, keepdims=True) * vh)
        finals.append(S)
    o = outs[0] + outs[1]
    mu = jnp.mean(o, axis=-1, keepdims=True)
    var = jnp.mean(jnp.square(o - mu), axis=-1, keepdims=True)
    y = ((o - mu) * lax.rsqrt(var + RW_LN_EPS)).reshape(B, T, W) * lp["rw_ln_g"].astype(F32) + lp["rw_ln_b"].astype(F32)
    y = y + (bonuses[0] + bonuses[1]).reshape(B, T, W)
    return y.astype(p.dtype) * gate, jnp.stack(finals, axis=1)


def _na_mixer(p, lp, ctx_kv):
    B, T, _ = p.shape
    q, k, v = [p[..., i * GROUP_W:(i + 1) * GROUP_W].reshape(B, T, NA_HEADS, HEAD_DIM).transpose(0, 2, 1, 3)
               for i in range(3)]
    if ctx_kv is None:
        o = _blocked_attention(q[:, :, None], k, v)[:, :, 0]
    else:
        o = _neighbourhood_attention(q, k, v, lp["na_bias"], ctx_kv[0], ctx_kv[1])
    return o.transpose(0, 2, 1, 3).reshape(B, T, GROUP_W), (k, v)


def _gated_delta_chunked(q, k, v, g, beta, S0):
    B, T, H, dk = q.shape
    dv = v.shape[-1]
    n = T // DN_CHUNK

    def to_chunks(a):
        a = a.reshape((B, n, DN_CHUNK, H) + a.shape[3:])
        return jnp.moveaxis(a, (1, 3), (0, 2))

    qc, kc, vc, bc = to_chunks(q), to_chunks(k), to_chunks(v), to_chunks(beta)
    gc = jnp.cumsum(to_chunks(g), axis=-1)
    idx = np.arange(DN_CHUNK)
    incl = idx[:, None] >= idx[None, :]
    strict = idx[:, None] > idx[None, :]
    diff = gc[..., :, None] - gc[..., None, :]
    dmat = jnp.where(incl, jnp.exp(jnp.where(incl, diff, 0.0)), 0.0)
    kb = kc * bc[..., None]
    A = jnp.where(strict, jnp.einsum("...id,...jd->...ij", kb, kc) * dmat, 0.0)
    eye = jnp.eye(DN_CHUNK, dtype=A.dtype)
    Tm = lax.linalg.triangular_solve(A + eye, jnp.broadcast_to(eye, A.shape), left_side=True, lower=True,
                                     unit_diagonal=True)
    u = Tm @ (vc * bc[..., None])
    w = Tm @ (kb * jnp.exp(gc)[..., None])
    qk = jnp.where(incl, jnp.einsum("...id,...jd->...ij", qc, kc) * dmat, 0.0)

    def step(S, inp):
        q_i, k_i, u_i, w_i, g_i, qk_i = inp
        v_new = u_i - jnp.einsum("bhck,bhkv->bhcv", w_i, S)
        o = jnp.einsum("bhck,bhkv->bhcv", q_i * jnp.exp(g_i)[..., None], S) + jnp.einsum("bhij,bhjv->bhiv", qk_i, v_new)
        g_last = g_i[..., -1]
        S = S * jnp.exp(g_last)[..., None, None] + jnp.einsum(
            "bhck,bhcv->bhkv", k_i * jnp.exp(g_last[..., None] - g_i)[..., None], v_new)
        return S, o

    S, o = lax.scan(step, S0, (qc, kc, u, w, gc, qk))
    return jnp.moveaxis(o, (0, 2), (1, 3)).reshape(B, T, H, dv), S


def _deltanet_mixer(p, lp, S0):
    B, T, _ = p.shape
    W, H = GROUP_W, DN_HEADS
    qkv = lax.conv_general_dilated(p[..., :3 * W], lp["dn_conv"][:, None, :], window_strides=(1,),
                                   padding=[(DN_CONV // 2, DN_CONV // 2)],
                                   dimension_numbers=("NWC", "WIO", "NWC"), feature_group_count=3 * W)
    qkv = jax.nn.silu(qkv)
    heads = lambda a: a.astype(F32).reshape(B, T, H, HEAD_DIM)
    q = _l2_norm(heads(qkv[..., :W])) * (HEAD_DIM ** -0.5)
    k = _l2_norm(heads(qkv[..., W:2 * W]))
    v = heads(qkv[..., 2 * W:])
    beta = jax.nn.sigmoid(p[..., 3 * W:3 * W + 2 * H].astype(F32)).reshape(B, T, 2, H)
    alpha = p[..., 3 * W + 2 * H:3 * W + 4 * H].astype(F32).reshape(B, T, 2, H)
    g = -jnp.exp(lp["dn_a_log"].astype(F32)) * jax.nn.softplus(alpha + lp["dn_dt_bias"].astype(F32))
    z = heads(p[..., 3 * W + 4 * H:])
    S0 = S0.astype(F32)
    o_f, s_f = _gated_delta_chunked(q, k, v, g[:, :, 0], beta[:, :, 0], S0[:, 0])
    rev = lambda a: jnp.flip(a, axis=1)
    o_b, s_b = _gated_delta_chunked(rev(q), rev(k), rev(v), rev(g[:, :, 1]), rev(beta[:, :, 1]), S0[:, 1])
    o = _rms_norm(o_f + rev(o_b), lp["dn_norm_g"]) * jax.nn.silu(z)
    return o.reshape(B, T, W).astype(p.dtype), jnp.stack([s_f, s_b], axis=1)


def _trunk_layer(x, cond, lp, ctx):
    B, T, _ = x.shape
    mod = jax.nn.silu(cond) @ lp["w_mod"] + lp["b_mod"]
    sh1, sc1, g1, sh2, sc2, g2 = [m[:, None, :] for m in jnp.split(mod, 6, axis=-1)]
    h = _rms_norm(x, lp["norm1_g"]) * (1 + sc1) + sh1
    proj = h @ lp["w_in"]
    pa, pb, pc, pd = jnp.split(proj, [GQA_PROJ, GQA_PROJ + RW_PROJ, GQA_PROJ + RW_PROJ + NA_PROJ], axis=-1)
    if ctx is None:
        ctx_a = None
        ctx_c = None
        s_rw0 = jnp.zeros((B, 2, RW_HEADS, HEAD_DIM, HEAD_DIM), F32)
        s_dn0 = jnp.zeros((B, 2, DN_HEADS, HEAD_DIM, HEAD_DIM), F32)
    else:
        ka_c, va_c, kc_c, vc_c, s_rw0, s_dn0 = ctx
        ctx_a = (ka_c, va_c)
        ctx_c = (kc_c, vc_c)
    oa, (ka, va) = _gqa_mixer(pa, lp, ctx_a)
    ob, s_rw = _rwkv_mixer(pb, lp, s_rw0)
    oc, (kc, vc) = _na_mixer(pc, lp, ctx_c)
    od, s_dn = _deltanet_mixer(pd, lp, s_dn0)
    mix = jnp.concatenate([oa, ob, oc, od], axis=-1) @ lp["w_out"]
    x = x + g1 * mix
    h2 = _rms_norm(x, lp["norm2_g"]) * (1 + sc2) + sh2
    x = _peer_residual(x.reshape(B * T, D_MODEL), h2.reshape(B * T, D_MODEL), g2, lp, T).reshape(B, T, D_MODEL)
    return x, (ka, va, kc, vc, s_rw, s_dn)


def kernel(x_prompt, x_sample, c, cache_gqa_k, cache_gqa_v, cache_na_k, cache_na_v, state_rwkv, state_delta, c_ctx, norm1_g, norm2_g, w_mod, b_mod, w_in, w_out, gqa_q_norm, gqa_k_norm, rw_mu, rw_w0, rw_w2, rw_a0, rw_a2, rw_g2, rw_k_k, rw_k_a, rw_r_k, rw_ln_g, rw_ln_b, na_bias, dn_conv, dn_a_log, dn_dt_bias, dn_norm_g, peer_wq, peer_keys, peer_u, peer_v, final_norm_g):
    depth = w_in.shape[0]

    def layer_params(l):
        return {
            "norm1_g": norm1_g[l], "norm2_g": norm2_g[l], "w_mod": w_mod[l], "b_mod": b_mod[l],
            "w_in": w_in[l], "w_out": w_out[l], "gqa_q_norm": gqa_q_norm[l], "gqa_k_norm": gqa_k_norm[l],
            "rw_mu": rw_mu[l], "rw_w0": rw_w0[l], "rw_w2": rw_w2[l], "rw_a0": rw_a0[l], "rw_a2": rw_a2[l],
            "rw_g2": rw_g2[l], "rw_k_k": rw_k_k[l], "rw_k_a": rw_k_a[l], "rw_r_k": rw_r_k[l],
            "rw_ln_g": rw_ln_g[l], "rw_ln_b": rw_ln_b[l], "na_bias": na_bias[l],
            "dn_conv": dn_conv[l], "dn_a_log": dn_a_log[l], "dn_dt_bias": dn_dt_bias[l], "dn_norm_g": dn_norm_g[l],
            "peer_wq_bf": peer_wq[l].astype(BF16),
            "peer_keys_bf": peer_keys[l].reshape(2 * PEER_HEADS, PEER_NKEYS, PEER_DQ // 2).astype(BF16),
            "peer_u_bf": peer_u[l].astype(BF16),
            "peer_vt_bf": peer_v[l].astype(BF16).T,
        }

    params = [layer_params(l) for l in range(depth)]

    xp = x_prompt
    cond_ctx = c_ctx[None, :]
    per_layer = []
    for l in range(depth):
        xp, st = _trunk_layer(xp, cond_ctx, params[l], None)
        per_layer.append(st)
    y_prompt = _rms_norm(xp, final_norm_g)

    xs = x_sample
    for l in range(depth):
        ctx = (cache_gqa_k[:, l], cache_gqa_v[:, l], cache_na_k[:, l], cache_na_v[:, l],
               state_rwkv[:, l], state_delta[:, l])
        xs, _ = _trunk_layer(xs, c, params[l], ctx)
    y_sample = _rms_norm(xs, final_norm_g)

    dt = x_prompt.dtype
    new_gqa_k = jnp.stack([st[0] for st in per_layer], axis=1).astype(dt)
    new_gqa_v = jnp.stack([st[1] for st in per_layer], axis=1).astype(dt)
    new_na_k = jnp.stack([st[2] for st in per_layer], axis=1).astype(dt)
    new_na_v = jnp.stack([st[3] for st in per_layer], axis=1).astype(dt)
    new_state_rwkv = jnp.stack([st[4] for st in per_layer], axis=1).astype(dt)
    new_state_delta = jnp.stack([st[5] for st in per_layer], axis=1).astype(dt)
    return (y_prompt, y_sample, new_gqa_k, new_gqa_v, new_na_k, new_na_v, new_state_rwkv, new_state_delta)
```

```python
import functools
import math

import jax
import jax.numpy as jnp
import numpy as np
from jax import lax
from jax.experimental import pallas as pl
from jax.experimental.pallas import tpu as pltpu

F32 = jnp.float32
BF16 = jnp.bfloat16

D_MODEL = 1024
GRID_W = 64
HEAD_DIM = 64
GROUP_W = D_MODEL // 4
NORM_EPS = 1e-6
NEG_INF = -1e30
GQA_HEADS = GROUP_W // HEAD_DIM
GQA_KV_HEADS = GQA_HEADS // 2
GQA_GROUP = GQA_HEADS // GQA_KV_HEADS
ROPE_THETA = 10000.0
Q_BLOCK = 128
RW_HEADS = GROUP_W // HEAD_DIM
RW_DECAY_RANK = 32
RW_AAA_RANK = 32
RW_GATE_RANK = 64
RW_LN_EPS = 64e-5
NA_HEADS = GROUP_W // HEAD_DIM
NA_ROWS = 8
NA_COLS = 16
NA_QCOLS = 16
NA_SPAN = NA_QCOLS + NA_COLS
DN_HEADS = GROUP_W // HEAD_DIM
DN_CONV = 5
DN_CHUNK = 64
PEER_HEADS = 8
PEER_NKEYS = 128
PEER_EXPERTS = PEER_NKEYS * PEER_NKEYS
PEER_DQ = 256
PEER_TOPK = 16
GQA_PROJ = GROUP_W + 2 * GQA_KV_HEADS * HEAD_DIM
RW_PROJ = 3 * GROUP_W + 2 * RW_DECAY_RANK + 2 * RW_AAA_RANK + RW_GATE_RANK
NA_PROJ = 3 * GROUP_W
DN_PROJ = 3 * GROUP_W + 4 * DN_HEADS + GROUP_W

LANES = 128
VMEM_LIMIT_BYTES = 56 * 1024 * 1024

PEER_SEL_TOKENS = 128
PEER_TOKENS = 256
PEER_ROWS_PER_STEP = 8
PEER_CHUNK = PEER_ROWS_PER_STEP * PEER_NKEYS
PEER_STEPS = PEER_EXPERTS // PEER_CHUNK
INV_SQRT2 = 1.0 / math.sqrt(2.0)

IN_COLS = GQA_PROJ + RW_PROJ + NA_PROJ + DN_PROJ
IN_COLS_PADDED = -(-IN_COLS // LANES) * LANES
MOD_SHIFT1, MOD_SCALE1, MOD_GATE1, MOD_SHIFT2, MOD_SCALE2, MOD_GATE2 = range(6)
PROJ_TOKENS = 512

ATTN_SCALE = HEAD_DIM ** -0.5
ATTN_Q_TILE = 256
NA_ROW_BLOCK = 8

SCAN_CHUNK = 64
SCAN_PRECISION = lax.Precision.HIGHEST


def _top_k_rows(s, k):
    rows = s.shape[0]
    iota = lax.broadcasted_iota(jnp.int32, s.shape, 0)
    vals = []
    work = s
    for _ in range(k):
        m = jnp.max(work, axis=0, keepdims=True)
        first = jnp.min(jnp.where(work == m, iota, rows), axis=0, keepdims=True)
        work = jnp.where(iota == first, -jnp.inf, work)
        vals.append(m)
    return vals, work


def _peer_select_kernel(h_ref, wq_ref, keys_ref, s1m_ref, e1_ref, s2m_ref, e2_ref, t3_ref):
    x = h_ref[...].astype(BF16)
    q = jnp.dot(x, wq_ref[...], preferred_element_type=F32)
    half = PEER_DQ // 2
    t3_rows = []
    for h in range(PEER_HEADS):
        scores, tops, sel = [], [], []
        for p in range(2):
            c0 = (2 * h + p) * half
            qhp = q[:, c0:c0 + half].astype(BF16)
            s = lax.dot_general(keys_ref[2 * h + p], qhp, (((1,), (1,)), ((), ())),
                                preferred_element_type=F32)
            vals, work = _top_k_rows(s, PEER_TOPK)
            scores.append(s)
            tops.append(vals)
            sel.append(work == -jnp.inf)
        m1, m2 = tops
        m2_all = jnp.concatenate(m2, axis=0)
        cand = jnp.concatenate([m1[i] + m2_all for i in range(8)]
                               + [jnp.concatenate(m1[8:], axis=0) + m2[0]], axis=0)
        top_s, _ = _top_k_rows(cand, PEER_TOPK)
        mx = top_s[0]
        z = top_s[0] - mx
        z = jnp.exp(z)
        for r in range(1, PEER_TOPK):
            z = z + jnp.exp(top_s[r] - mx)
        inv_z = 1.0 / z
        s1m_ref[h] = jnp.where(sel[0], scores[0], -jnp.inf)
        e1_ref[h] = jnp.where(sel[0], jnp.exp(scores[0] - m1[0]) * inv_z, 0.0)
        s2m_ref[h] = jnp.where(sel[1], scores[1], -jnp.inf)
        e2_ref[h] = jnp.exp(scores[1] - m2[0])
        t3_rows.append(top_s[PEER_TOPK - 1])
    t3_ref[...] = jnp.concatenate(t3_rows, axis=0)


def _peer_select(h2, wq_bf, keys_bf):
    n = h2.shape[0]
    tt = PEER_SEL_TOKENS
    big = jax.ShapeDtypeStruct((PEER_HEADS, PEER_NKEYS, n), F32)
    big_spec = pl.BlockSpec((PEER_HEADS, PEER_NKEYS, tt), lambda i: (0, 0, i))
    return pl.pallas_call(
        _peer_select_kernel,
        grid=(n // tt,),
        in_specs=[
            pl.BlockSpec((tt, D_MODEL), lambda i: (i, 0)),
            pl.BlockSpec((D_MODEL, PEER_HEADS * PEER_DQ), lambda i: (0, 0)),
            pl.BlockSpec((2 * PEER_HEADS, PEER_NKEYS, PEER_DQ // 2), lambda i: (0, 0, 0)),
        ],
        out_specs=[big_spec, big_spec, big_spec, big_spec,
                   pl.BlockSpec((PEER_HEADS, tt), lambda i: (0, i))],
        out_shape=[big, big, big, big, jax.ShapeDtypeStruct((PEER_HEADS, n), F32)],
        compiler_params=pltpu.CompilerParams(dimension_semantics=("arbitrary",),
                                             vmem_limit_bytes=VMEM_LIMIT_BYTES),
        name="peer_select",
    )(h2, wq_bf, keys_bf)


def _peer_dense_kernel(h_ref, u_ref, vt_ref, s1m_ref, e1_ref, s2m_ref, e2_ref, t3_ref, xres_ref, mod_ref, fin_ref,
                       out_ref, ht_ref, acc_ref, p_ref, *, final_norm):
    j = pl.program_id(1)
    tt = h_ref.shape[0]

    @pl.when(j == 0)
    def _():
        ht_ref[...] = h_ref[...].T.astype(BF16)
        acc_ref[...] = jnp.zeros_like(acc_ref)

    act = jnp.dot(u_ref[...], ht_ref[...], preferred_element_type=F32)
    for a in range(PEER_ROWS_PER_STEP):
        rows = slice(a * PEER_NKEYS, (a + 1) * PEER_NKEYS)
        for c in range(tt // LANES):
            cs = slice(c * LANES, (c + 1) * LANES)
            zz = act[rows, cs]
            gelu = 0.5 * zz * (1.0 + lax.erf(zz * INV_SQRT2))
            w = jnp.zeros((PEER_NKEYS, LANES), F32)
            for h in range(PEER_HEADS):
                cand = s1m_ref[h, a:a + 1, cs] + s2m_ref[h, :, cs]
                w = w + jnp.where(cand >= t3_ref[h:h + 1, cs], e2_ref[h, :, cs], 0.0) * e1_ref[h, a:a + 1, cs]
            p_ref[rows, cs] = (w * gelu).astype(BF16)
    acc_ref[...] += jnp.dot(vt_ref[...], p_ref[...], preferred_element_type=F32)

    @pl.when(j == PEER_STEPS - 1)
    def _():
        y = xres_ref[...] + mod_ref[0, MOD_GATE2:MOD_GATE2 + 1] * acc_ref[...].T
        if final_norm:
            y = y * lax.rsqrt(jnp.mean(y * y, axis=-1, keepdims=True) + NORM_EPS) * fin_ref[...]
        out_ref[...] = y


def _peer_dense(h2, u_bf, vt_bf, sel, xres, mod, tokens_per_cond, final_gain):
    n = h2.shape[0]
    tt = PEER_TOKENS
    s1m, e1, s2m, e2, t3 = sel
    fin = jnp.ones((1, D_MODEL), F32) if final_gain is None else final_gain.astype(F32)[None]
    row_spec = pl.BlockSpec((PEER_HEADS, PEER_ROWS_PER_STEP, tt), lambda i, j: (0, j, i))
    col_spec = pl.BlockSpec((PEER_HEADS, PEER_NKEYS, tt), lambda i, j: (0, 0, i))
    tok_spec = pl.BlockSpec((tt, D_MODEL), lambda i, j: (i, 0))
    return pl.pallas_call(
        functools.partial(_peer_dense_kernel, final_norm=final_gain is not None),
        grid=(n // tt, PEER_STEPS),
        in_specs=[
            tok_spec,
            pl.BlockSpec((PEER_CHUNK, D_MODEL), lambda i, j: (j, 0)),
            pl.BlockSpec((D_MODEL, PEER_CHUNK), lambda i, j: (0, j)),
            row_spec, row_spec, col_spec, col_spec,
            pl.BlockSpec((PEER_HEADS, tt), lambda i, j: (0, i)),
            tok_spec,
            _mod_spec(mod, tokens_per_cond, tt),
            pl.BlockSpec((1, D_MODEL), lambda i, j: (0, 0)),
        ],
        out_specs=tok_spec,
        out_shape=jax.ShapeDtypeStruct((n, D_MODEL), F32),
        scratch_shapes=[pltpu.VMEM((D_MODEL, tt), BF16), pltpu.VMEM((D_MODEL, tt), F32),
                        pltpu.VMEM((PEER_CHUNK, tt), BF16)],
        compiler_params=pltpu.CompilerParams(dimension_semantics=("arbitrary", "arbitrary"),
                                             vmem_limit_bytes=VMEM_LIMIT_BYTES),
        name="peer_dense",
    )(h2, u_bf, vt_bf, s1m, e1, s2m, e2, t3, xres, mod, fin)


def _peer_residual(x, h2, mod, lp, tokens_per_cond, final_gain):
    sel = _peer_select(h2, lp["peer_wq_bf"], lp["peer_keys_bf"])
    return _peer_dense(h2, lp["peer_u_bf"], lp["peer_vt_bf"], sel, x, mod, tokens_per_cond, final_gain)


def _rms_norm(x, g):
    xf = x.astype(F32)
    y = xf * lax.rsqrt(jnp.mean(xf * xf, axis=-1, keepdims=True) + NORM_EPS)
    return (y * g.astype(F32)).astype(x.dtype)


def _l2_norm(x):
    return x * lax.rsqrt(jnp.sum(x * x, axis=-1, keepdims=True) + NORM_EPS)


def _axial_rope_tables(T):
    t = np.arange(T)
    row, col = t // GRID_W, t % GRID_W
    n_freq = HEAD_DIM // 4
    inv = ROPE_THETA ** (-2.0 * np.arange(n_freq) / (HEAD_DIM // 2))
    ang = np.concatenate([row[:, None] * inv[None], col[:, None] * inv[None]], axis=1)
    return jnp.asarray(np.cos(ang), F32), jnp.asarray(np.sin(ang), F32)


def _apply_rope(x, cos, sin):
    xf = x.astype(F32)
    half = HEAD_DIM // 2
    x1, x2 = xf[..., :half], xf[..., half:]
    c, s = cos[None, :, None, :], sin[None, :, None, :]
    return jnp.concatenate([x1 * c - x2 * s, x2 * c + x1 * s], axis=-1).astype(x.dtype)


def _attention_kernel(q_ref, k_ref, v_ref, o_ref):
    q = q_ref[0, 0].astype(BF16)
    s = lax.dot_general(q, k_ref[0, 0], (((1,), (1,)), ((), ())), preferred_element_type=F32) * ATTN_SCALE
    m = jnp.max(s, axis=-1, keepdims=True)
    p = jnp.exp(s - m)
    l = jnp.sum(p, axis=-1, keepdims=True)
    o = jnp.dot(p.astype(BF16), v_ref[0, 0], preferred_element_type=F32)
    o_ref[0, 0] = o / l


def _attention(q, k, v):
    B, hq, T, dh = q.shape
    hkv, L = k.shape[1], k.shape[2]
    group = hq // hkv
    tq = min(T, ATTN_Q_TILE)
    kv_spec = pl.BlockSpec((1, 1, L, dh), lambda b, h, i: (b, h // group, 0, 0))
    q_spec = pl.BlockSpec((1, 1, tq, dh), lambda b, h, i: (b, h, i, 0))
    return pl.pallas_call(
        _attention_kernel,
        grid=(B, hq, T // tq),
        in_specs=[q_spec, kv_spec, kv_spec],
        out_specs=q_spec,
        out_shape=jax.ShapeDtypeStruct((B, hq, T, dh), F32),
        compiler_params=pltpu.CompilerParams(dimension_semantics=("arbitrary",) * 3,
                                             vmem_limit_bytes=VMEM_LIMIT_BYTES),
        name="attention",
    )(q, k.astype(BF16), v.astype(BF16))


def _na_bias_table(bias):
    qcols = np.arange(GRID_W)
    kcols = np.arange(GRID_W)
    cstart = np.clip(qcols - NA_COLS // 2, 0, GRID_W - NA_COLS)
    valid = (kcols[None, :] >= cstart[:, None]) & (kcols[None, :] < cstart[:, None] + NA_COLS)
    dc_idx = np.clip(kcols[None, :] - qcols[:, None], -(NA_COLS - 1), NA_COLS - 1) + (NA_COLS - 1)
    dr_idx = np.arange(NA_ROWS)[:, None] + np.arange(NA_ROWS)[None, :]
    rel = bias.astype(F32)[:, dr_idx[:, :, None, None], dc_idx[None, None, :, :]]
    rel = jnp.where(valid[None, None, None], rel, NEG_INF)
    H = bias.shape[0]
    return rel.transpose(0, 1, 3, 2, 4).reshape(H, NA_ROWS, GRID_W, NA_ROWS * GRID_W)


def _na_kernel(q_ref, k_ref, v_ref, kc_ref, vc_ref, bias_ref, o_ref):
    blk = pl.program_id(2)
    n_rows = k_ref.shape[2] // GRID_W
    win = NA_ROWS * GRID_W
    for i in range(NA_ROW_BLOCK):
        r = blk * NA_ROW_BLOCK + i
        start = jnp.clip(r - NA_ROWS // 2, 0, n_rows - NA_ROWS)
        d0 = start - r + (NA_ROWS - 1)
        off = pl.multiple_of(start * GRID_W, GRID_W)
        q = q_ref[0, 0, i * GRID_W:(i + 1) * GRID_W, :].astype(BF16)
        s_loc = lax.dot_general(q, k_ref[0, 0, pl.ds(off, win), :], (((1,), (1,)), ((), ())),
                                preferred_element_type=F32) * ATTN_SCALE + bias_ref[0, d0]
        s_ctx = lax.dot_general(q, kc_ref[0, 0], (((1,), (1,)), ((), ())), preferred_element_type=F32) * ATTN_SCALE
        m = jnp.maximum(jnp.max(s_loc, axis=-1, keepdims=True), jnp.max(s_ctx, axis=-1, keepdims=True))
        p_loc = jnp.exp(s_loc - m)
        p_ctx = jnp.exp(s_ctx - m)
        l = jnp.sum(p_loc, axis=-1, keepdims=True) + jnp.sum(p_ctx, axis=-1, keepdims=True)
        o = (jnp.dot(p_loc.astype(BF16), v_ref[0, 0, pl.ds(off, win), :], preferred_element_type=F32)
             + jnp.dot(p_ctx.astype(BF16), vc_ref[0, 0], preferred_element_type=F32))
        o_ref[0, 0, i * GRID_W:(i + 1) * GRID_W, :] = o / l


def _neighbourhood_attention(q, k, v, bias, k_ctx, v_ctx):
    B, H, T, dh = q.shape
    P = k_ctx.shape[2]
    rows = T // GRID_W
    assert rows >= NA_ROWS and rows % NA_ROW_BLOCK == 0
    tq = NA_ROW_BLOCK * GRID_W
    full = lambda n: pl.BlockSpec((1, 1, n, dh), lambda b, h, i: (b, h, 0, 0))
    q_spec = pl.BlockSpec((1, 1, tq, dh), lambda b, h, i: (b, h, i, 0))
    return pl.pallas_call(
        _na_kernel,
        grid=(B, H, rows // NA_ROW_BLOCK),
        in_specs=[q_spec, full(T), full(T), full(P), full(P),
                  pl.BlockSpec((1, NA_ROWS, GRID_W, NA_ROWS * GRID_W), lambda b, h, i: (h, 0, 0, 0))],
        out_specs=q_spec,
        out_shape=jax.ShapeDtypeStruct((B, H, T, dh), F32),
        compiler_params=pltpu.CompilerParams(dimension_semantics=("arbitrary",) * 3,
                                             vmem_limit_bytes=VMEM_LIMIT_BYTES),
        name="na_attention",
    )(q, k.astype(BF16), v.astype(BF16), k_ctx.astype(BF16), v_ctx.astype(BF16), _na_bias_table(bias))


def _gqa_mixer(p, lp, ctx_kv):
    B, T, _ = p.shape
    kvw = GQA_KV_HEADS * HEAD_DIM
    q = _rms_norm(p[..., :GROUP_W].reshape(B, T, GQA_HEADS, HEAD_DIM), lp["gqa_q_norm"])
    k = _rms_norm(p[..., GROUP_W:GROUP_W + kvw].reshape(B, T, GQA_KV_HEADS, HEAD_DIM), lp["gqa_k_norm"])
    v = p[..., GROUP_W + kvw:].reshape(B, T, GQA_KV_HEADS, HEAD_DIM)
    if ctx_kv is not None:
        cos, sin = _axial_rope_tables(T)
        q = _apply_rope(q, cos, sin)
        k = _apply_rope(k, cos, sin)
    qh = q.transpose(0, 2, 1, 3)
    kh = k.transpose(0, 2, 1, 3)
    vh = v.transpose(0, 2, 1, 3)
    if ctx_kv is None:
        o = _attention(qh, kh, vh)
    else:
        k_ctx, v_ctx = ctx_kv
        o = _attention(qh, jnp.concatenate([kh, k_ctx], axis=2), jnp.concatenate([vh, v_ctx], axis=2))
    return o.transpose(0, 2, 1, 3).reshape(B, T, GROUP_W), (kh, vh)


def _dot(a, b):
    return jnp.dot(a, b, preferred_element_type=F32, precision=SCAN_PRECISION)


def _dot_nt(a, b):
    return lax.dot_general(a, b, (((1,), (1,)), ((), ())), preferred_element_type=F32, precision=SCAN_PRECISION)


def _chunk_masks(reverse):
    row = lax.broadcasted_iota(jnp.int32, (SCAN_CHUNK, SCAN_CHUNK), 0)
    col = lax.broadcasted_iota(jnp.int32, (SCAN_CHUNK, SCAN_CHUNK), 1)
    earlier = (row < col) if reverse else (row > col)
    eye = row == col
    blk16 = (row >> 4) == (col >> 4)
    blk32 = (row >> 5) == (col >> 5)
    return dict(
        strict=earlier.astype(F32), incl=(earlier | eye).astype(F32), eye=eye.astype(F32),
        d16=blk16.astype(F32), off32=(blk32 & jnp.logical_not(blk16)).astype(F32),
        off64=jnp.logical_not(blk32).astype(F32))


def _unit_triangular_inverse(n, m):
    eye = m["eye"]
    nd = n * m["d16"]
    n2 = _dot(nd, nd)
    x = eye + nd
    x = x + _dot(x, n2)
    n4 = _dot(n2, n2)
    x = x + _dot(x, n4)
    n8 = _dot(n4, n4)
    x = x + _dot(x, n8)
    x = x + _dot(x, _dot(n * m["off32"], x))
    x = x + _dot(x, _dot(n * m["off64"], x))
    return x


def _rwkv_chunk(r, lw, k, v, a, b, h0, m, reverse):
    cum = _dot(m["incl"], lw)
    last = cum[0:1] if reverse else cum[SCAN_CHUNK - 1:SCAN_CHUNK]
    g_inv = jnp.exp(-cum)
    at = a * jnp.exp(cum - lw)
    rt = r * jnp.exp(cum)
    bt = b * g_inv
    kt = k * g_inv
    to_end = jnp.exp(last - cum)
    bh = b * to_end
    kh = k * to_end
    ar = jnp.concatenate([at, rt], axis=0)
    gb = _dot_nt(ar, bt)
    gk = _dot_nt(ar, kt)
    n_ab = gb[:SCAN_CHUNK] * m["strict"]
    l_ak = gk[:SCAN_CHUNK] * m["strict"]
    m_rb = gb[SCAN_CHUNK:] * m["incl"]
    m_rk = gk[SCAN_CHUNK:] * m["incl"]
    tinv = _unit_triangular_inverse(n_ab, m)
    p1 = _dot(tinv, at)
    u0 = _dot(tinv, _dot(l_ak, v))
    p2 = rt + _dot(m_rb, p1)
    o0 = _dot(m_rb, u0) + _dot(m_rk, v)
    bh_t = _dot_nt(m["eye"], bh)
    kh_t = _dot_nt(m["eye"], kh)
    a_c = m["eye"] * jnp.exp(last) + _dot(bh_t, p1)
    g_c = _dot(bh_t, u0) + _dot(kh_t, v)
    return _dot(p2, h0) + o0, _dot(a_c, h0) + g_c


def _rwkv_scan_pallas(seqs, h0):
    B, H, T, dh = seqs[0][0].shape
    n = T // SCAN_CHUNK
    fwd = pl.BlockSpec((1, H, SCAN_CHUNK, dh), lambda b, c: (b, 0, c, 0))
    bwd = pl.BlockSpec((1, H, SCAN_CHUNK, dh), lambda b, c: (b, 0, n - 1 - c, 0))
    state_spec = pl.BlockSpec((1, 2, H, dh, dh), lambda b, c: (b, 0, 0, 0, 0))
    o_f, o_b, h_fin = pl.pallas_call(
        _rwkv_scan_kernel,
        grid=(B, n),
        in_specs=[fwd] * 6 + [bwd] * 6 + [state_spec],
        out_specs=[fwd, bwd, state_spec],
        out_shape=[jax.ShapeDtypeStruct((B, H, T, dh), F32), jax.ShapeDtypeStruct((B, H, T, dh), F32),
                   jax.ShapeDtypeStruct((B, 2, H, dh, dh), F32)],
        scratch_shapes=[pltpu.VMEM((2, H, dh, dh), F32)],
        compiler_params=pltpu.CompilerParams(dimension_semantics=("arbitrary", "arbitrary"),
                                             vmem_limit_bytes=VMEM_LIMIT_BYTES),
        name="rwkv_scan",
    )(*seqs[0], *seqs[1], h0)
    return o_f, o_b, h_fin


def _rwkv_scan_kernel(*refs):
    ins, h0_ref = refs[:12], refs[12]
    o_refs, hout_ref, h_ref = refs[13:15], refs[15], refs[16]
    c = pl.program_id(1)

    @pl.when(c == 0)
    def _():
        h_ref[...] = h0_ref[0]

    for d in range(2):
        m = _chunk_masks(reverse=(d == 1))
        r_ref, lw_ref, k_ref, v_ref, a_ref, b_ref = ins[6 * d:6 * d + 6]
        for h in range(RW_HEADS):
            o, h_new = _rwkv_chunk(r_ref[0, h], lw_ref[0, h], k_ref[0, h], v_ref[0, h], a_ref[0, h], b_ref[0, h],
                                   h_ref[d, h], m, reverse=(d == 1))
            o_refs[d][0, h] = o
            h_ref[d, h] = h_new

    @pl.when(c == pl.num_programs(1) - 1)
    def _():
        hout_ref[0] = h_ref[...]


def _rwkv_mixer(p, lp, S0):
    B, T, _ = p.shape
    W, H = GROUP_W, RW_HEADS
    o0 = 3 * W
    o1 = o0 + 2 * RW_DECAY_RANK
    o2 = o1 + 2 * RW_AAA_RANK
    rkv = p[..., :o0]
    wd = p[..., o0:o1].reshape(B, T, 2, RW_DECAY_RANK)
    ad = p[..., o1:o2].reshape(B, T, 2, RW_AAA_RANK)
    gate = jax.nn.sigmoid(p[..., o2:]) @ lp["rw_g2"]
    heads = lambda a: a.astype(F32).reshape(B, T, H, HEAD_DIM)
    hm = lambda a: jnp.swapaxes(a, 1, 2)
    seqs, bonuses = [], []
    for d in range(2):
        base = jnp.concatenate([rkv, wd[:, :, d], ad[:, :, d]], axis=-1)
        if d == 0:
            shifted = jnp.pad(base[:, :-1], ((0, 0), (1, 0), (0, 0)))
        else:
            shifted = jnp.pad(base[:, 1:], ((0, 0), (0, 1), (0, 0)))
        xd = base + (shifted - base) * lp["rw_mu"][d]
        r, k, v = xd[..., :W], xd[..., W:2 * W], xd[..., 2 * W:3 * W]
        wl, al = xd[..., 3 * W:3 * W + RW_DECAY_RANK], xd[..., 3 * W + RW_DECAY_RANK:]
        w = -jax.nn.softplus(-(lp["rw_w0"][d] + jnp.tanh(wl) @ lp["rw_w2"][d])) - 0.5
        log_decay = -jnp.exp(w.astype(F32))
        a = jax.nn.sigmoid((lp["rw_a0"][d] + al @ lp["rw_a2"][d]).astype(F32))
        kf = k.astype(F32)
        kk = _l2_norm(heads(kf * lp["rw_k_k"].astype(F32)))
        k_eff = heads(kf * (1.0 + (a - 1.0) * lp["rw_k_a"].astype(F32)))
        rh, vh, ah = heads(r), heads(v), heads(a)
        seqs.append([hm(rh), hm(heads(log_decay)), hm(k_eff), hm(vh), hm(-kk), hm(kk * ah)])
        bonuses.append(jnp.sum(rh * k_eff * lp["rw_r_k"].astype(F32), axis=-1, keepdims=True) * vh)
    o_f, o_b, h_fin = _rwkv_scan_pallas(seqs, jnp.swapaxes(S0.astype(F32), -1, -2))
    o = hm(o_f + o_b)
    finals = jnp.swapaxes(h_fin, -1, -2)
    mu = jnp.mean(o, axis=-1, keepdims=True)
    var = jnp.mean(jnp.square(o - mu), axis=-1, keepdims=True)
    y = ((o - mu) * lax.rsqrt(var + RW_LN_EPS)).reshape(B, T, W) * lp["rw_ln_g"].astype(F32) + lp["rw_ln_b"].astype(F32)
    y = y + (bonuses[0] + bonuses[1]).reshape(B, T, W)
    return y.astype(p.dtype) * gate, finals


def _na_mixer(p, lp, ctx_kv):
    B, T, _ = p.shape
    q, k, v = [p[..., i * GROUP_W:(i + 1) * GROUP_W].reshape(B, T, NA_HEADS, HEAD_DIM).transpose(0, 2, 1, 3)
               for i in range(3)]
    if ctx_kv is None:
        o = _attention(q, k, v)
    else:
        o = _neighbourhood_attention(q, k, v, lp["na_bias"], ctx_kv[0], ctx_kv[1])
    return o.transpose(0, 2, 1, 3).reshape(B, T, GROUP_W), (k, v)


def _delta_chunk(q, k, v, gb, betab, s0, m, reverse):
    gc = _dot(m["incl"], gb)
    last = gc[0:1] if reverse else gc[SCAN_CHUNK - 1:SCAN_CHUNK]
    diff = gc - _dot_nt(m["eye"], gc)
    keep = m["incl"] > 0.0
    dmat = jnp.where(keep, jnp.exp(jnp.where(keep, diff, 0.0)), 0.0)
    kb = k * betab
    gram = _dot_nt(jnp.concatenate([kb, q], axis=0), k)
    a_mat = gram[:SCAN_CHUNK] * dmat * m["strict"]
    qk = gram[SCAN_CHUNK:] * dmat
    tm = _unit_triangular_inverse(-a_mat, m)
    u = _dot(tm, v * betab)
    w = _dot(tm, kb * jnp.exp(gc))
    v_new = u - _dot(w, s0)
    o = _dot(q * jnp.exp(gc), s0) + _dot(qk, v_new)
    kd_t = _dot_nt(m["eye"], k * jnp.exp(last - gc))
    return o, s0 * jnp.exp(last) + _dot(kd_t, v_new)


def _delta_scan_kernel(*refs):
    ins, s0_ref = refs[:10], refs[10]
    o_refs, sout_ref, s_ref = refs[11:13], refs[13], refs[14]
    c = pl.program_id(1)

    @pl.when(c == 0)
    def _():
        s_ref[...] = s0_ref[0]

    for d in range(2):
        m = _chunk_masks(reverse=(d == 1))
        q_ref, k_ref, v_ref, g_ref, b_ref = ins[5 * d:5 * d + 5]
        for h in range(DN_HEADS):
            o, s_new = _delta_chunk(q_ref[0, h], k_ref[0, h], v_ref[0, h], g_ref[0, h], b_ref[0, h], s_ref[d, h], m,
                                    reverse=(d == 1))
            o_refs[d][0, h] = o
            s_ref[d, h] = s_new

    @pl.when(c == pl.num_programs(1) - 1)
    def _():
        sout_ref[0] = s_ref[...]


def _delta_scan_pallas(q, k, v, g, beta, s0):
    assert SCAN_CHUNK == HEAD_DIM
    B, H, T, dh = q.shape
    n = T // SCAN_CHUNK
    fwd = pl.BlockSpec((1, H, SCAN_CHUNK, dh), lambda b, c: (b, 0, c, 0))
    bwd = pl.BlockSpec((1, H, SCAN_CHUNK, dh), lambda b, c: (b, 0, n - 1 - c, 0))
    state_spec = pl.BlockSpec((1, 2, H, dh, dh), lambda b, c: (b, 0, 0, 0, 0))
    seq = jax.ShapeDtypeStruct((B, H, T, dh), F32)
    return pl.pallas_call(
        _delta_scan_kernel,
        grid=(B, n),
        in_specs=[fwd] * 5 + [bwd] * 5 + [state_spec],
        out_specs=[fwd, bwd, state_spec],
        out_shape=[seq, seq, jax.ShapeDtypeStruct((B, 2, H, dh, dh), F32)],
        scratch_shapes=[pltpu.VMEM((2, H, dh, dh), F32)],
        compiler_params=pltpu.CompilerParams(dimension_semantics=("arbitrary", "arbitrary"),
                                             vmem_limit_bytes=VMEM_LIMIT_BYTES),
        name="delta_scan",
    )(q, k, v, g[:, 0], beta[:, 0], q, k, v, g[:, 1], beta[:, 1], s0)


def _deltanet_mixer(p, lp, S0):
    B, T, _ = p.shape
    W, H = GROUP_W, DN_HEADS
    xin = jnp.pad(p[..., :3 * W], ((0, 0), (DN_CONV // 2, DN_CONV // 2), (0, 0)))
    qkv = sum(xin[:, j:j + T] * lp["dn_conv"][j] for j in range(DN_CONV))
    qkv = jax.nn.silu(qkv)
    heads = lambda a: a.astype(F32).reshape(B, T, H, HEAD_DIM)
    hm = lambda a: jnp.swapaxes(a, 1, 2)
    q = _l2_norm(heads(qkv[..., :W])) * (HEAD_DIM ** -0.5)
    k = _l2_norm(heads(qkv[..., W:2 * W]))
    v = heads(qkv[..., 2 * W:])
    beta = jax.nn.sigmoid(p[..., 3 * W:3 * W + 2 * H].astype(F32)).reshape(B, T, 2, H)
    alpha = p[..., 3 * W + 2 * H:3 * W + 4 * H].astype(F32).reshape(B, T, 2, H)
    g = -jnp.exp(lp["dn_a_log"].astype(F32)) * jax.nn.softplus(alpha + lp["dn_dt_bias"].astype(F32))
    z = heads(p[..., 3 * W + 4 * H:])
    lanes = lambda a: jnp.broadcast_to(a.transpose(0, 2, 3, 1)[..., None], (B, 2, H, T, HEAD_DIM))
    o_f, o_b, s_fin = _delta_scan_pallas(hm(q), hm(k), hm(v), lanes(g), lanes(beta), S0.astype(F32))
    o = _rms_norm(hm(o_f + o_b), lp["dn_norm_g"]) * jax.nn.silu(z)
    return o.reshape(B, T, W).astype(p.dtype), s_fin


def _modulated_norm(x, gain, scale, shift):
    y = x * lax.rsqrt(jnp.mean(x * x, axis=-1, keepdims=True) + NORM_EPS)
    return (y * gain) * (1.0 + scale) + shift


def _proj_in_kernel(x_ref, mod_ref, gain_ref, w_ref, o_ref):
    h = _modulated_norm(x_ref[...], gain_ref[...], mod_ref[0, MOD_SCALE1:MOD_SCALE1 + 1], mod_ref[0, MOD_SHIFT1:MOD_SHIFT1 + 1])
    o_ref[...] = jnp.dot(h.astype(BF16), w_ref[...], preferred_element_type=F32)


def _proj_out_kernel(mix_ref, x_ref, mod_ref, gain_ref, w_ref, x_out_ref, h2_ref):
    y = jnp.dot(mix_ref[...].astype(BF16), w_ref[...], preferred_element_type=F32)
    x_new = x_ref[...] + mod_ref[0, MOD_GATE1:MOD_GATE1 + 1] * y
    x_out_ref[...] = x_new
    h2_ref[...] = _modulated_norm(x_new, gain_ref[...], mod_ref[0, MOD_SCALE2:MOD_SCALE2 + 1],
                                  mod_ref[0, MOD_SHIFT2:MOD_SHIFT2 + 1])


def _mod_spec(mod, tokens_per_cond, tile):
    tiles_per_cond = tokens_per_cond // tile
    if mod.shape[0] == 1:
        return pl.BlockSpec((1, 6, D_MODEL), lambda i, *_: (0, 0, 0))
    assert tokens_per_cond % tile == 0
    return pl.BlockSpec((1, 6, D_MODEL), lambda i, *_: (i // tiles_per_cond, 0, 0))


def _proj_in(x, mod, gain, w_bf, tokens_per_cond):
    n, cols = x.shape[0], w_bf.shape[1]
    tm = PROJ_TOKENS
    tok = pl.BlockSpec((tm, D_MODEL), lambda i: (i, 0))
    return pl.pallas_call(
        _proj_in_kernel,
        grid=(n // tm,),
        in_specs=[tok, _mod_spec(mod, tokens_per_cond, tm), pl.BlockSpec((1, D_MODEL), lambda i: (0, 0)),
                  pl.BlockSpec((D_MODEL, cols), lambda i: (0, 0))],
        out_specs=pl.BlockSpec((tm, cols), lambda i: (i, 0)),
        out_shape=jax.ShapeDtypeStruct((n, cols), F32),
        compiler_params=pltpu.CompilerParams(dimension_semantics=("arbitrary",), vmem_limit_bytes=VMEM_LIMIT_BYTES),
        name="proj_in",
    )(x, mod, gain, w_bf)


def _proj_out(mix, x, mod, gain, w_bf, tokens_per_cond):
    n = x.shape[0]
    tm = PROJ_TOKENS
    tok = pl.BlockSpec((tm, D_MODEL), lambda i: (i, 0))
    out = jax.ShapeDtypeStruct((n, D_MODEL), F32)
    return pl.pallas_call(
        _proj_out_kernel,
        grid=(n // tm,),
        in_specs=[tok, tok, _mod_spec(mod, tokens_per_cond, tm), pl.BlockSpec((1, D_MODEL), lambda i: (0, 0)),
                  pl.BlockSpec((D_MODEL, D_MODEL), lambda i: (0, 0))],
        out_specs=[tok, tok],
        out_shape=[out, out],
        compiler_params=pltpu.CompilerParams(dimension_semantics=("arbitrary",), vmem_limit_bytes=VMEM_LIMIT_BYTES),
        name="proj_out",
    )(mix, x, mod, gain, w_bf)


def _trunk_layer(x, cond, lp, ctx, final_gain):
    B, T, _ = x.shape
    mod = (jax.nn.silu(cond) @ lp["w_mod"] + lp["b_mod"]).reshape(-1, 6, D_MODEL)
    x = x.reshape(B * T, D_MODEL)
    proj = _proj_in(x, mod, lp["norm1_g"][None], lp["w_in_bf"], T).reshape(B, T, -1)
    pa, pb, pc, pd, _ = jnp.split(proj, [GQA_PROJ, GQA_PROJ + RW_PROJ, GQA_PROJ + RW_PROJ + NA_PROJ, IN_COLS], axis=-1)
    if ctx is None:
        ctx_a = None
        ctx_c = None
        s_rw0 = jnp.zeros((B, 2, RW_HEADS, HEAD_DIM, HEAD_DIM), F32)
        s_dn0 = jnp.zeros((B, 2, DN_HEADS, HEAD_DIM, HEAD_DIM), F32)
    else:
        ka_c, va_c, kc_c, vc_c, s_rw0, s_dn0 = ctx
        ctx_a = (ka_c, va_c)
        ctx_c = (kc_c, vc_c)
    oa, (ka, va) = _gqa_mixer(pa, lp, ctx_a)
    ob, s_rw = _rwkv_mixer(pb, lp, s_rw0)
    oc, (kc, vc) = _na_mixer(pc, lp, ctx_c)
    od, s_dn = _deltanet_mixer(pd, lp, s_dn0)
    mix = jnp.concatenate([oa, ob, oc, od], axis=-1).reshape(B * T, D_MODEL)
    x, h2 = _proj_out(mix, x, mod, lp["norm2_g"][None], lp["w_out_bf"], T)
    x = _peer_residual(x, h2, mod, lp, T, final_gain)
    return x.reshape(B, T, D_MODEL), (ka, va, kc, vc, s_rw, s_dn)


def kernel(x_prompt, x_sample, c, cache_gqa_k, cache_gqa_v, cache_na_k, cache_na_v, state_rwkv, state_delta, c_ctx, norm1_g, norm2_g, w_mod, b_mod, w_in, w_out, gqa_q_norm, gqa_k_norm, rw_mu, rw_w0, rw_w2, rw_a0, rw_a2, rw_g2, rw_k_k, rw_k_a, rw_r_k, rw_ln_g, rw_ln_b, na_bias, dn_conv, dn_a_log, dn_dt_bias, dn_norm_g, peer_wq, peer_keys, peer_u, peer_v, final_norm_g):
    depth = w_in.shape[0]

    def layer_params(l):
        return {
            "norm1_g": norm1_g[l], "norm2_g": norm2_g[l], "w_mod": w_mod[l], "b_mod": b_mod[l],
            "w_in_bf": jnp.pad(w_in[l].astype(BF16), ((0, 0), (0, IN_COLS_PADDED - IN_COLS))),
            "w_out_bf": w_out[l].astype(BF16), "gqa_q_norm": gqa_q_norm[l], "gqa_k_norm": gqa_k_norm[l],
            "rw_mu": rw_mu[l], "rw_w0": rw_w0[l], "rw_w2": rw_w2[l], "rw_a0": rw_a0[l], "rw_a2": rw_a2[l],
            "rw_g2": rw_g2[l], "rw_k_k": rw_k_k[l], "rw_k_a": rw_k_a[l], "rw_r_k": rw_r_k[l],
            "rw_ln_g": rw_ln_g[l], "rw_ln_b": rw_ln_b[l], "na_bias": na_bias[l],
            "dn_conv": dn_conv[l], "dn_a_log": dn_a_log[l], "dn_dt_bias": dn_dt_bias[l], "dn_norm_g": dn_norm_g[l],
            "peer_wq_bf": peer_wq[l].astype(BF16),
            "peer_keys_bf": peer_keys[l].reshape(2 * PEER_HEADS, PEER_NKEYS, PEER_DQ // 2).astype(BF16),
            "peer_u_bf": peer_u[l].astype(BF16),
            "peer_vt_bf": peer_v[l].astype(BF16).T,
        }

    params = [layer_params(l) for l in range(depth)]

    xp = x_prompt
    cond_ctx = c_ctx[None, :]
    per_layer = []
    closing = lambda l: final_norm_g if l == depth - 1 else None
    for l in range(depth):
        xp, st = _trunk_layer(xp, cond_ctx, params[l], None, closing(l))
        per_layer.append(st)
    y_prompt = xp

    xs = x_sample
    for l in range(depth):
        ctx = (cache_gqa_k[:, l], cache_gqa_v[:, l], cache_na_k[:, l], cache_na_v[:, l],
               state_rwkv[:, l], state_delta[:, l])
        xs, _ = _trunk_layer(xs, c, params[l], ctx, closing(l))
    y_sample = xs

    dt = x_prompt.dtype
    new_gqa_k = jnp.stack([st[0] for st in per_layer], axis=1).astype(dt)
    new_gqa_v = jnp.stack([st[1] for st in per_layer], axis=1).astype(dt)
    new_na_k = jnp.stack([st[2] for st in per_layer], axis=1).astype(dt)
    new_na_v = jnp.stack([st[3] for st in per_layer], axis=1).astype(dt)
    new_state_rwkv = jnp.stack([st[4] for st in per_layer], axis=1).astype(dt)
    new_state_delta = jnp.stack([st[5] for st in per_layer], axis=1).astype(dt)
    return (y_prompt, y_sample, new_gqa_k, new_gqa_v, new_na_k, new_na_v, new_state_rwkv, new_state_delta)
```

```python
import functools
import math

import jax
import jax.numpy as jnp
import numpy as np
from jax import lax
from jax.experimental import pallas as pl
from jax.experimental.pallas import tpu as pltpu

F32 = jnp.float32
BF16 = jnp.bfloat16

D_MODEL = 1024
GRID_W = 64
HEAD_DIM = 64
GROUP_W = D_MODEL // 4
NORM_EPS = 1e-6
NEG_INF = -1e30
GQA_HEADS = GROUP_W // HEAD_DIM
GQA_KV_HEADS = GQA_HEADS // 2
GQA_GROUP = GQA_HEADS // GQA_KV_HEADS
ROPE_THETA = 10000.0
Q_BLOCK = 128
RW_HEADS = GROUP_W // HEAD_DIM
RW_DECAY_RANK = 32
RW_AAA_RANK = 32
RW_GATE_RANK = 64
RW_LN_EPS = 64e-5
NA_HEADS = GROUP_W // HEAD_DIM
NA_ROWS = 8
NA_COLS = 16
NA_QCOLS = 16
NA_SPAN = NA_QCOLS + NA_COLS
DN_HEADS = GROUP_W // HEAD_DIM
DN_CONV = 5
DN_CHUNK = 64
PEER_HEADS = 8
PEER_NKEYS = 128
PEER_EXPERTS = PEER_NKEYS * PEER_NKEYS
PEER_DQ = 256
PEER_TOPK = 16
GQA_PROJ = GROUP_W + 2 * GQA_KV_HEADS * HEAD_DIM
RW_PROJ = 3 * GROUP_W + 2 * RW_DECAY_RANK + 2 * RW_AAA_RANK + RW_GATE_RANK
NA_PROJ = 3 * GROUP_W
DN_PROJ = 3 * GROUP_W + 4 * DN_HEADS + GROUP_W

LANES = 128
VMEM_LIMIT_BYTES = 56 * 1024 * 1024

PEER_SEL_TOKENS = 128
PEER_TOKENS = 256
PEER_ROWS_PER_STEP = 16
PEER_SUB_ROWS = 2
PEER_CHUNK = PEER_ROWS_PER_STEP * PEER_NKEYS
PEER_STEPS = PEER_EXPERTS // PEER_CHUNK
INV_SQRT2 = 1.0 / math.sqrt(2.0)

IN_COLS = GQA_PROJ + RW_PROJ + NA_PROJ + DN_PROJ
IN_COLS_PADDED = -(-IN_COLS // LANES) * LANES
MOD_SHIFT1, MOD_SCALE1, MOD_GATE1, MOD_SHIFT2, MOD_SCALE2, MOD_GATE2 = range(6)
PROJ_TOKENS = 512

ATTN_SCALE = HEAD_DIM ** -0.5
ATTN_Q_TILE = 256
NA_ROW_BLOCK = 8

SCAN_CHUNK = 64


def _top_k_rows(s, k):
    rows = s.shape[0]
    iota = lax.broadcasted_iota(jnp.int32, s.shape, 0)
    vals = []
    work = s
    for _ in range(k):
        m = jnp.max(work, axis=0, keepdims=True)
        first = jnp.min(jnp.where(work == m, iota, rows), axis=0, keepdims=True)
        work = jnp.where(iota == first, -jnp.inf, work)
        vals.append(m)
    return vals, work


def _peer_select_kernel(h_ref, wq_ref, keys_ref, s1m_ref, e1_ref, s2m_ref, e2_ref, t3_ref):
    x = h_ref[...].astype(BF16)
    q = jnp.dot(x, wq_ref[...], preferred_element_type=F32)
    half = PEER_DQ // 2
    t3_rows = []
    for h in range(PEER_HEADS):
        scores, tops, sel = [], [], []
        for p in range(2):
            c0 = (2 * h + p) * half
            qhp = q[:, c0:c0 + half].astype(BF16)
            s = lax.dot_general(keys_ref[2 * h + p], qhp, (((1,), (1,)), ((), ())),
                                preferred_element_type=F32)
            vals, work = _top_k_rows(s, PEER_TOPK)
            scores.append(s)
            tops.append(vals)
            sel.append(work == -jnp.inf)
        m1, m2 = tops
        m2_all = jnp.concatenate(m2, axis=0)
        cand = jnp.concatenate([m1[i] + m2_all for i in range(8)]
                               + [jnp.concatenate(m1[8:], axis=0) + m2[0]], axis=0)
        top_s, _ = _top_k_rows(cand, PEER_TOPK)
        mx = top_s[0]
        z = top_s[0] - mx
        z = jnp.exp(z)
        for r in range(1, PEER_TOPK):
            z = z + jnp.exp(top_s[r] - mx)
        inv_z = 1.0 / z
        s1m_ref[h] = jnp.where(sel[0], scores[0], -jnp.inf)
        e1_ref[h] = jnp.where(sel[0], jnp.exp(scores[0] - m1[0]) * inv_z, 0.0)
        s2m_ref[h] = jnp.where(sel[1], scores[1], -jnp.inf)
        e2_ref[h] = jnp.exp(scores[1] - m2[0])
        t3_rows.append(top_s[PEER_TOPK - 1])
    t3_ref[...] = jnp.concatenate(t3_rows, axis=0)


def _peer_select(h2, wq_bf, keys_bf):
    n = h2.shape[0]
    tt = PEER_SEL_TOKENS
    big = jax.ShapeDtypeStruct((PEER_HEADS, PEER_NKEYS, n), F32)
    big_spec = pl.BlockSpec((PEER_HEADS, PEER_NKEYS, tt), lambda i: (0, 0, i))
    return pl.pallas_call(
        _peer_select_kernel,
        grid=(n // tt,),
        in_specs=[
            pl.BlockSpec((tt, D_MODEL), lambda i: (i, 0)),
            pl.BlockSpec((D_MODEL, PEER_HEADS * PEER_DQ), lambda i: (0, 0)),
            pl.BlockSpec((2 * PEER_HEADS, PEER_NKEYS, PEER_DQ // 2), lambda i: (0, 0, 0)),
        ],
        out_specs=[big_spec, big_spec, big_spec, big_spec,
                   pl.BlockSpec((PEER_HEADS, tt), lambda i: (0, i))],
        out_shape=[big, big, big, big, jax.ShapeDtypeStruct((PEER_HEADS, n), F32)],
        compiler_params=pltpu.CompilerParams(dimension_semantics=("arbitrary",),
                                             vmem_limit_bytes=VMEM_LIMIT_BYTES),
        name="peer_select",
    )(h2, wq_bf, keys_bf)


def _peer_dense_kernel(h_ref, u_ref, vt_ref, s1m_ref, e1_ref, s2m_ref, e2_ref, t3_ref, xres_ref, mod_ref, fin_ref,
                       out_ref, ht_ref, acc_ref, p_ref, *, final_norm):
    j = pl.program_id(1)
    tt = h_ref.shape[0]

    @pl.when(j == 0)
    def _():
        ht_ref[...] = h_ref[...].T.astype(BF16)
        acc_ref[...] = jnp.zeros_like(acc_ref)

    sub = PEER_SUB_ROWS * PEER_NKEYS

    def activations(k):
        return jnp.dot(u_ref[k * sub:(k + 1) * sub, :], ht_ref[...], preferred_element_type=F32)

    def weighted(k, act, slot):
        for ai in range(PEER_SUB_ROWS):
            a = k * PEER_SUB_ROWS + ai
            rows = slice(ai * PEER_NKEYS, (ai + 1) * PEER_NKEYS)
            for c in range(tt // LANES):
                cs = slice(c * LANES, (c + 1) * LANES)
                zz = act[rows, cs]
                gelu = 0.5 * zz * (1.0 + lax.erf(zz * INV_SQRT2))
                w = jnp.zeros((PEER_NKEYS, LANES), F32)
                for h in range(PEER_HEADS):
                    cand = s1m_ref[h, a:a + 1, cs] + s2m_ref[h, :, cs]
                    w = w + jnp.where(cand >= t3_ref[h:h + 1, cs], e2_ref[h, :, cs], 0.0) * e1_ref[h, a:a + 1, cs]
                p_ref[slot, rows, cs] = (w * gelu).astype(BF16)

    n_sub = PEER_ROWS_PER_STEP // PEER_SUB_ROWS
    act = activations(0)
    for k in range(n_sub):
        nxt = activations(k + 1) if k + 1 < n_sub else None
        weighted(k, act, k % 2)
        acc_ref[...] += jnp.dot(vt_ref[:, k * sub:(k + 1) * sub], p_ref[k % 2], preferred_element_type=F32)
        act = nxt

    @pl.when(j == PEER_STEPS - 1)
    def _():
        y = xres_ref[...] + mod_ref[0, MOD_GATE2:MOD_GATE2 + 1] * acc_ref[...].T
        if final_norm:
            y = y * lax.rsqrt(jnp.mean(y * y, axis=-1, keepdims=True) + NORM_EPS) * fin_ref[...]
        out_ref[...] = y


def _peer_dense(h2, u_bf, vt_bf, sel, xres, mod, tokens_per_cond, final_gain):
    n = h2.shape[0]
    tt = PEER_TOKENS
    s1m, e1, s2m, e2, t3 = sel
    fin = jnp.ones((1, D_MODEL), F32) if final_gain is None else final_gain.astype(F32)[None]
    row_spec = pl.BlockSpec((PEER_HEADS, PEER_ROWS_PER_STEP, tt), lambda i, j: (0, j, i))
    col_spec = pl.BlockSpec((PEER_HEADS, PEER_NKEYS, tt), lambda i, j: (0, 0, i))
    tok_spec = pl.BlockSpec((tt, D_MODEL), lambda i, j: (i, 0))
    return pl.pallas_call(
        functools.partial(_peer_dense_kernel, final_norm=final_gain is not None),
        grid=(n // tt, PEER_STEPS),
        in_specs=[
            tok_spec,
            pl.BlockSpec((PEER_CHUNK, D_MODEL), lambda i, j: (j, 0)),
            pl.BlockSpec((D_MODEL, PEER_CHUNK), lambda i, j: (0, j)),
            row_spec, row_spec, col_spec, col_spec,
            pl.BlockSpec((PEER_HEADS, tt), lambda i, j: (0, i)),
            tok_spec,
            _mod_spec(mod, tokens_per_cond, tt),
            pl.BlockSpec((1, D_MODEL), lambda i, j: (0, 0)),
        ],
        out_specs=tok_spec,
        out_shape=jax.ShapeDtypeStruct((n, D_MODEL), F32),
        scratch_shapes=[pltpu.VMEM((D_MODEL, tt), BF16), pltpu.VMEM((D_MODEL, tt), F32),
                        pltpu.VMEM((2, PEER_SUB_ROWS * PEER_NKEYS, tt), BF16)],
        compiler_params=pltpu.CompilerParams(dimension_semantics=("arbitrary", "arbitrary"),
                                             vmem_limit_bytes=VMEM_LIMIT_BYTES),
        name="peer_dense",
    )(h2, u_bf, vt_bf, s1m, e1, s2m, e2, t3, xres, mod, fin)


def _peer_residual(x, h2, mod, lp, tokens_per_cond, final_gain):
    sel = _peer_select(h2, lp["peer_wq_bf"], lp["peer_keys_bf"])
    return _peer_dense(h2, lp["peer_u_bf"], lp["peer_vt_bf"], sel, x, mod, tokens_per_cond, final_gain)


def _rms_norm(x, g):
    xf = x.astype(F32)
    y = xf * lax.rsqrt(jnp.mean(xf * xf, axis=-1, keepdims=True) + NORM_EPS)
    return (y * g.astype(F32)).astype(x.dtype)


def _l2_norm(x):
    return x * lax.rsqrt(jnp.sum(x * x, axis=-1, keepdims=True) + NORM_EPS)


def _axial_rope_tables(T):
    t = np.arange(T)
    row, col = t // GRID_W, t % GRID_W
    n_freq = HEAD_DIM // 4
    inv = ROPE_THETA ** (-2.0 * np.arange(n_freq) / (HEAD_DIM // 2))
    ang = np.concatenate([row[:, None] * inv[None], col[:, None] * inv[None]], axis=1)
    return jnp.asarray(np.cos(ang), F32), jnp.asarray(np.sin(ang), F32)


def _apply_rope(x, cos, sin):
    xf = x.astype(F32)
    half = HEAD_DIM // 2
    x1, x2 = xf[..., :half], xf[..., half:]
    c, s = cos[None, :, None, :], sin[None, :, None, :]
    return jnp.concatenate([x1 * c - x2 * s, x2 * c + x1 * s], axis=-1).astype(x.dtype)


def _attention_kernel(q_ref, k_ref, v_ref, o_ref):
    q = q_ref[0, 0].astype(BF16)
    s = lax.dot_general(q, k_ref[0, 0], (((1,), (1,)), ((), ())), preferred_element_type=F32) * ATTN_SCALE
    m = jnp.max(s, axis=-1, keepdims=True)
    p = jnp.exp(s - m)
    l = jnp.sum(p, axis=-1, keepdims=True)
    o = jnp.dot(p.astype(BF16), v_ref[0, 0], preferred_element_type=F32)
    o_ref[0, 0] = o / l


def _attention(q, k, v):
    B, hq, T, dh = q.shape
    hkv, L = k.shape[1], k.shape[2]
    group = hq // hkv
    tq = min(T, ATTN_Q_TILE)
    kv_spec = pl.BlockSpec((1, 1, L, dh), lambda b, h, i: (b, h // group, 0, 0))
    q_spec = pl.BlockSpec((1, 1, tq, dh), lambda b, h, i: (b, h, i, 0))
    return pl.pallas_call(
        _attention_kernel,
        grid=(B, hq, T // tq),
        in_specs=[q_spec, kv_spec, kv_spec],
        out_specs=q_spec,
        out_shape=jax.ShapeDtypeStruct((B, hq, T, dh), F32),
        compiler_params=pltpu.CompilerParams(dimension_semantics=("arbitrary",) * 3,
                                             vmem_limit_bytes=VMEM_LIMIT_BYTES),
        name="attention",
    )(q, k.astype(BF16), v.astype(BF16))


def _na_bias_table(bias):
    qcols = np.arange(GRID_W)
    kcols = np.arange(GRID_W)
    cstart = np.clip(qcols - NA_COLS // 2, 0, GRID_W - NA_COLS)
    valid = (kcols[None, :] >= cstart[:, None]) & (kcols[None, :] < cstart[:, None] + NA_COLS)
    dc_idx = np.clip(kcols[None, :] - qcols[:, None], -(NA_COLS - 1), NA_COLS - 1) + (NA_COLS - 1)
    onehot = jnp.asarray(dc_idx[:, :, None] == np.arange(2 * NA_COLS - 1), F32)
    rel = jnp.einsum("qkc,hrc->hrqk", onehot, bias.astype(F32), precision=lax.Precision.HIGHEST)
    rel = jnp.where(valid[None, None], rel, NEG_INF)
    return jnp.stack([jnp.concatenate([rel[:, d0 + j] for j in range(NA_ROWS)], axis=-1)
                      for d0 in range(NA_ROWS)], axis=1)


def _na_kernel(q_ref, k_ref, v_ref, kc_ref, vc_ref, bias_ref, o_ref):
    blk = pl.program_id(2)
    n_rows = k_ref.shape[2] // GRID_W
    win = NA_ROWS * GRID_W
    for i in range(NA_ROW_BLOCK):
        r = blk * NA_ROW_BLOCK + i
        start = jnp.clip(r - NA_ROWS // 2, 0, n_rows - NA_ROWS)
        d0 = start - r + (NA_ROWS - 1)
        off = pl.multiple_of(start * GRID_W, GRID_W)
        q = q_ref[0, 0, i * GRID_W:(i + 1) * GRID_W, :].astype(BF16)
        s_loc = lax.dot_general(q, k_ref[0, 0, pl.ds(off, win), :], (((1,), (1,)), ((), ())),
                                preferred_element_type=F32) * ATTN_SCALE + bias_ref[0, d0]
        s_ctx = lax.dot_general(q, kc_ref[0, 0], (((1,), (1,)), ((), ())), preferred_element_type=F32) * ATTN_SCALE
        m = jnp.maximum(jnp.max(s_loc, axis=-1, keepdims=True), jnp.max(s_ctx, axis=-1, keepdims=True))
        p_loc = jnp.exp(s_loc - m)
        p_ctx = jnp.exp(s_ctx - m)
        l = jnp.sum(p_loc, axis=-1, keepdims=True) + jnp.sum(p_ctx, axis=-1, keepdims=True)
        o = (jnp.dot(p_loc.astype(BF16), v_ref[0, 0, pl.ds(off, win), :], preferred_element_type=F32)
             + jnp.dot(p_ctx.astype(BF16), vc_ref[0, 0], preferred_element_type=F32))
        o_ref[0, 0, i * GRID_W:(i + 1) * GRID_W, :] = o / l


def _neighbourhood_attention(q, k, v, bias, k_ctx, v_ctx):
    B, H, T, dh = q.shape
    P = k_ctx.shape[2]
    rows = T // GRID_W
    assert rows >= NA_ROWS and rows % NA_ROW_BLOCK == 0
    tq = NA_ROW_BLOCK * GRID_W
    full = lambda n: pl.BlockSpec((1, 1, n, dh), lambda b, h, i: (b, h, 0, 0))
    q_spec = pl.BlockSpec((1, 1, tq, dh), lambda b, h, i: (b, h, i, 0))
    return pl.pallas_call(
        _na_kernel,
        grid=(B, H, rows // NA_ROW_BLOCK),
        in_specs=[q_spec, full(T), full(T), full(P), full(P),
                  pl.BlockSpec((1, NA_ROWS, GRID_W, NA_ROWS * GRID_W), lambda b, h, i: (h, 0, 0, 0))],
        out_specs=q_spec,
        out_shape=jax.ShapeDtypeStruct((B, H, T, dh), F32),
        compiler_params=pltpu.CompilerParams(dimension_semantics=("arbitrary",) * 3,
                                             vmem_limit_bytes=VMEM_LIMIT_BYTES),
        name="na_attention",
    )(q, k.astype(BF16), v.astype(BF16), k_ctx.astype(BF16), v_ctx.astype(BF16), _na_bias_table(bias))


def _gqa_mixer(p, lp, ctx_kv):
    B, T, _ = p.shape
    kvw = GQA_KV_HEADS * HEAD_DIM
    q = _rms_norm(p[..., :GROUP_W].reshape(B, T, GQA_HEADS, HEAD_DIM), lp["gqa_q_norm"])
    k = _rms_norm(p[..., GROUP_W:GROUP_W + kvw].reshape(B, T, GQA_KV_HEADS, HEAD_DIM), lp["gqa_k_norm"])
    v = p[..., GROUP_W + kvw:].reshape(B, T, GQA_KV_HEADS, HEAD_DIM)
    if ctx_kv is not None:
        cos, sin = _axial_rope_tables(T)
        q = _apply_rope(q, cos, sin)
        k = _apply_rope(k, cos, sin)
    qh = q.transpose(0, 2, 1, 3)
    kh = k.transpose(0, 2, 1, 3)
    vh = v.transpose(0, 2, 1, 3)
    if ctx_kv is None:
        o = _attention(qh, kh, vh)
    else:
        k_ctx, v_ctx = ctx_kv
        o = _attention(qh, jnp.concatenate([kh, k_ctx], axis=2), jnp.concatenate([vh, v_ctx], axis=2))
    return o.transpose(0, 2, 1, 3).reshape(B, T, GROUP_W), (kh, vh)


class _Par:
    def __init__(self, xs):
        self.xs = list(xs)

    def __add__(self, o):
        return _lift(jnp.add, self, o)

    __radd__ = __add__

    def __sub__(self, o):
        return _lift(jnp.subtract, self, o)

    def __rsub__(self, o):
        return _lift(lambda x, y: y - x, self, o)

    def __mul__(self, o):
        return _lift(jnp.multiply, self, o)

    __rmul__ = __mul__

    def __neg__(self):
        return _lift(jnp.negative, self)

    def __getitem__(self, idx):
        return _lift(lambda x: x[idx], self)


def _lift(f, *args):
    n = next(len(a.xs) for a in args if isinstance(a, _Par))
    return _Par([f(*[a.xs[i] if isinstance(a, _Par) else a for a in args]) for i in range(n)])


def _exp(x):
    return _lift(jnp.exp, x)


def _stack_rows(x, y):
    return _lift(lambda a, b: jnp.concatenate([a, b], axis=0), x, y)


def _mm(a, b, dims, two_pass):
    f = lambda x, y: lax.dot_general(x, y, (dims, ((), ())), preferred_element_type=F32)

    def one(a, b):
        a16 = a.astype(BF16)
        hi = b.astype(BF16)
        if not two_pass:
            return f(a16, hi)
        return f(a16, hi) + f(a16, (b - hi.astype(F32)).astype(BF16))

    return _lift(one, a, b)


def _dot(a, b, two_pass=False):
    return _mm(a, b, ((1,), (0,)), two_pass)


def _dot_nt(a, b, two_pass=False):
    return _mm(a, b, ((1,), (1,)), two_pass)


def _dot_tn(a, b):
    return _mm(a, b, ((0,), (0,)), False)


def _chunk_masks(reverse):
    row = lax.broadcasted_iota(jnp.int32, (SCAN_CHUNK, SCAN_CHUNK), 0)
    col = lax.broadcasted_iota(jnp.int32, (SCAN_CHUNK, SCAN_CHUNK), 1)
    earlier = (row < col) if reverse else (row > col)
    eye = row == col
    blk16 = (row >> 4) == (col >> 4)
    blk32 = (row >> 5) == (col >> 5)
    return dict(
        strict=earlier.astype(F32), incl=(earlier | eye).astype(F32), eye=eye.astype(F32),
        d16=blk16.astype(F32), off32=(blk32 & jnp.logical_not(blk16)).astype(F32),
        off64=jnp.logical_not(blk32).astype(F32))


def _scan_chains(heads):
    fwd, bwd = _chunk_masks(False), _chunk_masks(True)
    chains = [(d, h) for d in range(2) for h in range(heads)]
    masks = {key: _Par([(bwd if d else fwd)[key] for d, _ in chains]) for key in fwd}
    return chains, [d == 1 for d, _ in chains], masks


def _last_row(cum, revs):
    return _Par([c[0:1] if rev else c[SCAN_CHUNK - 1:SCAN_CHUNK] for c, rev in zip(cum.xs, revs)])


def _unit_triangular_inverse(n, m):
    eye = m["eye"]
    nd = n * m["d16"]
    n2 = _dot(nd, nd)
    x = eye + nd
    x = x + _dot(x, n2)
    n4 = _dot(n2, n2)
    x = x + _dot(x, n4)
    n8 = _dot(n4, n4)
    x = x + _dot(x, n8)
    x = x + _dot(x, _dot(n * m["off32"], x))
    x = x + _dot(x, _dot(n * m["off64"], x))
    return x


def _rwkv_chunk(r, lw, k, v, a, b, h0, m, revs):
    cum = _dot(m["incl"], lw, two_pass=True)
    last = _last_row(cum, revs)
    g_inv = _exp(-cum)
    at = a * _exp(cum - lw)
    rt = r * _exp(cum)
    bt = b * g_inv
    kt = k * g_inv
    to_end = _exp(last - cum)
    bh = b * to_end
    kh = k * to_end
    ar = _stack_rows(at, rt)
    gb = _dot_nt(ar, bt)
    gk = _dot_nt(ar, kt)
    n_ab = gb[:SCAN_CHUNK] * m["strict"]
    l_ak = gk[:SCAN_CHUNK] * m["strict"]
    m_rb = gb[SCAN_CHUNK:] * m["incl"]
    m_rk = gk[SCAN_CHUNK:] * m["incl"]
    tinv = _unit_triangular_inverse(n_ab, m)
    p1 = _dot(tinv, at)
    u0 = _dot(tinv, _dot(l_ak, v))
    p2 = rt + _dot(m_rb, p1)
    o0 = _dot(m_rb, u0) + _dot(m_rk, v)
    a_c = m["eye"] * _exp(last) + _dot_tn(bh, p1)
    g_c = _dot_tn(bh, u0) + _dot_tn(kh, v)
    return _dot(p2, h0) + o0, _dot(a_c, h0) + g_c


def _rwkv_scan_pallas(seqs, h0):
    B, H, T, dh = seqs[0][0].shape
    n = T // SCAN_CHUNK
    fwd = pl.BlockSpec((1, H, SCAN_CHUNK, dh), lambda b, c: (b, 0, c, 0))
    bwd = pl.BlockSpec((1, H, SCAN_CHUNK, dh), lambda b, c: (b, 0, n - 1 - c, 0))
    state_spec = pl.BlockSpec((1, 2, H, dh, dh), lambda b, c: (b, 0, 0, 0, 0))
    o_f, o_b, h_fin = pl.pallas_call(
        _rwkv_scan_kernel,
        grid=(B, n),
        in_specs=[fwd] * 6 + [bwd] * 6 + [state_spec],
        out_specs=[fwd, bwd, state_spec],
        out_shape=[jax.ShapeDtypeStruct((B, H, T, dh), F32), jax.ShapeDtypeStruct((B, H, T, dh), F32),
                   jax.ShapeDtypeStruct((B, 2, H, dh, dh), F32)],
        scratch_shapes=[pltpu.VMEM((2, H, dh, dh), F32)],
        compiler_params=pltpu.CompilerParams(dimension_semantics=("arbitrary", "arbitrary"),
                                             vmem_limit_bytes=VMEM_LIMIT_BYTES),
        name="rwkv_scan",
    )(*seqs[0], *seqs[1], h0)
    return o_f, o_b, h_fin


def _rwkv_scan_kernel(*refs):
    ins, h0_ref = refs[:12], refs[12]
    o_refs, hout_ref, h_ref = refs[13:15], refs[15], refs[16]
    c = pl.program_id(1)

    @pl.when(c == 0)
    def _():
        h_ref[...] = h0_ref[0]

    chains, revs, m = _scan_chains(RW_HEADS)
    seqs = [_Par([ins[6 * d + i][0, h] for d, h in chains]) for i in range(6)]
    o, h_new = _rwkv_chunk(*seqs, _Par([h_ref[d, h] for d, h in chains]), m, revs)
    for i, (d, h) in enumerate(chains):
        o_refs[d][0, h] = o.xs[i]
        h_ref[d, h] = h_new.xs[i]

    @pl.when(c == pl.num_programs(1) - 1)
    def _():
        hout_ref[0] = h_ref[...]


def _rwkv_mixer(p, lp, S0):
    B, T, _ = p.shape
    W, H = GROUP_W, RW_HEADS
    o0 = 3 * W
    o1 = o0 + 2 * RW_DECAY_RANK
    o2 = o1 + 2 * RW_AAA_RANK
    rkv = p[..., :o0]
    wd = p[..., o0:o1].reshape(B, T, 2, RW_DECAY_RANK)
    ad = p[..., o1:o2].reshape(B, T, 2, RW_AAA_RANK)
    gate = jax.nn.sigmoid(p[..., o2:]) @ lp["rw_g2"]
    heads = lambda a: a.astype(F32).reshape(B, T, H, HEAD_DIM)
    hm = lambda a: jnp.swapaxes(a, 1, 2)
    seqs, bonuses = [], []
    for d in range(2):
        base = jnp.concatenate([rkv, wd[:, :, d], ad[:, :, d]], axis=-1)
        if d == 0:
            shifted = jnp.pad(base[:, :-1], ((0, 0), (1, 0), (0, 0)))
        else:
            shifted = jnp.pad(base[:, 1:], ((0, 0), (0, 1), (0, 0)))
        xd = base + (shifted - base) * lp["rw_mu"][d]
        r, k, v = xd[..., :W], xd[..., W:2 * W], xd[..., 2 * W:3 * W]
        wl, al = xd[..., 3 * W:3 * W + RW_DECAY_RANK], xd[..., 3 * W + RW_DECAY_RANK:]
        w = -jax.nn.softplus(-(lp["rw_w0"][d] + jnp.tanh(wl) @ lp["rw_w2"][d])) - 0.5
        log_decay = -jnp.exp(w.astype(F32))
        a = jax.nn.sigmoid((lp["rw_a0"][d] + al @ lp["rw_a2"][d]).astype(F32))
        kf = k.astype(F32)
        kk = _l2_norm(heads(kf * lp["rw_k_k"].astype(F32)))
        k_eff = heads(kf * (1.0 + (a - 1.0) * lp["rw_k_a"].astype(F32)))
        rh, vh, ah = heads(r), heads(v), heads(a)
        seqs.append([hm(rh), hm(heads(log_decay)), hm(k_eff), hm(vh), hm(-kk), hm(kk * ah)])
        bonuses.append(jnp.sum(rh * k_eff * lp["rw_r_k"].astype(F32), axis=-1, keepdims=True) * vh)
    o_f, o_b, h_fin = _rwkv_scan_pallas(seqs, jnp.swapaxes(S0.astype(F32), -1, -2))
    o = hm(o_f + o_b)
    finals = jnp.swapaxes(h_fin, -1, -2)
    mu = jnp.mean(o, axis=-1, keepdims=True)
    var = jnp.mean(jnp.square(o - mu), axis=-1, keepdims=True)
    y = ((o - mu) * lax.rsqrt(var + RW_LN_EPS)).reshape(B, T, W) * lp["rw_ln_g"].astype(F32) + lp["rw_ln_b"].astype(F32)
    y = y + (bonuses[0] + bonuses[1]).reshape(B, T, W)
    return y.astype(p.dtype) * gate, finals


def _na_mixer(p, lp, ctx_kv):
    B, T, _ = p.shape
    q, k, v = [p[..., i * GROUP_W:(i + 1) * GROUP_W].reshape(B, T, NA_HEADS, HEAD_DIM).transpose(0, 2, 1, 3)
               for i in range(3)]
    if ctx_kv is None:
        o = _attention(q, k, v)
    else:
        o = _neighbourhood_attention(q, k, v, lp["na_bias"], ctx_kv[0], ctx_kv[1])
    return o.transpose(0, 2, 1, 3).reshape(B, T, GROUP_W), (k, v)


def _delta_chunk(q, k, v, gb, betab, s0, m, revs):
    gc = _dot(m["incl"], gb, two_pass=True)
    last = _last_row(gc, revs)
    diff = gc - _dot_nt(m["eye"], gc, two_pass=True)
    dmat = _lift(lambda keep, d: jnp.where(keep > 0.0, jnp.exp(jnp.where(keep > 0.0, d, 0.0)), 0.0), m["incl"], diff)
    kb = k * betab
    gram = _dot_nt(_stack_rows(kb, q), k)
    a_mat = gram[:SCAN_CHUNK] * dmat * m["strict"]
    qk = gram[SCAN_CHUNK:] * dmat
    tm = _unit_triangular_inverse(-a_mat, m)
    u = _dot(tm, v * betab)
    w = _dot(tm, kb * _exp(gc))
    v_new = u - _dot(w, s0)
    o = _dot(q * _exp(gc), s0) + _dot(qk, v_new)
    return o, s0 * _exp(last) + _dot_tn(k * _exp(last - gc), v_new)


def _delta_scan_kernel(*refs):
    ins, s0_ref = refs[:10], refs[10]
    o_refs, sout_ref, s_ref = refs[11:13], refs[13], refs[14]
    c = pl.program_id(1)

    @pl.when(c == 0)
    def _():
        s_ref[...] = s0_ref[0]

    chains, revs, m = _scan_chains(DN_HEADS)
    seqs = [_Par([ins[5 * d + i][0, h] for d, h in chains]) for i in range(5)]
    o, s_new = _delta_chunk(*seqs, _Par([s_ref[d, h] for d, h in chains]), m, revs)
    for i, (d, h) in enumerate(chains):
        o_refs[d][0, h] = o.xs[i]
        s_ref[d, h] = s_new.xs[i]

    @pl.when(c == pl.num_programs(1) - 1)
    def _():
        sout_ref[0] = s_ref[...]


def _delta_scan_pallas(q, k, v, g, beta, s0):
    assert SCAN_CHUNK == HEAD_DIM
    B, H, T, dh = q.shape
    n = T // SCAN_CHUNK
    fwd = pl.BlockSpec((1, H, SCAN_CHUNK, dh), lambda b, c: (b, 0, c, 0))
    bwd = pl.BlockSpec((1, H, SCAN_CHUNK, dh), lambda b, c: (b, 0, n - 1 - c, 0))
    state_spec = pl.BlockSpec((1, 2, H, dh, dh), lambda b, c: (b, 0, 0, 0, 0))
    seq = jax.ShapeDtypeStruct((B, H, T, dh), F32)
    return pl.pallas_call(
        _delta_scan_kernel,
        grid=(B, n),
        in_specs=[fwd] * 5 + [bwd] * 5 + [state_spec],
        out_specs=[fwd, bwd, state_spec],
        out_shape=[seq, seq, jax.ShapeDtypeStruct((B, 2, H, dh, dh), F32)],
        scratch_shapes=[pltpu.VMEM((2, H, dh, dh), F32)],
        compiler_params=pltpu.CompilerParams(dimension_semantics=("arbitrary", "arbitrary"),
                                             vmem_limit_bytes=VMEM_LIMIT_BYTES),
        name="delta_scan",
    )(q, k, v, g[:, 0], beta[:, 0], q, k, v, g[:, 1], beta[:, 1], s0)


def _deltanet_mixer(p, lp, S0):
    B, T, _ = p.shape
    W, H = GROUP_W, DN_HEADS
    xin = jnp.pad(p[..., :3 * W], ((0, 0), (DN_CONV // 2, DN_CONV // 2), (0, 0)))
    qkv = sum(xin[:, j:j + T] * lp["dn_conv"][j] for j in range(DN_CONV))
    qkv = jax.nn.silu(qkv)
    heads = lambda a: a.astype(F32).reshape(B, T, H, HEAD_DIM)
    hm = lambda a: jnp.swapaxes(a, 1, 2)
    q = _l2_norm(heads(qkv[..., :W])) * (HEAD_DIM ** -0.5)
    k = _l2_norm(heads(qkv[..., W:2 * W]))
    v = heads(qkv[..., 2 * W:])
    beta = jax.nn.sigmoid(p[..., 3 * W:3 * W + 2 * H].astype(F32)).reshape(B, T, 2, H)
    alpha = p[..., 3 * W + 2 * H:3 * W + 4 * H].astype(F32).reshape(B, T, 2, H)
    g = -jnp.exp(lp["dn_a_log"].astype(F32)) * jax.nn.softplus(alpha + lp["dn_dt_bias"].astype(F32))
    z = heads(p[..., 3 * W + 4 * H:])
    lanes = lambda a: jnp.broadcast_to(a.transpose(0, 2, 3, 1)[..., None], (B, 2, H, T, HEAD_DIM))
    o_f, o_b, s_fin = _delta_scan_pallas(hm(q), hm(k), hm(v), lanes(g), lanes(beta), S0.astype(F32))
    o = _rms_norm(hm(o_f + o_b), lp["dn_norm_g"]) * jax.nn.silu(z)
    return o.reshape(B, T, W).astype(p.dtype), s_fin


def _modulated_norm(x, gain, scale, shift):
    y = x * lax.rsqrt(jnp.mean(x * x, axis=-1, keepdims=True) + NORM_EPS)
    return (y * gain) * (1.0 + scale) + shift


def _proj_in_kernel(x_ref, mod_ref, gain_ref, w_ref, o_ref):
    h = _modulated_norm(x_ref[...], gain_ref[...], mod_ref[0, MOD_SCALE1:MOD_SCALE1 + 1], mod_ref[0, MOD_SHIFT1:MOD_SHIFT1 + 1])
    o_ref[...] = jnp.dot(h.astype(BF16), w_ref[...], preferred_element_type=F32)


def _proj_out_kernel(mix_ref, x_ref, mod_ref, gain_ref, w_ref, x_out_ref, h2_ref):
    y = jnp.dot(mix_ref[...].astype(BF16), w_ref[...], preferred_element_type=F32)
    x_new = x_ref[...] + mod_ref[0, MOD_GATE1:MOD_GATE1 + 1] * y
    x_out_ref[...] = x_new
    h2_ref[...] = _modulated_norm(x_new, gain_ref[...], mod_ref[0, MOD_SCALE2:MOD_SCALE2 + 1],
                                  mod_ref[0, MOD_SHIFT2:MOD_SHIFT2 + 1])


def _mod_spec(mod, tokens_per_cond, tile):
    tiles_per_cond = tokens_per_cond // tile
    if mod.shape[0] == 1:
        return pl.BlockSpec((1, 6, D_MODEL), lambda i, *_: (0, 0, 0))
    assert tokens_per_cond % tile == 0
    return pl.BlockSpec((1, 6, D_MODEL), lambda i, *_: (i // tiles_per_cond, 0, 0))


def _proj_in(x, mod, gain, w_bf, tokens_per_cond):
    n, cols = x.shape[0], w_bf.shape[1]
    tm = PROJ_TOKENS
    tok = pl.BlockSpec((tm, D_MODEL), lambda i: (i, 0))
    return pl.pallas_call(
        _proj_in_kernel,
        grid=(n // tm,),
        in_specs=[tok, _mod_spec(mod, tokens_per_cond, tm), pl.BlockSpec((1, D_MODEL), lambda i: (0, 0)),
                  pl.BlockSpec((D_MODEL, cols), lambda i: (0, 0))],
        out_specs=pl.BlockSpec((tm, cols), lambda i: (i, 0)),
        out_shape=jax.ShapeDtypeStruct((n, cols), F32),
        compiler_params=pltpu.CompilerParams(dimension_semantics=("arbitrary",), vmem_limit_bytes=VMEM_LIMIT_BYTES),
        name="proj_in",
    )(x, mod, gain, w_bf)


def _proj_out(mix, x, mod, gain, w_bf, tokens_per_cond):
    n = x.shape[0]
    tm = PROJ_TOKENS
    tok = pl.BlockSpec((tm, D_MODEL), lambda i: (i, 0))
    out = jax.ShapeDtypeStruct((n, D_MODEL), F32)
    return pl.pallas_call(
        _proj_out_kernel,
        grid=(n // tm,),
        in_specs=[tok, tok, _mod_spec(mod, tokens_per_cond, tm), pl.BlockSpec((1, D_MODEL), lambda i: (0, 0)),
                  pl.BlockSpec((D_MODEL, D_MODEL), lambda i: (0, 0))],
        out_specs=[tok, tok],
        out_shape=[out, out],
        compiler_params=pltpu.CompilerParams(dimension_semantics=("arbitrary",), vmem_limit_bytes=VMEM_LIMIT_BYTES),
        name="proj_out",
    )(mix, x, mod, gain, w_bf)


def _trunk_layer(x, cond, lp, ctx, final_gain):
    B, T, _ = x.shape
    mod = (jax.nn.silu(cond) @ lp["w_mod"] + lp["b_mod"]).reshape(-1, 6, D_MODEL)
    x = x.reshape(B * T, D_MODEL)
    proj = _proj_in(x, mod, lp["norm1_g"][None], lp["w_in_bf"], T).reshape(B, T, -1)
    pa, pb, pc, pd, _ = jnp.split(proj, [GQA_PROJ, GQA_PROJ + RW_PROJ, GQA_PROJ + RW_PROJ + NA_PROJ, IN_COLS], axis=-1)
    if ctx is None:
        ctx_a = None
        ctx_c = None
        s_rw0 = jnp.zeros((B, 2, RW_HEADS, HEAD_DIM, HEAD_DIM), F32)
        s_dn0 = jnp.zeros((B, 2, DN_HEADS, HEAD_DIM, HEAD_DIM), F32)
    else:
        ka_c, va_c, kc_c, vc_c, s_rw0, s_dn0 = ctx
        ctx_a = (ka_c, va_c)
        ctx_c = (kc_c, vc_c)
    oa, (ka, va) = _gqa_mixer(pa, lp, ctx_a)
    ob, s_rw = _rwkv_mixer(pb, lp, s_rw0)
    oc, (kc, vc) = _na_mixer(pc, lp, ctx_c)
    od, s_dn = _deltanet_mixer(pd, lp, s_dn0)
    mix = jnp.concatenate([oa, ob, oc, od], axis=-1).reshape(B * T, D_MODEL)
    x, h2 = _proj_out(mix, x, mod, lp["norm2_g"][None], lp["w_out_bf"], T)
    x = _peer_residual(x, h2, mod, lp, T, final_gain)
    return x.reshape(B, T, D_MODEL), (ka, va, kc, vc, s_rw, s_dn)


def kernel(x_prompt, x_sample, c, cache_gqa_k, cache_gqa_v, cache_na_k, cache_na_v, state_rwkv, state_delta, c_ctx, norm1_g, norm2_g, w_mod, b_mod, w_in, w_out, gqa_q_norm, gqa_k_norm, rw_mu, rw_w0, rw_w2, rw_a0, rw_a2, rw_g2, rw_k_k, rw_k_a, rw_r_k, rw_ln_g, rw_ln_b, na_bias, dn_conv, dn_a_log, dn_dt_bias, dn_norm_g, peer_wq, peer_keys, peer_u, peer_v, final_norm_g):
    depth = w_in.shape[0]

    def layer_params(l):
        return {
            "norm1_g": norm1_g[l], "norm2_g": norm2_g[l], "w_mod": w_mod[l], "b_mod": b_mod[l],
            "w_in_bf": jnp.pad(w_in[l].astype(BF16), ((0, 0), (0, IN_COLS_PADDED - IN_COLS))),
            "w_out_bf": w_out[l].astype(BF16), "gqa_q_norm": gqa_q_norm[l], "gqa_k_norm": gqa_k_norm[l],
            "rw_mu": rw_mu[l], "rw_w0": rw_w0[l], "rw_w2": rw_w2[l], "rw_a0": rw_a0[l], "rw_a2": rw_a2[l],
            "rw_g2": rw_g2[l], "rw_k_k": rw_k_k[l], "rw_k_a": rw_k_a[l], "rw_r_k": rw_r_k[l],
            "rw_ln_g": rw_ln_g[l], "rw_ln_b": rw_ln_b[l], "na_bias": na_bias[l],
            "dn_conv": dn_conv[l], "dn_a_log": dn_a_log[l], "dn_dt_bias": dn_dt_bias[l], "dn_norm_g": dn_norm_g[l],
            "peer_wq_bf": peer_wq[l].astype(BF16),
            "peer_keys_bf": peer_keys[l].reshape(2 * PEER_HEADS, PEER_NKEYS, PEER_DQ // 2).astype(BF16),
            "peer_u_bf": peer_u[l].astype(BF16),
            "peer_vt_bf": peer_v[l].astype(BF16).T,
        }

    params = [layer_params(l) for l in range(depth)]

    xp = x_prompt
    cond_ctx = c_ctx[None, :]
    per_layer = []
    closing = lambda l: final_norm_g if l == depth - 1 else None
    for l in range(depth):
        xp, st = _trunk_layer(xp, cond_ctx, params[l], None, closing(l))
        per_layer.append(st)
    y_prompt = xp

    xs = x_sample
    for l in range(depth):
        ctx = (cache_gqa_k[:, l], cache_gqa_v[:, l], cache_na_k[:, l], cache_na_v[:, l],
               state_rwkv[:, l], state_delta[:, l])
        xs, _ = _trunk_layer(xs, c, params[l], ctx, closing(l))
    y_sample = xs

    dt = x_prompt.dtype
    new_gqa_k = jnp.stack([st[0] for st in per_layer], axis=1).astype(dt)
    new_gqa_v = jnp.stack([st[1] for st in per_layer], axis=1).astype(dt)
    new_na_k = jnp.stack([st[2] for st in per_layer], axis=1).astype(dt)
    new_na_v = jnp.stack([st[3] for st in per_layer], axis=1).astype(dt)
    new_state_rwkv = jnp.stack([st[4] for st in per_layer], axis=1).astype(dt)
    new_state_delta = jnp.stack([st[5] for st in per_layer], axis=1).astype(dt)
    return (y_prompt, y_sample, new_gqa_k, new_gqa_v, new_na_k, new_na_v, new_state_rwkv, new_state_delta)
```

```python
import functools
import math

import jax
import jax.numpy as jnp
import numpy as np
from jax import lax
from jax.experimental import pallas as pl
from jax.experimental.pallas import tpu as pltpu

F32 = jnp.float32
BF16 = jnp.bfloat16

D_MODEL = 1024
GRID_W = 64
HEAD_DIM = 64
GROUP_W = D_MODEL // 4
NORM_EPS = 1e-6
NEG_INF = -1e30
GQA_HEADS = GROUP_W // HEAD_DIM
GQA_KV_HEADS = GQA_HEADS // 2
ROPE_THETA = 10000.0
RW_HEADS = GROUP_W // HEAD_DIM
RW_DECAY_RANK = 32
RW_AAA_RANK = 32
RW_GATE_RANK = 64
RW_LN_EPS = 64e-5
NA_HEADS = GROUP_W // HEAD_DIM
NA_ROWS = 8
NA_COLS = 16
DN_HEADS = GROUP_W // HEAD_DIM
DN_CONV = 5
PEER_HEADS = 8
PEER_NKEYS = 128
PEER_EXPERTS = PEER_NKEYS * PEER_NKEYS
PEER_DQ = 256
PEER_TOPK = 16
GQA_PROJ = GROUP_W + 2 * GQA_KV_HEADS * HEAD_DIM
RW_PROJ = 3 * GROUP_W + 2 * RW_DECAY_RANK + 2 * RW_AAA_RANK + RW_GATE_RANK
NA_PROJ = 3 * GROUP_W
DN_PROJ = 3 * GROUP_W + 4 * DN_HEADS + GROUP_W

LANES = 128
VMEM_LIMIT_BYTES = 56 * 1024 * 1024

PEER_SEL_TOKENS = 128
PEER_TOKENS = 256
PEER_ROWS_PER_STEP = 16
PEER_SUB_ROWS = 2
PEER_CHUNK = PEER_ROWS_PER_STEP * PEER_NKEYS
PEER_STEPS = PEER_EXPERTS // PEER_CHUNK
INV_SQRT2 = 1.0 / math.sqrt(2.0)

IN_COLS = GQA_PROJ + RW_PROJ + NA_PROJ + DN_PROJ
IN_COLS_PADDED = -(-IN_COLS // LANES) * LANES
MOD_SHIFT1, MOD_SCALE1, MOD_GATE1, MOD_SHIFT2, MOD_SCALE2, MOD_GATE2 = range(6)
PROJ_TOKENS = 512

ATTN_SCALE = HEAD_DIM ** -0.5
ATTN_Q_TILE = 256
NA_ROW_BLOCK = 8

SCAN_CHUNK = 64
SCAN_ROWS = 2


class _Par:
    def __init__(self, xs):
        self.xs = list(xs)

    def __add__(self, o):
        return _lift(jnp.add, self, o)

    __radd__ = __add__

    def __sub__(self, o):
        return _lift(jnp.subtract, self, o)

    def __rsub__(self, o):
        return _lift(lambda x, y: y - x, self, o)

    def __mul__(self, o):
        return _lift(jnp.multiply, self, o)

    __rmul__ = __mul__

    def __neg__(self):
        return _lift(jnp.negative, self)

    def __getitem__(self, idx):
        return _lift(lambda x: x[idx], self)


def _lift(f, *args):
    n = next(len(a.xs) for a in args if isinstance(a, _Par))
    return _Par([f(*[a.xs[i] if isinstance(a, _Par) else a for a in args]) for i in range(n)])


def _exp(x):
    return _lift(jnp.exp, x)


def _stack_rows(x, y):
    return _lift(lambda a, b: jnp.concatenate([a, b], axis=0), x, y)


def _top_k_rows(s, k):
    rows = s.shape[0]
    iota = lax.broadcasted_iota(jnp.int32, s.shape, 0).astype(F32)
    vals = []
    work = s
    for _ in range(k):
        m = jnp.max(work, axis=0, keepdims=True)
        first = jnp.min(jnp.where(work == m, iota, float(rows)), axis=0, keepdims=True)
        work = jnp.where(iota == first, -jnp.inf, work)
        vals.append(m)
    return vals, work


def _peer_select_kernel(h_ref, wq_ref, keys_ref, thr_ref, e1_ref, s2m_ref, e2_ref):
    x = h_ref[...].astype(BF16)
    q = jnp.dot(x, wq_ref[...], preferred_element_type=F32)
    half = PEER_DQ // 2
    for h in range(PEER_HEADS):
        scores, tops, sel = [], [], []
        for p in range(2):
            c0 = (2 * h + p) * half
            qhp = q[:, c0:c0 + half].astype(BF16)
            s = lax.dot_general(keys_ref[2 * h + p], qhp, (((1,), (1,)), ((), ())),
                                preferred_element_type=F32)
            vals, work = _top_k_rows(s, PEER_TOPK)
            scores.append(s)
            tops.append(vals)
            sel.append(work == -jnp.inf)
        m1, m2 = tops
        m2_all = jnp.concatenate(m2, axis=0)
        cand = jnp.concatenate([m1[i] + m2_all for i in range(8)]
                               + [jnp.concatenate(m1[8:], axis=0) + m2[0]], axis=0)
        top_s, _ = _top_k_rows(cand, PEER_TOPK)
        mx = top_s[0]
        z = jnp.exp(top_s[0] - mx)
        for r in range(1, PEER_TOPK):
            z = z + jnp.exp(top_s[r] - mx)
        half_inv_z = 0.5 / z
        t3 = top_s[PEER_TOPK - 1]
        s1m = jnp.where(sel[0], scores[0], -jnp.inf)
        thr = jnp.full_like(s1m, jnp.inf)
        for j in range(PEER_TOPK):
            thr = jnp.where(s1m + m2[j] >= t3, m2[j], thr)
        thr_ref[h] = thr
        e1_ref[h] = jnp.where(sel[0], jnp.exp(scores[0] - m1[0]) * half_inv_z, 0.0)
        s2m_ref[h] = jnp.where(sel[1], scores[1], -jnp.inf)
        e2_ref[h] = jnp.exp(scores[1] - m2[0])


def _peer_select(h2, wq_bf, keys_bf):
    n = h2.shape[0]
    tt = PEER_SEL_TOKENS
    big = jax.ShapeDtypeStruct((PEER_HEADS, PEER_NKEYS, n), F32)
    big_spec = pl.BlockSpec((PEER_HEADS, PEER_NKEYS, tt), lambda i: (0, 0, i))
    return pl.pallas_call(
        _peer_select_kernel,
        grid=(n // tt,),
        in_specs=[
            pl.BlockSpec((tt, D_MODEL), lambda i: (i, 0)),
            pl.BlockSpec((D_MODEL, PEER_HEADS * PEER_DQ), lambda i: (0, 0)),
            pl.BlockSpec((2 * PEER_HEADS, PEER_NKEYS, PEER_DQ // 2), lambda i: (0, 0, 0)),
        ],
        out_specs=[big_spec] * 4,
        out_shape=[big] * 4,
        compiler_params=pltpu.CompilerParams(dimension_semantics=("arbitrary",),
                                             vmem_limit_bytes=VMEM_LIMIT_BYTES),
        name="peer_select",
    )(h2, wq_bf, keys_bf)


def _peer_dense_kernel(h_ref, u_ref, vt_ref, thr_ref, e1_ref, s2m_ref, e2_ref, xres_ref, mod_ref, fin_ref,
                       out_ref, ht_ref, acc_ref, p_ref, *, final_norm):
    j = pl.program_id(1)
    tt = h_ref.shape[0]

    @pl.when(j == 0)
    def _():
        ht_ref[...] = h_ref[...].T.astype(BF16)
        acc_ref[...] = jnp.zeros_like(acc_ref)

    sub = PEER_SUB_ROWS * PEER_NKEYS

    def activations(k):
        return jnp.dot(u_ref[k * sub:(k + 1) * sub, :], ht_ref[...], preferred_element_type=F32)

    def weighted(k, act, slot):
        for ai in range(PEER_SUB_ROWS):
            a = k * PEER_SUB_ROWS + ai
            rows = slice(ai * PEER_NKEYS, (ai + 1) * PEER_NKEYS)
            for c in range(tt // LANES):
                cs = slice(c * LANES, (c + 1) * LANES)
                zz = act[rows, cs]
                gelu2 = zz * (1.0 + lax.erf(zz * INV_SQRT2))
                w = jnp.zeros((PEER_NKEYS, LANES), F32)
                for h in range(PEER_HEADS):
                    picked = s2m_ref[h, :, cs] >= thr_ref[h, a:a + 1, cs]
                    w = w + jnp.where(picked, e2_ref[h, :, cs], 0.0) * e1_ref[h, a:a + 1, cs]
                p_ref[slot, rows, cs] = (w * gelu2).astype(BF16)

    n_sub = PEER_ROWS_PER_STEP // PEER_SUB_ROWS
    act = activations(0)
    for k in range(n_sub):
        nxt = activations(k + 1) if k + 1 < n_sub else None
        weighted(k, act, k % 2)
        acc_ref[...] += jnp.dot(vt_ref[:, k * sub:(k + 1) * sub], p_ref[k % 2], preferred_element_type=F32)
        act = nxt

    @pl.when(j == PEER_STEPS - 1)
    def _():
        y = xres_ref[...] + mod_ref[0, MOD_GATE2:MOD_GATE2 + 1] * acc_ref[...].T
        if final_norm:
            y = y * lax.rsqrt(jnp.mean(y * y, axis=-1, keepdims=True) + NORM_EPS) * fin_ref[...]
        out_ref[...] = y


def _mod_spec(mod, tokens_per_cond, tile):
    tiles_per_cond = tokens_per_cond // tile
    if mod.shape[0] == 1:
        return pl.BlockSpec((1, 6, D_MODEL), lambda i, *_: (0, 0, 0))
    assert tokens_per_cond % tile == 0
    return pl.BlockSpec((1, 6, D_MODEL), lambda i, *_: (i // tiles_per_cond, 0, 0))


def _peer_dense(h2, u_bf, vt_bf, sel, xres, mod, tokens_per_cond, final_gain):
    n = h2.shape[0]
    tt = PEER_TOKENS
    thr, e1, s2m, e2 = sel
    fin = jnp.ones((1, D_MODEL), F32) if final_gain is None else final_gain.astype(F32)[None]
    row_spec = pl.BlockSpec((PEER_HEADS, PEER_ROWS_PER_STEP, tt), lambda i, j: (0, j, i))
    col_spec = pl.BlockSpec((PEER_HEADS, PEER_NKEYS, tt), lambda i, j: (0, 0, i))
    tok_spec = pl.BlockSpec((tt, D_MODEL), lambda i, j: (i, 0))
    return pl.pallas_call(
        functools.partial(_peer_dense_kernel, final_norm=final_gain is not None),
        grid=(n // tt, PEER_STEPS),
        in_specs=[
            tok_spec,
            pl.BlockSpec((PEER_CHUNK, D_MODEL), lambda i, j: (j, 0)),
            pl.BlockSpec((D_MODEL, PEER_CHUNK), lambda i, j: (0, j)),
            row_spec, row_spec, col_spec, col_spec,
            tok_spec,
            _mod_spec(mod, tokens_per_cond, tt),
            pl.BlockSpec((1, D_MODEL), lambda i, j: (0, 0)),
        ],
        out_specs=tok_spec,
        out_shape=jax.ShapeDtypeStruct((n, D_MODEL), F32),
        scratch_shapes=[pltpu.VMEM((D_MODEL, tt), BF16), pltpu.VMEM((D_MODEL, tt), F32),
                        pltpu.VMEM((2, PEER_SUB_ROWS * PEER_NKEYS, tt), BF16)],
        compiler_params=pltpu.CompilerParams(dimension_semantics=("arbitrary", "arbitrary"),
                                             vmem_limit_bytes=VMEM_LIMIT_BYTES),
        name="peer_dense",
    )(h2, u_bf, vt_bf, thr, e1, s2m, e2, xres, mod, fin)


def _peer_residual(x, h2, mod, lp, tokens_per_cond, final_gain):
    sel = _peer_select(h2, lp["peer_wq_bf"], lp["peer_keys_bf"])
    return _peer_dense(h2, lp["peer_u_bf"], lp["peer_vt_bf"], sel, x, mod, tokens_per_cond, final_gain)


def _rms_norm(x, g):
    xf = x.astype(F32)
    y = xf * lax.rsqrt(jnp.mean(xf * xf, axis=-1, keepdims=True) + NORM_EPS)
    return (y * g.astype(F32)).astype(x.dtype)


def _l2_norm(x):
    return x * lax.rsqrt(jnp.sum(x * x, axis=-1, keepdims=True) + NORM_EPS)


def _axial_rope_tables(T):
    t = np.arange(T)
    row, col = t // GRID_W, t % GRID_W
    n_freq = HEAD_DIM // 4
    inv = ROPE_THETA ** (-2.0 * np.arange(n_freq) / (HEAD_DIM // 2))
    ang = np.concatenate([row[:, None] * inv[None], col[:, None] * inv[None]], axis=1)
    return jnp.asarray(np.cos(ang), F32), jnp.asarray(np.sin(ang), F32)


def _apply_rope(x, cos, sin):
    xf = x.astype(F32)
    half = HEAD_DIM // 2
    x1, x2 = xf[..., :half], xf[..., half:]
    c, s = cos[None, :, None, :], sin[None, :, None, :]
    return jnp.concatenate([x1 * c - x2 * s, x2 * c + x1 * s], axis=-1).astype(x.dtype)


def _attention_kernel(q_ref, k_ref, v_ref, o_ref):
    q = q_ref[0, 0].astype(BF16)
    s = lax.dot_general(q, k_ref[0, 0], (((1,), (1,)), ((), ())), preferred_element_type=F32) * ATTN_SCALE
    m = jnp.max(s, axis=-1, keepdims=True)
    p = jnp.exp(s - m)
    l = jnp.sum(p, axis=-1, keepdims=True)
    o = jnp.dot(p.astype(BF16), v_ref[0, 0], preferred_element_type=F32)
    o_ref[0, 0] = o / l


def _attention(q, k, v):
    B, hq, T, dh = q.shape
    hkv, L = k.shape[1], k.shape[2]
    group = hq // hkv
    tq = min(T, ATTN_Q_TILE)
    kv_spec = pl.BlockSpec((1, 1, L, dh), lambda b, h, i: (b, h // group, 0, 0))
    q_spec = pl.BlockSpec((1, 1, tq, dh), lambda b, h, i: (b, h, i, 0))
    return pl.pallas_call(
        _attention_kernel,
        grid=(B, hq, T // tq),
        in_specs=[q_spec, kv_spec, kv_spec],
        out_specs=q_spec,
        out_shape=jax.ShapeDtypeStruct((B, hq, T, dh), F32),
        compiler_params=pltpu.CompilerParams(dimension_semantics=("arbitrary",) * 3,
                                             vmem_limit_bytes=VMEM_LIMIT_BYTES),
        name="attention",
    )(q, k.astype(BF16), v.astype(BF16))


def _na_bias_table(bias):
    qcols = np.arange(GRID_W)
    kcols = np.arange(GRID_W)
    cstart = np.clip(qcols - NA_COLS // 2, 0, GRID_W - NA_COLS)
    valid = (kcols[None, :] >= cstart[:, None]) & (kcols[None, :] < cstart[:, None] + NA_COLS)
    dc_idx = np.clip(kcols[None, :] - qcols[:, None], -(NA_COLS - 1), NA_COLS - 1) + (NA_COLS - 1)
    onehot = jnp.asarray(dc_idx[:, :, None] == np.arange(2 * NA_COLS - 1), F32)
    rel = jnp.einsum("qkc,hrc->hrqk", onehot, bias.astype(F32), precision=lax.Precision.HIGHEST)
    rel = jnp.where(valid[None, None], rel, NEG_INF)
    return jnp.stack([jnp.concatenate([rel[:, d0 + j] for j in range(NA_ROWS)], axis=-1)
                      for d0 in range(NA_ROWS)], axis=1)


def _na_kernel(q_ref, k_ref, v_ref, kc_ref, vc_ref, bias_ref, o_ref):
    blk = pl.program_id(2)
    n_rows = k_ref.shape[2] // GRID_W
    win = NA_ROWS * GRID_W
    nt = (((1,), (1,)), ((), ()))
    qs, ks, vs, bs = [], [], [], []
    for i in range(NA_ROW_BLOCK):
        r = blk * NA_ROW_BLOCK + i
        start = jnp.clip(r - NA_ROWS // 2, 0, n_rows - NA_ROWS)
        off = pl.multiple_of(start * GRID_W, GRID_W)
        qs.append(q_ref[0, 0, i * GRID_W:(i + 1) * GRID_W, :].astype(BF16))
        ks.append(k_ref[0, 0, pl.ds(off, win), :])
        vs.append(v_ref[0, 0, pl.ds(off, win), :])
        bs.append(bias_ref[0, start - r + (NA_ROWS - 1)])
    q, k, v, bias = _Par(qs), _Par(ks), _Par(vs), _Par(bs)
    s_loc = _lift(lambda q, k, b: lax.dot_general(q, k, nt, preferred_element_type=F32) * ATTN_SCALE + b, q, k, bias)
    s_ctx = _lift(lambda q: lax.dot_general(q, kc_ref[0, 0], nt, preferred_element_type=F32) * ATTN_SCALE, q)
    m = _lift(lambda a, b: jnp.maximum(jnp.max(a, axis=-1, keepdims=True), jnp.max(b, axis=-1, keepdims=True)),
              s_loc, s_ctx)
    p_loc = _exp(s_loc - m)
    p_ctx = _exp(s_ctx - m)
    l = _lift(lambda a, b: jnp.sum(a, axis=-1, keepdims=True) + jnp.sum(b, axis=-1, keepdims=True), p_loc, p_ctx)
    o = _lift(lambda pl_, v, pc: jnp.dot(pl_.astype(BF16), v, preferred_element_type=F32)
              + jnp.dot(pc.astype(BF16), vc_ref[0, 0], preferred_element_type=F32), p_loc, v, p_ctx)
    for i in range(NA_ROW_BLOCK):
        o_ref[0, 0, i * GRID_W:(i + 1) * GRID_W, :] = o.xs[i] / l.xs[i]


def _neighbourhood_attention(q, k, v, bias, k_ctx, v_ctx):
    B, H, T, dh = q.shape
    P = k_ctx.shape[2]
    rows = T // GRID_W
    assert rows >= NA_ROWS and rows % NA_ROW_BLOCK == 0
    tq = NA_ROW_BLOCK * GRID_W
    full = lambda n: pl.BlockSpec((1, 1, n, dh), lambda b, h, i: (b, h, 0, 0))
    q_spec = pl.BlockSpec((1, 1, tq, dh), lambda b, h, i: (b, h, i, 0))
    return pl.pallas_call(
        _na_kernel,
        grid=(B, H, rows // NA_ROW_BLOCK),
        in_specs=[q_spec, full(T), full(T), full(P), full(P),
                  pl.BlockSpec((1, NA_ROWS, GRID_W, NA_ROWS * GRID_W), lambda b, h, i: (h, 0, 0, 0))],
        out_specs=q_spec,
        out_shape=jax.ShapeDtypeStruct((B, H, T, dh), F32),
        compiler_params=pltpu.CompilerParams(dimension_semantics=("arbitrary",) * 3,
                                             vmem_limit_bytes=VMEM_LIMIT_BYTES),
        name="na_attention",
    )(q, k.astype(BF16), v.astype(BF16), k_ctx.astype(BF16), v_ctx.astype(BF16), _na_bias_table(bias))


def _gqa_mixer(p, lp, ctx_kv):
    B, T, _ = p.shape
    kvw = GQA_KV_HEADS * HEAD_DIM
    q = _rms_norm(p[..., :GROUP_W].reshape(B, T, GQA_HEADS, HEAD_DIM), lp["gqa_q_norm"])
    k = _rms_norm(p[..., GROUP_W:GROUP_W + kvw].reshape(B, T, GQA_KV_HEADS, HEAD_DIM), lp["gqa_k_norm"])
    v = p[..., GROUP_W + kvw:].reshape(B, T, GQA_KV_HEADS, HEAD_DIM)
    if ctx_kv is not None:
        cos, sin = _axial_rope_tables(T)
        q = _apply_rope(q, cos, sin)
        k = _apply_rope(k, cos, sin)
    qh = q.transpose(0, 2, 1, 3)
    kh = k.transpose(0, 2, 1, 3)
    vh = v.transpose(0, 2, 1, 3)
    if ctx_kv is None:
        o = _attention(qh, kh, vh)
    else:
        k_ctx, v_ctx = ctx_kv
        o = _attention(qh, jnp.concatenate([kh, k_ctx], axis=2), jnp.concatenate([vh, v_ctx], axis=2))
    return o.transpose(0, 2, 1, 3).reshape(B, T, GROUP_W), (kh, vh)


def _na_mixer(p, lp, ctx_kv):
    B, T, _ = p.shape
    q, k, v = [p[..., i * GROUP_W:(i + 1) * GROUP_W].reshape(B, T, NA_HEADS, HEAD_DIM).transpose(0, 2, 1, 3)
               for i in range(3)]
    if ctx_kv is None:
        o = _attention(q, k, v)
    else:
        o = _neighbourhood_attention(q, k, v, lp["na_bias"], ctx_kv[0], ctx_kv[1])
    return o.transpose(0, 2, 1, 3).reshape(B, T, GROUP_W), (k, v)


def _mm(a, b, dims, two_pass):
    f = lambda x, y: lax.dot_general(x, y, (dims, ((), ())), preferred_element_type=F32)

    def one(a, b):
        a16 = a.astype(BF16)
        hi = b.astype(BF16)
        if not two_pass:
            return f(a16, hi)
        return f(a16, hi) + f(a16, (b - hi.astype(F32)).astype(BF16))

    return _lift(one, a, b)


def _dot(a, b, two_pass=False):
    return _mm(a, b, ((1,), (0,)), two_pass)


def _dot_nt(a, b, two_pass=False):
    return _mm(a, b, ((1,), (1,)), two_pass)


def _dot_tn(a, b):
    return _mm(a, b, ((0,), (0,)), False)


def _chunk_masks(reverse):
    row = lax.broadcasted_iota(jnp.int32, (SCAN_CHUNK, SCAN_CHUNK), 0)
    col = lax.broadcasted_iota(jnp.int32, (SCAN_CHUNK, SCAN_CHUNK), 1)
    earlier = (row < col) if reverse else (row > col)
    eye = row == col
    blk16 = (row >> 4) == (col >> 4)
    blk32 = (row >> 5) == (col >> 5)
    return dict(
        strict=earlier.astype(F32), incl=(earlier | eye).astype(F32), eye=eye.astype(F32),
        d16=blk16.astype(F32), off32=(blk32 & jnp.logical_not(blk16)).astype(F32),
        off64=jnp.logical_not(blk32).astype(F32))


def _scan_chains(heads):
    fwd, bwd = _chunk_masks(False), _chunk_masks(True)
    chains = [(b, d, h) for b in range(SCAN_ROWS) for d in range(2) for h in range(heads)]
    masks = {key: _Par([(bwd if d else fwd)[key] for _, d, _ in chains]) for key in fwd}
    return chains, [d == 1 for _, d, _ in chains], masks


def _last_row(cum, revs):
    return _Par([c[0:1] if rev else c[SCAN_CHUNK - 1:SCAN_CHUNK] for c, rev in zip(cum.xs, revs)])


def _unit_triangular_inverse(n, m):
    eye = m["eye"]
    nd = n * m["d16"]
    n2 = _dot(nd, nd)
    x = eye + nd
    x = x + _dot(x, n2)
    n4 = _dot(n2, n2)
    x = x + _dot(x, n4)
    n8 = _dot(n4, n4)
    x = x + _dot(x, n8)
    x = x + _dot(x, _dot(n * m["off32"], x))
    x = x + _dot(x, _dot(n * m["off64"], x))
    return x


def _rwkv_chunk(r, lw, k, v, a, b, h0, m, revs):
    cum = _dot(m["incl"], lw, two_pass=True)
    last = _last_row(cum, revs)
    g_inv = _exp(-cum)
    at = a * _exp(cum - lw)
    rt = r * _exp(cum)
    bt = b * g_inv
    kt = k * g_inv
    to_end = _exp(last - cum)
    bh = b * to_end
    kh = k * to_end
    ar = _stack_rows(at, rt)
    gb = _dot_nt(ar, bt)
    gk = _dot_nt(ar, kt)
    n_ab = gb[:SCAN_CHUNK] * m["strict"]
    l_ak = gk[:SCAN_CHUNK] * m["strict"]
    m_rb = gb[SCAN_CHUNK:] * m["incl"]
    m_rk = gk[SCAN_CHUNK:] * m["incl"]
    tinv = _unit_triangular_inverse(n_ab, m)
    p1 = _dot(tinv, at)
    u0 = _dot(tinv, _dot(l_ak, v))
    p2 = rt + _dot(m_rb, p1)
    o0 = _dot(m_rb, u0) + _dot(m_rk, v)
    a_c = m["eye"] * _exp(last) + _dot_tn(bh, p1)
    g_c = _dot_tn(bh, u0) + _dot_tn(kh, v)
    return _dot(p2, h0) + o0, _dot(a_c, h0) + g_c


def _rwkv_scan_kernel(*refs):
    ins, h0_ref = refs[:12], refs[12]
    o_refs, hout_ref, h_ref = refs[13:15], refs[15], refs[16]
    c = pl.program_id(1)

    @pl.when(c == 0)
    def _():
        h_ref[...] = h0_ref[...]

    chains, revs, m = _scan_chains(RW_HEADS)
    seqs = [_Par([ins[6 * d + i][b, h] for b, d, h in chains]) for i in range(6)]
    o, h_new = _rwkv_chunk(*seqs, _Par([h_ref[b, d, h] for b, d, h in chains]), m, revs)
    for i, (b, d, h) in enumerate(chains):
        o_refs[d][b, h] = o.xs[i]
        h_ref[b, d, h] = h_new.xs[i]

    @pl.when(c == pl.num_programs(1) - 1)
    def _():
        hout_ref[...] = h_ref[...]


def _scan_specs(B, H, T, dh):
    n = T // SCAN_CHUNK
    fwd = pl.BlockSpec((SCAN_ROWS, H, SCAN_CHUNK, dh), lambda b, c: (b, 0, c, 0))
    bwd = pl.BlockSpec((SCAN_ROWS, H, SCAN_CHUNK, dh), lambda b, c: (b, 0, n - 1 - c, 0))
    state = pl.BlockSpec((SCAN_ROWS, 2, H, dh, dh), lambda b, c: (b, 0, 0, 0, 0))
    return (B // SCAN_ROWS, n), fwd, bwd, state


def _rwkv_scan_pallas(seqs, h0):
    B, H, T, dh = seqs[0][0].shape
    grid, fwd, bwd, state_spec = _scan_specs(B, H, T, dh)
    seq = jax.ShapeDtypeStruct((B, H, T, dh), F32)
    return pl.pallas_call(
        _rwkv_scan_kernel,
        grid=grid,
        in_specs=[fwd] * 6 + [bwd] * 6 + [state_spec],
        out_specs=[fwd, bwd, state_spec],
        out_shape=[seq, seq, jax.ShapeDtypeStruct((B, 2, H, dh, dh), F32)],
        scratch_shapes=[pltpu.VMEM((SCAN_ROWS, 2, H, dh, dh), F32)],
        compiler_params=pltpu.CompilerParams(dimension_semantics=("arbitrary", "arbitrary"),
                                             vmem_limit_bytes=VMEM_LIMIT_BYTES),
        name="rwkv_scan",
    )(*seqs[0], *seqs[1], h0)


def _rwkv_mixer(p, lp, S0):
    B, T, _ = p.shape
    W, H = GROUP_W, RW_HEADS
    o0 = 3 * W
    o1 = o0 + 2 * RW_DECAY_RANK
    o2 = o1 + 2 * RW_AAA_RANK
    rkv = p[..., :o0]
    wd = p[..., o0:o1].reshape(B, T, 2, RW_DECAY_RANK)
    ad = p[..., o1:o2].reshape(B, T, 2, RW_AAA_RANK)
    gate = jax.nn.sigmoid(p[..., o2:]) @ lp["rw_g2"]
    heads = lambda a: a.astype(F32).reshape(B, T, H, HEAD_DIM)
    hm = lambda a: jnp.swapaxes(a, 1, 2)
    seqs, bonuses = [], []
    for d in range(2):
        base = jnp.concatenate([rkv, wd[:, :, d], ad[:, :, d]], axis=-1)
        if d == 0:
            shifted = jnp.pad(base[:, :-1], ((0, 0), (1, 0), (0, 0)))
        else:
            shifted = jnp.pad(base[:, 1:], ((0, 0), (0, 1), (0, 0)))
        xd = base + (shifted - base) * lp["rw_mu"][d]
        r, k, v = xd[..., :W], xd[..., W:2 * W], xd[..., 2 * W:3 * W]
        wl, al = xd[..., 3 * W:3 * W + RW_DECAY_RANK], xd[..., 3 * W + RW_DECAY_RANK:]
        w = -jax.nn.softplus(-(lp["rw_w0"][d] + jnp.tanh(wl) @ lp["rw_w2"][d])) - 0.5
        log_decay = -jnp.exp(w.astype(F32))
        a = jax.nn.sigmoid((lp["rw_a0"][d] + al @ lp["rw_a2"][d]).astype(F32))
        kf = k.astype(F32)
        kk = _l2_norm(heads(kf * lp["rw_k_k"].astype(F32)))
        k_eff = heads(kf * (1.0 + (a - 1.0) * lp["rw_k_a"].astype(F32)))
        rh, vh, ah = heads(r), heads(v), heads(a)
        seqs.append([hm(rh), hm(heads(log_decay)), hm(k_eff), hm(vh), hm(-kk), hm(kk * ah)])
        bonuses.append(jnp.sum(rh * k_eff * lp["rw_r_k"].astype(F32), axis=-1, keepdims=True) * vh)
    o_f, o_b, h_fin = _rwkv_scan_pallas(seqs, jnp.swapaxes(S0.astype(F32), -1, -2))
    o = hm(o_f + o_b)
    finals = jnp.swapaxes(h_fin, -1, -2)
    mu = jnp.mean(o, axis=-1, keepdims=True)
    var = jnp.mean(jnp.square(o - mu), axis=-1, keepdims=True)
    y = ((o - mu) * lax.rsqrt(var + RW_LN_EPS)).reshape(B, T, W) * lp["rw_ln_g"].astype(F32) + lp["rw_ln_b"].astype(F32)
    y = y + (bonuses[0] + bonuses[1]).reshape(B, T, W)
    return y.astype(p.dtype) * gate, finals


def _delta_chunk(q, k, v, gb, betab, s0, m, revs):
    gc = _dot(m["incl"], gb, two_pass=True)
    last = _last_row(gc, revs)
    diff = gc - _dot_nt(m["eye"], gc, two_pass=True)
    dmat = _lift(lambda keep, d: jnp.where(keep > 0.0, jnp.exp(jnp.where(keep > 0.0, d, 0.0)), 0.0), m["incl"], diff)
    kb = k * betab
    gram = _dot_nt(_stack_rows(kb, q), k)
    a_mat = gram[:SCAN_CHUNK] * dmat * m["strict"]
    qk = gram[SCAN_CHUNK:] * dmat
    tm = _unit_triangular_inverse(-a_mat, m)
    u = _dot(tm, v * betab)
    w = _dot(tm, kb * _exp(gc))
    v_new = u - _dot(w, s0)
    o = _dot(q * _exp(gc), s0) + _dot(qk, v_new)
    return o, s0 * _exp(last) + _dot_tn(k * _exp(last - gc), v_new)


def _delta_scan_kernel(*refs):
    ins, s0_ref = refs[:10], refs[10]
    o_refs, sout_ref, s_ref = refs[11:13], refs[13], refs[14]
    c = pl.program_id(1)

    @pl.when(c == 0)
    def _():
        s_ref[...] = s0_ref[...]

    chains, revs, m = _scan_chains(DN_HEADS)
    seqs = [_Par([ins[5 * d + i][b, h] for b, d, h in chains]) for i in range(5)]
    o, s_new = _delta_chunk(*seqs, _Par([s_ref[b, d, h] for b, d, h in chains]), m, revs)
    for i, (b, d, h) in enumerate(chains):
        o_refs[d][b, h] = o.xs[i]
        s_ref[b, d, h] = s_new.xs[i]

    @pl.when(c == pl.num_programs(1) - 1)
    def _():
        sout_ref[...] = s_ref[...]


def _delta_scan_pallas(q, k, v, g, beta, s0):
    assert SCAN_CHUNK == HEAD_DIM
    B, H, T, dh = q.shape
    grid, fwd, bwd, state_spec = _scan_specs(B, H, T, dh)
    seq = jax.ShapeDtypeStruct((B, H, T, dh), F32)
    return pl.pallas_call(
        _delta_scan_kernel,
        grid=grid,
        in_specs=[fwd] * 5 + [bwd] * 5 + [state_spec],
        out_specs=[fwd, bwd, state_spec],
        out_shape=[seq, seq, jax.ShapeDtypeStruct((B, 2, H, dh, dh), F32)],
        scratch_shapes=[pltpu.VMEM((SCAN_ROWS, 2, H, dh, dh), F32)],
        compiler_params=pltpu.CompilerParams(dimension_semantics=("arbitrary", "arbitrary"),
                                             vmem_limit_bytes=VMEM_LIMIT_BYTES),
        name="delta_scan",
    )(q, k, v, g[:, 0], beta[:, 0], q, k, v, g[:, 1], beta[:, 1], s0)


def _deltanet_mixer(p, lp, S0):
    B, T, _ = p.shape
    W, H = GROUP_W, DN_HEADS
    xin = jnp.pad(p[..., :3 * W], ((0, 0), (DN_CONV // 2, DN_CONV // 2), (0, 0)))
    qkv = sum(xin[:, j:j + T] * lp["dn_conv"][j] for j in range(DN_CONV))
    qkv = jax.nn.silu(qkv)
    heads = lambda a: a.astype(F32).reshape(B, T, H, HEAD_DIM)
    hm = lambda a: jnp.swapaxes(a, 1, 2)
    q = _l2_norm(heads(qkv[..., :W])) * (HEAD_DIM ** -0.5)
    k = _l2_norm(heads(qkv[..., W:2 * W]))
    v = heads(qkv[..., 2 * W:])
    beta = jax.nn.sigmoid(p[..., 3 * W:3 * W + 2 * H].astype(F32)).reshape(B, T, 2, H)
    alpha = p[..., 3 * W + 2 * H:3 * W + 4 * H].astype(F32).reshape(B, T, 2, H)
    g = -jnp.exp(lp["dn_a_log"].astype(F32)) * jax.nn.softplus(alpha + lp["dn_dt_bias"].astype(F32))
    z = heads(p[..., 3 * W + 4 * H:])
    lanes = lambda a: jnp.broadcast_to(a.transpose(0, 2, 3, 1)[..., None], (B, 2, H, T, HEAD_DIM))
    o_f, o_b, s_fin = _delta_scan_pallas(hm(q), hm(k), hm(v), lanes(g), lanes(beta), S0.astype(F32))
    o = _rms_norm(hm(o_f + o_b), lp["dn_norm_g"]) * jax.nn.silu(z)
    return o.reshape(B, T, W).astype(p.dtype), s_fin


def _modulated_norm(x, gain, scale, shift):
    y = x * lax.rsqrt(jnp.mean(x * x, axis=-1, keepdims=True) + NORM_EPS)
    return (y * gain) * (1.0 + scale) + shift


def _proj_in_kernel(x_ref, mod_ref, gain_ref, w_ref, o_ref):
    h = _modulated_norm(x_ref[...], gain_ref[...], mod_ref[0, MOD_SCALE1:MOD_SCALE1 + 1], mod_ref[0, MOD_SHIFT1:MOD_SHIFT1 + 1])
    o_ref[...] = jnp.dot(h.astype(BF16), w_ref[...], preferred_element_type=F32)


def _proj_out_kernel(mix_ref, x_ref, mod_ref, gain_ref, w_ref, x_out_ref, h2_ref):
    y = jnp.dot(mix_ref[...].astype(BF16), w_ref[...], preferred_element_type=F32)
    x_new = x_ref[...] + mod_ref[0, MOD_GATE1:MOD_GATE1 + 1] * y
    x_out_ref[...] = x_new
    h2_ref[...] = _modulated_norm(x_new, gain_ref[...], mod_ref[0, MOD_SCALE2:MOD_SCALE2 + 1],
                                  mod_ref[0, MOD_SHIFT2:MOD_SHIFT2 + 1])


def _proj_in(x, mod, gain, w_bf, tokens_per_cond):
    n, cols = x.shape[0], w_bf.shape[1]
    tm = PROJ_TOKENS
    tok = pl.BlockSpec((tm, D_MODEL), lambda i: (i, 0))
    return pl.pallas_call(
        _proj_in_kernel,
        grid=(n // tm,),
        in_specs=[tok, _mod_spec(mod, tokens_per_cond, tm), pl.BlockSpec((1, D_MODEL), lambda i: (0, 0)),
                  pl.BlockSpec((D_MODEL, cols), lambda i: (0, 0))],
        out_specs=pl.BlockSpec((tm, cols), lambda i: (i, 0)),
        out_shape=jax.ShapeDtypeStruct((n, cols), F32),
        compiler_params=pltpu.CompilerParams(dimension_semantics=("arbitrary",), vmem_limit_bytes=VMEM_LIMIT_BYTES),
        name="proj_in",
    )(x, mod, gain, w_bf)


def _proj_out(mix, x, mod, gain, w_bf, tokens_per_cond):
    n = x.shape[0]
    tm = PROJ_TOKENS
    tok = pl.BlockSpec((tm, D_MODEL), lambda i: (i, 0))
    out = jax.ShapeDtypeStruct((n, D_MODEL), F32)
    return pl.pallas_call(
        _proj_out_kernel,
        grid=(n // tm,),
        in_specs=[tok, tok, _mod_spec(mod, tokens_per_cond, tm), pl.BlockSpec((1, D_MODEL), lambda i: (0, 0)),
                  pl.BlockSpec((D_MODEL, D_MODEL), lambda i: (0, 0))],
        out_specs=[tok, tok],
        out_shape=[out, out],
        compiler_params=pltpu.CompilerParams(dimension_semantics=("arbitrary",), vmem_limit_bytes=VMEM_LIMIT_BYTES),
        name="proj_out",
    )(mix, x, mod, gain, w_bf)


def _trunk_layer(x, cond, lp, ctx, final_gain):
    B, T, _ = x.shape
    mod = (jax.nn.silu(cond) @ lp["w_mod"] + lp["b_mod"]).reshape(-1, 6, D_MODEL)
    x = x.reshape(B * T, D_MODEL)
    proj = _proj_in(x, mod, lp["norm1_g"][None], lp["w_in_bf"], T).reshape(B, T, -1)
    pa, pb, pc, pd, _ = jnp.split(proj, [GQA_PROJ, GQA_PROJ + RW_PROJ, GQA_PROJ + RW_PROJ + NA_PROJ, IN_COLS], axis=-1)
    if ctx is None:
        ctx_a = None
        ctx_c = None
        s_rw0 = jnp.zeros((B, 2, RW_HEADS, HEAD_DIM, HEAD_DIM), F32)
        s_dn0 = jnp.zeros((B, 2, DN_HEADS, HEAD_DIM, HEAD_DIM), F32)
    else:
        ka_c, va_c, kc_c, vc_c, s_rw0, s_dn0 = ctx
        ctx_a = (ka_c, va_c)
        ctx_c = (kc_c, vc_c)
    oa, (ka, va) = _gqa_mixer(pa, lp, ctx_a)
    ob, s_rw = _rwkv_mixer(pb, lp, s_rw0)
    oc, (kc, vc) = _na_mixer(pc, lp, ctx_c)
    od, s_dn = _deltanet_mixer(pd, lp, s_dn0)
    mix = jnp.concatenate([oa, ob, oc, od], axis=-1).reshape(B * T, D_MODEL)
    x, h2 = _proj_out(mix, x, mod, lp["norm2_g"][None], lp["w_out_bf"], T)
    x = _peer_residual(x, h2, mod, lp, T, final_gain)
    return x.reshape(B, T, D_MODEL), (ka, va, kc, vc, s_rw, s_dn)


def kernel(x_prompt, x_sample, c, cache_gqa_k, cache_gqa_v, cache_na_k, cache_na_v, state_rwkv, state_delta, c_ctx, norm1_g, norm2_g, w_mod, b_mod, w_in, w_out, gqa_q_norm, gqa_k_norm, rw_mu, rw_w0, rw_w2, rw_a0, rw_a2, rw_g2, rw_k_k, rw_k_a, rw_r_k, rw_ln_g, rw_ln_b, na_bias, dn_conv, dn_a_log, dn_dt_bias, dn_norm_g, peer_wq, peer_keys, peer_u, peer_v, final_norm_g):
    depth = w_in.shape[0]

    def layer_params(l):
        return {
            "norm1_g": norm1_g[l], "norm2_g": norm2_g[l], "w_mod": w_mod[l], "b_mod": b_mod[l],
            "w_in_bf": jnp.pad(w_in[l].astype(BF16), ((0, 0), (0, IN_COLS_PADDED - IN_COLS))),
            "w_out_bf": w_out[l].astype(BF16), "gqa_q_norm": gqa_q_norm[l], "gqa_k_norm": gqa_k_norm[l],
            "rw_mu": rw_mu[l], "rw_w0": rw_w0[l], "rw_w2": rw_w2[l], "rw_a0": rw_a0[l], "rw_a2": rw_a2[l],
            "rw_g2": rw_g2[l], "rw_k_k": rw_k_k[l], "rw_k_a": rw_k_a[l], "rw_r_k": rw_r_k[l],
            "rw_ln_g": rw_ln_g[l], "rw_ln_b": rw_ln_b[l], "na_bias": na_bias[l],
            "dn_conv": dn_conv[l], "dn_a_log": dn_a_log[l], "dn_dt_bias": dn_dt_bias[l], "dn_norm_g": dn_norm_g[l],
            "peer_wq_bf": peer_wq[l].astype(BF16),
            "peer_keys_bf": peer_keys[l].reshape(2 * PEER_HEADS, PEER_NKEYS, PEER_DQ // 2).astype(BF16),
            "peer_u_bf": peer_u[l].astype(BF16),
            "peer_vt_bf": peer_v[l].astype(BF16).T,
        }

    params = [layer_params(l) for l in range(depth)]

    xp = x_prompt
    cond_ctx = c_ctx[None, :]
    per_layer = []
    closing = lambda l: final_norm_g if l == depth - 1 else None
    for l in range(depth):
        xp, st = _trunk_layer(xp, cond_ctx, params[l], None, closing(l))
        per_layer.append(st)
    y_prompt = xp

    xs = x_sample
    for l in range(depth):
        ctx = (cache_gqa_k[:, l], cache_gqa_v[:, l], cache_na_k[:, l], cache_na_v[:, l],
               state_rwkv[:, l], state_delta[:, l])
        xs, _ = _trunk_layer(xs, c, params[l], ctx, closing(l))
    y_sample = xs

    dt = x_prompt.dtype
    new_gqa_k = jnp.stack([st[0] for st in per_layer], axis=1).astype(dt)
    new_gqa_v = jnp.stack([st[1] for st in per_layer], axis=1).astype(dt)
    new_na_k = jnp.stack([st[2] for st in per_layer], axis=1).astype(dt)
    new_na_v = jnp.stack([st[3] for st in per_layer], axis=1).astype(dt)
    new_state_rwkv = jnp.stack([st[4] for st in per_layer], axis=1).astype(dt)
    new_state_delta = jnp.stack([st[5] for st in per_layer], axis=1).astype(dt)
    return (y_prompt, y_sample, new_gqa_k, new_gqa_v, new_na_k, new_na_v, new_state_rwkv, new_state_delta)
```

```python
import functools
import math

import jax
import jax.numpy as jnp
import numpy as np
from jax import lax
from jax.experimental import pallas as pl
from jax.experimental.pallas import tpu as pltpu

F32 = jnp.float32
BF16 = jnp.bfloat16

D_MODEL = 1024
GRID_W = 64
HEAD_DIM = 64
GROUP_W = D_MODEL // 4
NORM_EPS = 1e-6
NEG_INF = -1e30
GQA_HEADS = GROUP_W // HEAD_DIM
GQA_KV_HEADS = GQA_HEADS // 2
ROPE_THETA = 10000.0
RW_HEADS = GROUP_W // HEAD_DIM
RW_DECAY_RANK = 32
RW_AAA_RANK = 32
RW_GATE_RANK = 64
RW_LN_EPS = 64e-5
NA_HEADS = GROUP_W // HEAD_DIM
NA_ROWS = 8
NA_COLS = 16
DN_HEADS = GROUP_W // HEAD_DIM
DN_CONV = 5
PEER_HEADS = 8
PEER_NKEYS = 128
PEER_EXPERTS = PEER_NKEYS * PEER_NKEYS
PEER_DQ = 256
PEER_TOPK = 16
GQA_PROJ = GROUP_W + 2 * GQA_KV_HEADS * HEAD_DIM
RW_PROJ = 3 * GROUP_W + 2 * RW_DECAY_RANK + 2 * RW_AAA_RANK + RW_GATE_RANK
NA_PROJ = 3 * GROUP_W
DN_PROJ = 3 * GROUP_W + 4 * DN_HEADS + GROUP_W

LANES = 128
VMEM_LIMIT_BYTES = 56 * 1024 * 1024

PEER_SEL_TOKENS = 128
PEER_TOKENS = 256
PEER_ROWS_PER_STEP = 16
PEER_SUB_ROWS = 2
PEER_CHUNK = PEER_ROWS_PER_STEP * PEER_NKEYS
PEER_STEPS = PEER_EXPERTS // PEER_CHUNK
INV_SQRT2 = 1.0 / math.sqrt(2.0)

IN_COLS = GQA_PROJ + RW_PROJ + NA_PROJ + DN_PROJ
IN_COLS_PADDED = -(-IN_COLS // LANES) * LANES
MOD_SHIFT1, MOD_SCALE1, MOD_GATE1, MOD_SHIFT2, MOD_SCALE2, MOD_GATE2 = range(6)
PROJ_TOKENS = 512

ATTN_SCALE = HEAD_DIM ** -0.5
ATTN_Q_TILE = 256
NA_ROW_BLOCK = 8

SCAN_CHUNK = 64
SCAN_ROWS = 2


class _Par:
    def __init__(self, xs):
        self.xs = list(xs)

    def __add__(self, o):
        return _lift(jnp.add, self, o)

    __radd__ = __add__

    def __sub__(self, o):
        return _lift(jnp.subtract, self, o)

    def __rsub__(self, o):
        return _lift(lambda x, y: y - x, self, o)

    def __mul__(self, o):
        return _lift(jnp.multiply, self, o)

    __rmul__ = __mul__

    def __neg__(self):
        return _lift(jnp.negative, self)

    def __getitem__(self, idx):
        return _lift(lambda x: x[idx], self)


def _lift(f, *args):
    n = next(len(a.xs) for a in args if isinstance(a, _Par))
    return _Par([f(*[a.xs[i] if isinstance(a, _Par) else a for a in args]) for i in range(n)])


def _exp(x):
    return _lift(jnp.exp, x)


def _stack_rows(x, y):
    return _lift(lambda a, b: jnp.concatenate([a, b], axis=0), x, y)


def _top_k_rows(s, k):
    rows = s.shape[0]
    iota = lax.broadcasted_iota(jnp.int32, s.shape, 0).astype(F32)
    vals = []
    work = s
    for _ in range(k):
        m = jnp.max(work, axis=0, keepdims=True)
        first = jnp.min(jnp.where(work == m, iota, float(rows)), axis=0, keepdims=True)
        work = jnp.where(iota == first, -jnp.inf, work)
        vals.append(m)
    return vals, work


def _peer_select_kernel(h_ref, wq_ref, keys_ref, thr_ref, e1_ref, s2m_ref, e2_ref):
    x = h_ref[...].astype(BF16)
    q = jnp.dot(x, wq_ref[...], preferred_element_type=F32)
    half = PEER_DQ // 2
    for h in range(PEER_HEADS):
        scores, tops, sel = [], [], []
        for p in range(2):
            c0 = (2 * h + p) * half
            qhp = q[:, c0:c0 + half].astype(BF16)
            s = lax.dot_general(keys_ref[2 * h + p], qhp, (((1,), (1,)), ((), ())),
                                preferred_element_type=F32)
            vals, work = _top_k_rows(s, PEER_TOPK)
            scores.append(s)
            tops.append(vals)
            sel.append(work == -jnp.inf)
        m1, m2 = tops
        m2_all = jnp.concatenate(m2, axis=0)
        cand = jnp.concatenate([m1[0] + m2_all] + [m1[i] + m2_all[:8] for i in range(1, 8)]
                               + [jnp.concatenate(m1[8:], axis=0) + m2[0]], axis=0)
        top_s, _ = _top_k_rows(cand, PEER_TOPK)
        mx = top_s[0]
        z = jnp.exp(top_s[0] - mx)
        for r in range(1, PEER_TOPK):
            z = z + jnp.exp(top_s[r] - mx)
        half_inv_z = 0.5 / z
        t3 = top_s[PEER_TOPK - 1]
        s1m = jnp.where(sel[0], scores[0], -jnp.inf)
        thr = jnp.full_like(s1m, jnp.inf)
        for j in range(PEER_TOPK):
            thr = jnp.where(s1m + m2[j] >= t3, m2[j], thr)
        thr_ref[h] = thr
        e1_ref[h] = jnp.where(sel[0], jnp.exp(scores[0] - m1[0]) * half_inv_z, 0.0)
        s2m_ref[h] = jnp.where(sel[1], scores[1], -jnp.inf)
        e2_ref[h] = jnp.exp(scores[1] - m2[0])


def _peer_select(h2, wq_bf, keys_bf):
    n = h2.shape[0]
    tt = PEER_SEL_TOKENS
    big = jax.ShapeDtypeStruct((PEER_HEADS, PEER_NKEYS, n), F32)
    big_spec = pl.BlockSpec((PEER_HEADS, PEER_NKEYS, tt), lambda i: (0, 0, i))
    return pl.pallas_call(
        _peer_select_kernel,
        grid=(n // tt,),
        in_specs=[
            pl.BlockSpec((tt, D_MODEL), lambda i: (i, 0)),
            pl.BlockSpec((D_MODEL, PEER_HEADS * PEER_DQ), lambda i: (0, 0)),
            pl.BlockSpec((2 * PEER_HEADS, PEER_NKEYS, PEER_DQ // 2), lambda i: (0, 0, 0)),
        ],
        out_specs=[big_spec] * 4,
        out_shape=[big] * 4,
        compiler_params=pltpu.CompilerParams(dimension_semantics=("arbitrary",),
                                             vmem_limit_bytes=VMEM_LIMIT_BYTES),
        name="peer_select",
    )(h2, wq_bf, keys_bf)


def _peer_dense_kernel(h_ref, u_ref, vt_ref, thr_ref, e1_ref, s2m_ref, e2_ref, xres_ref, mod_ref, fin_ref,
                       out_ref, ht_ref, acc_ref, p_ref, *, final_norm):
    j = pl.program_id(1)
    tt = h_ref.shape[0]

    @pl.when(j == 0)
    def _():
        ht_ref[...] = h_ref[...].T.astype(BF16)
        acc_ref[...] = jnp.zeros_like(acc_ref)

    sub = PEER_SUB_ROWS * PEER_NKEYS

    def activations(k):
        return jnp.dot(u_ref[k * sub:(k + 1) * sub, :], ht_ref[...], preferred_element_type=F32)

    def weighted(k, act, slot):
        for ai in range(PEER_SUB_ROWS):
            a = k * PEER_SUB_ROWS + ai
            rows = slice(ai * PEER_NKEYS, (ai + 1) * PEER_NKEYS)
            for c in range(tt // LANES):
                cs = slice(c * LANES, (c + 1) * LANES)
                zz = act[rows, cs]
                gelu2 = zz * (1.0 + lax.erf(zz * INV_SQRT2))
                w = jnp.zeros((PEER_NKEYS, LANES), F32)
                for h in range(PEER_HEADS):
                    picked = s2m_ref[h, :, cs] >= thr_ref[h, a:a + 1, cs]
                    w = w + jnp.where(picked, e2_ref[h, :, cs], 0.0) * e1_ref[h, a:a + 1, cs]
                p_ref[slot, rows, cs] = (w * gelu2).astype(BF16)

    n_sub = PEER_ROWS_PER_STEP // PEER_SUB_ROWS
    act = activations(0)
    for k in range(n_sub):
        nxt = activations(k + 1) if k + 1 < n_sub else None
        weighted(k, act, k % 2)
        acc_ref[...] += jnp.dot(vt_ref[:, k * sub:(k + 1) * sub], p_ref[k % 2], preferred_element_type=F32)
        act = nxt

    @pl.when(j == PEER_STEPS - 1)
    def _():
        y = xres_ref[...] + mod_ref[0, MOD_GATE2:MOD_GATE2 + 1] * acc_ref[...].T
        if final_norm:
            y = y * lax.rsqrt(jnp.mean(y * y, axis=-1, keepdims=True) + NORM_EPS) * fin_ref[...]
        out_ref[...] = y


def _mod_spec(mod, tokens_per_cond, tile):
    tiles_per_cond = tokens_per_cond // tile
    if mod.shape[0] == 1:
        return pl.BlockSpec((1, 6, D_MODEL), lambda i, *_: (0, 0, 0))
    assert tokens_per_cond % tile == 0
    return pl.BlockSpec((1, 6, D_MODEL), lambda i, *_: (i // tiles_per_cond, 0, 0))


def _peer_dense(h2, u_bf, vt_bf, sel, xres, mod, tokens_per_cond, final_gain):
    n = h2.shape[0]
    tt = PEER_TOKENS
    thr, e1, s2m, e2 = sel
    fin = jnp.ones((1, D_MODEL), F32) if final_gain is None else final_gain.astype(F32)[None]
    row_spec = pl.BlockSpec((PEER_HEADS, PEER_ROWS_PER_STEP, tt), lambda i, j: (0, j, i))
    col_spec = pl.BlockSpec((PEER_HEADS, PEER_NKEYS, tt), lambda i, j: (0, 0, i))
    tok_spec = pl.BlockSpec((tt, D_MODEL), lambda i, j: (i, 0))
    return pl.pallas_call(
        functools.partial(_peer_dense_kernel, final_norm=final_gain is not None),
        grid=(n // tt, PEER_STEPS),
        in_specs=[
            tok_spec,
            pl.BlockSpec((PEER_CHUNK, D_MODEL), lambda i, j: (j, 0)),
            pl.BlockSpec((D_MODEL, PEER_CHUNK), lambda i, j: (0, j)),
            row_spec, row_spec, col_spec, col_spec,
            tok_spec,
            _mod_spec(mod, tokens_per_cond, tt),
            pl.BlockSpec((1, D_MODEL), lambda i, j: (0, 0)),
        ],
        out_specs=tok_spec,
        out_shape=jax.ShapeDtypeStruct((n, D_MODEL), F32),
        scratch_shapes=[pltpu.VMEM((D_MODEL, tt), BF16), pltpu.VMEM((D_MODEL, tt), F32),
                        pltpu.VMEM((2, PEER_SUB_ROWS * PEER_NKEYS, tt), BF16)],
        compiler_params=pltpu.CompilerParams(dimension_semantics=("arbitrary", "arbitrary"),
                                             vmem_limit_bytes=VMEM_LIMIT_BYTES),
        name="peer_dense",
    )(h2, u_bf, vt_bf, thr, e1, s2m, e2, xres, mod, fin)


def _peer_residual(x, h2, mod, lp, tokens_per_cond, final_gain):
    sel = _peer_select(h2, lp["peer_wq_bf"], lp["peer_keys_bf"])
    return _peer_dense(h2, lp["peer_u_bf"], lp["peer_vt_bf"], sel, x, mod, tokens_per_cond, final_gain)


def _rms_norm(x, g):
    xf = x.astype(F32)
    y = xf * lax.rsqrt(jnp.mean(xf * xf, axis=-1, keepdims=True) + NORM_EPS)
    return (y * g.astype(F32)).astype(x.dtype)


def _l2_norm(x):
    return x * lax.rsqrt(jnp.sum(x * x, axis=-1, keepdims=True) + NORM_EPS)


def _axial_rope_tables(T):
    t = np.arange(T)
    row, col = t // GRID_W, t % GRID_W
    n_freq = HEAD_DIM // 4
    inv = ROPE_THETA ** (-2.0 * np.arange(n_freq) / (HEAD_DIM // 2))
    ang = np.concatenate([row[:, None] * inv[None], col[:, None] * inv[None]], axis=1)
    return jnp.asarray(np.cos(ang), F32), jnp.asarray(np.sin(ang), F32)


def _apply_rope(x, cos, sin):
    xf = x.astype(F32)
    half = HEAD_DIM // 2
    x1, x2 = xf[..., :half], xf[..., half:]
    c, s = cos[None, :, None, :], sin[None, :, None, :]
    return jnp.concatenate([x1 * c - x2 * s, x2 * c + x1 * s], axis=-1).astype(x.dtype)


def _attention_kernel(q_ref, k_ref, v_ref, o_ref, *, group):
    outs = []
    for h in range(q_ref.shape[2] // HEAD_DIM):
        q = q_ref[0, :, h * HEAD_DIM:(h + 1) * HEAD_DIM].astype(BF16)
        s = lax.dot_general(q, k_ref[0, h // group], (((1,), (1,)), ((), ())), preferred_element_type=F32) * ATTN_SCALE
        m = jnp.max(s, axis=-1, keepdims=True)
        p = jnp.exp(s - m)
        l = jnp.sum(p, axis=-1, keepdims=True)
        outs.append(jnp.dot(p.astype(BF16), v_ref[0, h // group], preferred_element_type=F32) / l)
    o_ref[0] = jnp.concatenate(outs, axis=1)


def _attention(q, k, v):
    B, T, width = q.shape
    hkv, L, dh = k.shape[1], k.shape[2], k.shape[3]
    group = width // dh // hkv
    tq = min(T, ATTN_Q_TILE)
    kv_spec = pl.BlockSpec((1, hkv, L, dh), lambda b, i: (b, 0, 0, 0))
    q_spec = pl.BlockSpec((1, tq, width), lambda b, i: (b, i, 0))
    return pl.pallas_call(
        functools.partial(_attention_kernel, group=group),
        grid=(B, T // tq),
        in_specs=[q_spec, kv_spec, kv_spec],
        out_specs=q_spec,
        out_shape=jax.ShapeDtypeStruct((B, T, width), F32),
        compiler_params=pltpu.CompilerParams(dimension_semantics=("arbitrary",) * 2,
                                             vmem_limit_bytes=VMEM_LIMIT_BYTES),
        name="attention",
    )(q, k.astype(BF16), v.astype(BF16))


def _na_bias_table(bias):
    qcols = np.arange(GRID_W)
    kcols = np.arange(GRID_W)
    cstart = np.clip(qcols - NA_COLS // 2, 0, GRID_W - NA_COLS)
    valid = (kcols[None, :] >= cstart[:, None]) & (kcols[None, :] < cstart[:, None] + NA_COLS)
    dc_idx = np.clip(kcols[None, :] - qcols[:, None], -(NA_COLS - 1), NA_COLS - 1) + (NA_COLS - 1)
    onehot = jnp.asarray(dc_idx[:, :, None] == np.arange(2 * NA_COLS - 1), F32)
    rel = jnp.einsum("qkc,hrc->hrqk", onehot, bias.astype(F32), precision=lax.Precision.HIGHEST)
    rel = jnp.where(valid[None, None], rel, NEG_INF)
    return jnp.stack([jnp.concatenate([rel[:, d0 + j] for j in range(NA_ROWS)], axis=-1)
                      for d0 in range(NA_ROWS)], axis=1)


def _na_kernel(q_ref, k_ref, v_ref, kc_ref, vc_ref, bias_ref, o_ref):
    blk = pl.program_id(1)
    n_rows = k_ref.shape[2] // GRID_W
    win = NA_ROWS * GRID_W
    nt = (((1,), (1,)), ((), ()))
    rows = []
    for i in range(NA_ROW_BLOCK):
        r = blk * NA_ROW_BLOCK + i
        start = jnp.clip(r - NA_ROWS // 2, 0, n_rows - NA_ROWS)
        rows.append((pl.multiple_of(start * GRID_W, GRID_W), start - r + (NA_ROWS - 1)))
    outs = []
    for h in range(NA_HEADS):
        q = _Par([q_ref[0, i * GRID_W:(i + 1) * GRID_W, h * HEAD_DIM:(h + 1) * HEAD_DIM].astype(BF16)
                  for i in range(NA_ROW_BLOCK)])
        k = _Par([k_ref[0, h, pl.ds(off, win), :] for off, _ in rows])
        v = _Par([v_ref[0, h, pl.ds(off, win), :] for off, _ in rows])
        bias = _Par([bias_ref[h, d0] for _, d0 in rows])
        s_loc = _lift(lambda q, k, b: lax.dot_general(q, k, nt, preferred_element_type=F32) * ATTN_SCALE + b,
                      q, k, bias)
        s_ctx = _lift(lambda q: lax.dot_general(q, kc_ref[0, h], nt, preferred_element_type=F32) * ATTN_SCALE, q)
        m = _lift(lambda a, b: jnp.maximum(jnp.max(a, axis=-1, keepdims=True), jnp.max(b, axis=-1, keepdims=True)),
                  s_loc, s_ctx)
        p_loc = _exp(s_loc - m)
        p_ctx = _exp(s_ctx - m)
        l = _lift(lambda a, b: jnp.sum(a, axis=-1, keepdims=True) + jnp.sum(b, axis=-1, keepdims=True), p_loc, p_ctx)
        o = _lift(lambda pl_, v, pc, l: (jnp.dot(pl_.astype(BF16), v, preferred_element_type=F32)
                                         + jnp.dot(pc.astype(BF16), vc_ref[0, h], preferred_element_type=F32)) / l,
                  p_loc, v, p_ctx, l)
        outs.append(jnp.concatenate(o.xs, axis=0))
    o_ref[0] = jnp.concatenate(outs, axis=1)


def _neighbourhood_attention(q, k, v, bias, k_ctx, v_ctx):
    B, H, T, dh = k.shape
    P = k_ctx.shape[2]
    rows = T // GRID_W
    assert rows >= NA_ROWS and rows % NA_ROW_BLOCK == 0
    tq = NA_ROW_BLOCK * GRID_W
    full = lambda n: pl.BlockSpec((1, H, n, dh), lambda b, i: (b, 0, 0, 0))
    q_spec = pl.BlockSpec((1, tq, H * dh), lambda b, i: (b, i, 0))
    return pl.pallas_call(
        _na_kernel,
        grid=(B, rows // NA_ROW_BLOCK),
        in_specs=[q_spec, full(T), full(T), full(P), full(P),
                  pl.BlockSpec((H, NA_ROWS, GRID_W, NA_ROWS * GRID_W), lambda b, i: (0, 0, 0, 0))],
        out_specs=q_spec,
        out_shape=jax.ShapeDtypeStruct((B, T, H * dh), F32),
        compiler_params=pltpu.CompilerParams(dimension_semantics=("arbitrary",) * 2,
                                             vmem_limit_bytes=VMEM_LIMIT_BYTES),
        name="na_attention",
    )(q, k.astype(BF16), v.astype(BF16), k_ctx.astype(BF16), v_ctx.astype(BF16), _na_bias_table(bias))


def _gqa_mixer(p, lp, ctx_kv):
    B, T, _ = p.shape
    kvw = GQA_KV_HEADS * HEAD_DIM
    q = _rms_norm(p[..., :GROUP_W].reshape(B, T, GQA_HEADS, HEAD_DIM), lp["gqa_q_norm"])
    k = _rms_norm(p[..., GROUP_W:GROUP_W + kvw].reshape(B, T, GQA_KV_HEADS, HEAD_DIM), lp["gqa_k_norm"])
    v = p[..., GROUP_W + kvw:].reshape(B, T, GQA_KV_HEADS, HEAD_DIM)
    if ctx_kv is not None:
        cos, sin = _axial_rope_tables(T)
        q = _apply_rope(q, cos, sin)
        k = _apply_rope(k, cos, sin)
    qf = q.reshape(B, T, GROUP_W)
    kh = k.transpose(0, 2, 1, 3)
    vh = v.transpose(0, 2, 1, 3)
    if ctx_kv is None:
        o = _attention(qf, kh, vh)
    else:
        k_ctx, v_ctx = ctx_kv
        o = _attention(qf, jnp.concatenate([kh, k_ctx], axis=2), jnp.concatenate([vh, v_ctx], axis=2))
    return o, (kh, vh)


def _na_mixer(p, lp, ctx_kv):
    B, T, _ = p.shape
    q = p[..., :GROUP_W]
    k, v = [p[..., i * GROUP_W:(i + 1) * GROUP_W].reshape(B, T, NA_HEADS, HEAD_DIM).transpose(0, 2, 1, 3)
            for i in (1, 2)]
    if ctx_kv is None:
        o = _attention(q, k, v)
    else:
        o = _neighbourhood_attention(q, k, v, lp["na_bias"], ctx_kv[0], ctx_kv[1])
    return o, (k, v)


def _mm(a, b, dims, two_pass):
    f = lambda x, y: lax.dot_general(x, y, (dims, ((), ())), preferred_element_type=F32)

    def one(a, b):
        a16 = a.astype(BF16)
        hi = b.astype(BF16)
        if not two_pass:
            return f(a16, hi)
        return f(a16, hi) + f(a16, (b - hi.astype(F32)).astype(BF16))

    return _lift(one, a, b)


def _dot(a, b, two_pass=False):
    return _mm(a, b, ((1,), (0,)), two_pass)


def _dot_nt(a, b, two_pass=False):
    return _mm(a, b, ((1,), (1,)), two_pass)


def _dot_tn(a, b):
    return _mm(a, b, ((0,), (0,)), False)


def _chunk_masks(reverse):
    row = lax.broadcasted_iota(jnp.int32, (SCAN_CHUNK, SCAN_CHUNK), 0)
    col = lax.broadcasted_iota(jnp.int32, (SCAN_CHUNK, SCAN_CHUNK), 1)
    earlier = (row < col) if reverse else (row > col)
    eye = row == col
    blk16 = (row >> 4) == (col >> 4)
    blk32 = (row >> 5) == (col >> 5)
    return dict(
        strict=earlier.astype(F32), incl=(earlier | eye).astype(F32), eye=eye.astype(F32),
        d16=blk16.astype(F32), off32=(blk32 & jnp.logical_not(blk16)).astype(F32),
        off64=jnp.logical_not(blk32).astype(F32))


def _scan_chains(heads):
    fwd, bwd = _chunk_masks(False), _chunk_masks(True)
    chains = [(b, d, h) for b in range(SCAN_ROWS) for d in range(2) for h in range(heads)]
    masks = {key: _Par([(bwd if d else fwd)[key] for _, d, _ in chains]) for key in fwd}
    return chains, [d == 1 for _, d, _ in chains], masks


def _last_row(cum, revs):
    return _Par([c[0:1] if rev else c[SCAN_CHUNK - 1:SCAN_CHUNK] for c, rev in zip(cum.xs, revs)])


def _unit_triangular_inverse(n, m):
    eye = m["eye"]
    nd = n * m["d16"]
    n2 = _dot(nd, nd)
    x = eye + nd
    x = x + _dot(x, n2)
    n4 = _dot(n2, n2)
    x = x + _dot(x, n4)
    n8 = _dot(n4, n4)
    x = x + _dot(x, n8)
    x = x + _dot(x, _dot(n * m["off32"], x))
    x = x + _dot(x, _dot(n * m["off64"], x))
    return x


def _rwkv_chunk(r, lw, k, v, a, b, h0, m, revs):
    cum = _dot(m["incl"], lw, two_pass=True)
    last = _last_row(cum, revs)
    g_inv = _exp(-cum)
    at = a * _exp(cum - lw)
    rt = r * _exp(cum)
    bt = b * g_inv
    kt = k * g_inv
    to_end = _exp(last - cum)
    bh = b * to_end
    kh = k * to_end
    ar = _stack_rows(at, rt)
    gb = _dot_nt(ar, bt)
    gk = _dot_nt(ar, kt)
    n_ab = gb[:SCAN_CHUNK] * m["strict"]
    l_ak = gk[:SCAN_CHUNK] * m["strict"]
    m_rb = gb[SCAN_CHUNK:] * m["incl"]
    m_rk = gk[SCAN_CHUNK:] * m["incl"]
    tinv = _unit_triangular_inverse(n_ab, m)
    p1 = _dot(tinv, at)
    u0 = _dot(tinv, _dot(l_ak, v))
    p2 = rt + _dot(m_rb, p1)
    o0 = _dot(m_rb, u0) + _dot(m_rk, v)
    a_c = m["eye"] * _exp(last) + _dot_tn(bh, p1)
    g_c = _dot_tn(bh, u0) + _dot_tn(kh, v)
    return _dot(p2, h0) + o0, _dot(a_c, h0) + g_c


def _rwkv_scan_kernel(*refs):
    ins, h0_ref = refs[:12], refs[12]
    o_refs, hout_ref, h_ref = refs[13:15], refs[15], refs[16]
    c = pl.program_id(1)

    @pl.when(c == 0)
    def _():
        h_ref[...] = h0_ref[...]

    chains, revs, m = _scan_chains(RW_HEADS)
    seqs = [_Par([_head_tile(ins[6 * d + i], b, h) for b, d, h in chains]) for i in range(6)]
    o, h_new = _rwkv_chunk(*seqs, _Par([h_ref[b, d, h] for b, d, h in chains]), m, revs)
    _store_heads(o_refs, o, chains)
    for i, (b, d, h) in enumerate(chains):
        h_ref[b, d, h] = h_new.xs[i]

    @pl.when(c == pl.num_programs(1) - 1)
    def _():
        hout_ref[...] = h_ref[...]


def _head_tile(ref, b, h):
    return ref[b, :, h * HEAD_DIM:(h + 1) * HEAD_DIM]


def _store_heads(o_refs, o, chains):
    for d in range(2):
        for b in range(SCAN_ROWS):
            tiles = [o.xs[i] for i, (bi, di, _) in enumerate(chains) if (bi, di) == (b, d)]
            o_refs[d][b] = jnp.concatenate(tiles, axis=1)


def _scan_specs(B, H, T, dh):
    n = T // SCAN_CHUNK
    fwd = pl.BlockSpec((SCAN_ROWS, SCAN_CHUNK, H * dh), lambda b, c: (b, c, 0))
    bwd = pl.BlockSpec((SCAN_ROWS, SCAN_CHUNK, H * dh), lambda b, c: (b, n - 1 - c, 0))
    state = pl.BlockSpec((SCAN_ROWS, 2, H, dh, dh), lambda b, c: (b, 0, 0, 0, 0))
    return (B // SCAN_ROWS, n), fwd, bwd, state


def _rwkv_scan_pallas(seqs, h0):
    B, T, _ = seqs[0][0].shape
    H, dh = RW_HEADS, HEAD_DIM
    grid, fwd, bwd, state_spec = _scan_specs(B, H, T, dh)
    seq = jax.ShapeDtypeStruct((B, T, H * dh), F32)
    return pl.pallas_call(
        _rwkv_scan_kernel,
        grid=grid,
        in_specs=[fwd] * 6 + [bwd] * 6 + [state_spec],
        out_specs=[fwd, bwd, state_spec],
        out_shape=[seq, seq, jax.ShapeDtypeStruct((B, 2, H, dh, dh), F32)],
        scratch_shapes=[pltpu.VMEM((SCAN_ROWS, 2, H, dh, dh), F32)],
        compiler_params=pltpu.CompilerParams(dimension_semantics=("arbitrary", "arbitrary"),
                                             vmem_limit_bytes=VMEM_LIMIT_BYTES),
        name="rwkv_scan",
    )(*seqs[0], *seqs[1], h0)


def _rwkv_mixer(p, lp, S0):
    B, T, _ = p.shape
    W, H = GROUP_W, RW_HEADS
    o0 = 3 * W
    o1 = o0 + 2 * RW_DECAY_RANK
    o2 = o1 + 2 * RW_AAA_RANK
    rkv = p[..., :o0]
    wd = p[..., o0:o1].reshape(B, T, 2, RW_DECAY_RANK)
    ad = p[..., o1:o2].reshape(B, T, 2, RW_AAA_RANK)
    gate = jax.nn.sigmoid(p[..., o2:]) @ lp["rw_g2"]
    heads = lambda a: a.astype(F32).reshape(B, T, H, HEAD_DIM)
    flat = lambda a: a.reshape(B, T, W)
    seqs, bonuses = [], []
    for d in range(2):
        base = jnp.concatenate([rkv, wd[:, :, d], ad[:, :, d]], axis=-1)
        if d == 0:
            shifted = jnp.pad(base[:, :-1], ((0, 0), (1, 0), (0, 0)))
        else:
            shifted = jnp.pad(base[:, 1:], ((0, 0), (0, 1), (0, 0)))
        xd = base + (shifted - base) * lp["rw_mu"][d]
        r, k, v = xd[..., :W], xd[..., W:2 * W], xd[..., 2 * W:3 * W]
        wl, al = xd[..., 3 * W:3 * W + RW_DECAY_RANK], xd[..., 3 * W + RW_DECAY_RANK:]
        w = -jax.nn.softplus(-(lp["rw_w0"][d] + jnp.tanh(wl) @ lp["rw_w2"][d])) - 0.5
        log_decay = -jnp.exp(w.astype(F32))
        a = jax.nn.sigmoid((lp["rw_a0"][d] + al @ lp["rw_a2"][d]).astype(F32))
        kf = k.astype(F32)
        kk = _l2_norm(heads(kf * lp["rw_k_k"].astype(F32)))
        k_eff = heads(kf * (1.0 + (a - 1.0) * lp["rw_k_a"].astype(F32)))
        rh, vh, ah = heads(r), heads(v), heads(a)
        seqs.append([r.astype(F32), log_decay, flat(k_eff), v.astype(F32), flat(-kk), flat(kk * ah)])
        bonuses.append(jnp.sum(rh * k_eff * lp["rw_r_k"].astype(F32), axis=-1, keepdims=True) * vh)
    o_f, o_b, h_fin = _rwkv_scan_pallas(seqs, jnp.swapaxes(S0.astype(F32), -1, -2))
    o = heads(o_f + o_b)
    finals = jnp.swapaxes(h_fin, -1, -2)
    mu = jnp.mean(o, axis=-1, keepdims=True)
    var = jnp.mean(jnp.square(o - mu), axis=-1, keepdims=True)
    y = ((o - mu) * lax.rsqrt(var + RW_LN_EPS)).reshape(B, T, W) * lp["rw_ln_g"].astype(F32) + lp["rw_ln_b"].astype(F32)
    y = y + (bonuses[0] + bonuses[1]).reshape(B, T, W)
    return y.astype(p.dtype) * gate, finals


def _delta_chunk(q, k, v, gb, betab, s0, m, revs):
    gc = _dot(m["incl"], gb, two_pass=True)
    last = _last_row(gc, revs)
    diff = gc - _dot_nt(m["eye"], gc, two_pass=True)
    dmat = _lift(lambda keep, d: jnp.where(keep > 0.0, jnp.exp(jnp.where(keep > 0.0, d, 0.0)), 0.0), m["incl"], diff)
    kb = k * betab
    gram = _dot_nt(_stack_rows(kb, q), k)
    a_mat = gram[:SCAN_CHUNK] * dmat * m["strict"]
    qk = gram[SCAN_CHUNK:] * dmat
    tm = _unit_triangular_inverse(-a_mat, m)
    u = _dot(tm, v * betab)
    w = _dot(tm, kb * _exp(gc))
    v_new = u - _dot(w, s0)
    o = _dot(q * _exp(gc), s0) + _dot(qk, v_new)
    return o, s0 * _exp(last) + _dot_tn(k * _exp(last - gc), v_new)


def _delta_scan_kernel(*refs):
    ins, s0_ref = refs[:10], refs[10]
    o_refs, sout_ref, s_ref = refs[11:13], refs[13], refs[14]
    c = pl.program_id(1)

    @pl.when(c == 0)
    def _():
        s_ref[...] = s0_ref[...]

    chains, revs, m = _scan_chains(DN_HEADS)
    seqs = [_Par([_head_tile(ins[5 * d + i], b, h) for b, d, h in chains]) for i in range(5)]
    o, s_new = _delta_chunk(*seqs, _Par([s_ref[b, d, h] for b, d, h in chains]), m, revs)
    _store_heads(o_refs, o, chains)
    for i, (b, d, h) in enumerate(chains):
        s_ref[b, d, h] = s_new.xs[i]

    @pl.when(c == pl.num_programs(1) - 1)
    def _():
        sout_ref[...] = s_ref[...]


def _delta_scan_pallas(q, k, v, g, beta, s0):
    assert SCAN_CHUNK == HEAD_DIM
    B, T, _ = q.shape
    H, dh = DN_HEADS, HEAD_DIM
    grid, fwd, bwd, state_spec = _scan_specs(B, H, T, dh)
    seq = jax.ShapeDtypeStruct((B, T, H * dh), F32)
    return pl.pallas_call(
        _delta_scan_kernel,
        grid=grid,
        in_specs=[fwd] * 5 + [bwd] * 5 + [state_spec],
        out_specs=[fwd, bwd, state_spec],
        out_shape=[seq, seq, jax.ShapeDtypeStruct((B, 2, H, dh, dh), F32)],
        scratch_shapes=[pltpu.VMEM((SCAN_ROWS, 2, H, dh, dh), F32)],
        compiler_params=pltpu.CompilerParams(dimension_semantics=("arbitrary", "arbitrary"),
                                             vmem_limit_bytes=VMEM_LIMIT_BYTES),
        name="delta_scan",
    )(q, k, v, g[0], beta[0], q, k, v, g[1], beta[1], s0)


def _deltanet_mixer(p, lp, S0):
    B, T, _ = p.shape
    W, H = GROUP_W, DN_HEADS
    xin = jnp.pad(p[..., :3 * W], ((0, 0), (DN_CONV // 2, DN_CONV // 2), (0, 0)))
    qkv = sum(xin[:, j:j + T] * lp["dn_conv"][j] for j in range(DN_CONV))
    qkv = jax.nn.silu(qkv)
    heads = lambda a: a.astype(F32).reshape(B, T, H, HEAD_DIM)
    flat = lambda a: a.reshape(B, T, W)
    q = _l2_norm(heads(qkv[..., :W])) * (HEAD_DIM ** -0.5)
    k = _l2_norm(heads(qkv[..., W:2 * W]))
    v = heads(qkv[..., 2 * W:])
    beta = jax.nn.sigmoid(p[..., 3 * W:3 * W + 2 * H].astype(F32)).reshape(B, T, 2, H)
    alpha = p[..., 3 * W + 2 * H:3 * W + 4 * H].astype(F32).reshape(B, T, 2, H)
    g = -jnp.exp(lp["dn_a_log"].astype(F32)) * jax.nn.softplus(alpha + lp["dn_dt_bias"].astype(F32))
    z = heads(p[..., 3 * W + 4 * H:])
    lanes = lambda a: [jnp.repeat(a[:, :, d], HEAD_DIM, axis=-1) for d in range(2)]
    o_f, o_b, s_fin = _delta_scan_pallas(flat(q), flat(k), flat(v), lanes(g), lanes(beta), S0.astype(F32))
    o = _rms_norm(heads(o_f + o_b), lp["dn_norm_g"]) * jax.nn.silu(z)
    return o.reshape(B, T, W).astype(p.dtype), s_fin


def _modulated_norm(x, gain, scale, shift):
    y = x * lax.rsqrt(jnp.mean(x * x, axis=-1, keepdims=True) + NORM_EPS)
    return (y * gain) * (1.0 + scale) + shift


def _proj_in_kernel(x_ref, mod_ref, gain_ref, w_ref, o_ref):
    h = _modulated_norm(x_ref[...], gain_ref[...], mod_ref[0, MOD_SCALE1:MOD_SCALE1 + 1], mod_ref[0, MOD_SHIFT1:MOD_SHIFT1 + 1])
    o_ref[...] = jnp.dot(h.astype(BF16), w_ref[...], preferred_element_type=F32)


def _proj_out_kernel(mix_ref, x_ref, mod_ref, gain_ref, w_ref, x_out_ref, h2_ref):
    y = jnp.dot(mix_ref[...].astype(BF16), w_ref[...], preferred_element_type=F32)
    x_new = x_ref[...] + mod_ref[0, MOD_GATE1:MOD_GATE1 + 1] * y
    x_out_ref[...] = x_new
    h2_ref[...] = _modulated_norm(x_new, gain_ref[...], mod_ref[0, MOD_SCALE2:MOD_SCALE2 + 1],
                                  mod_ref[0, MOD_SHIFT2:MOD_SHIFT2 + 1])


def _proj_in(x, mod, gain, w_bf, tokens_per_cond):
    n, cols = x.shape[0], w_bf.shape[1]
    tm = PROJ_TOKENS
    tok = pl.BlockSpec((tm, D_MODEL), lambda i: (i, 0))
    return pl.pallas_call(
        _proj_in_kernel,
        grid=(n // tm,),
        in_specs=[tok, _mod_spec(mod, tokens_per_cond, tm), pl.BlockSpec((1, D_MODEL), lambda i: (0, 0)),
                  pl.BlockSpec((D_MODEL, cols), lambda i: (0, 0))],
        out_specs=pl.BlockSpec((tm, cols), lambda i: (i, 0)),
        out_shape=jax.ShapeDtypeStruct((n, cols), F32),
        compiler_params=pltpu.CompilerParams(dimension_semantics=("arbitrary",), vmem_limit_bytes=VMEM_LIMIT_BYTES),
        name="proj_in",
    )(x, mod, gain, w_bf)


def _proj_out(mix, x, mod, gain, w_bf, tokens_per_cond):
    n = x.shape[0]
    tm = PROJ_TOKENS
    tok = pl.BlockSpec((tm, D_MODEL), lambda i: (i, 0))
    out = jax.ShapeDtypeStruct((n, D_MODEL), F32)
    return pl.pallas_call(
        _proj_out_kernel,
        grid=(n // tm,),
        in_specs=[tok, tok, _mod_spec(mod, tokens_per_cond, tm), pl.BlockSpec((1, D_MODEL), lambda i: (0, 0)),
                  pl.BlockSpec((D_MODEL, D_MODEL), lambda i: (0, 0))],
        out_specs=[tok, tok],
        out_shape=[out, out],
        compiler_params=pltpu.CompilerParams(dimension_semantics=("arbitrary",), vmem_limit_bytes=VMEM_LIMIT_BYTES),
        name="proj_out",
    )(mix, x, mod, gain, w_bf)


def _trunk_layer(x, cond, lp, ctx, final_gain):
    B, T, _ = x.shape
    mod = (jax.nn.silu(cond) @ lp["w_mod"] + lp["b_mod"]).reshape(-1, 6, D_MODEL)
    x = x.reshape(B * T, D_MODEL)
    proj = _proj_in(x, mod, lp["norm1_g"][None], lp["w_in_bf"], T).reshape(B, T, -1)
    pa, pb, pc, pd, _ = jnp.split(proj, [GQA_PROJ, GQA_PROJ + RW_PROJ, GQA_PROJ + RW_PROJ + NA_PROJ, IN_COLS], axis=-1)
    if ctx is None:
        ctx_a = None
        ctx_c = None
        s_rw0 = jnp.zeros((B, 2, RW_HEADS, HEAD_DIM, HEAD_DIM), F32)
        s_dn0 = jnp.zeros((B, 2, DN_HEADS, HEAD_DIM, HEAD_DIM), F32)
    else:
        ka_c, va_c, kc_c, vc_c, s_rw0, s_dn0 = ctx
        ctx_a = (ka_c, va_c)
        ctx_c = (kc_c, vc_c)
    oa, (ka, va) = _gqa_mixer(pa, lp, ctx_a)
    ob, s_rw = _rwkv_mixer(pb, lp, s_rw0)
    oc, (kc, vc) = _na_mixer(pc, lp, ctx_c)
    od, s_dn = _deltanet_mixer(pd, lp, s_dn0)
    mix = jnp.concatenate([oa, ob, oc, od], axis=-1).reshape(B * T, D_MODEL)
    x, h2 = _proj_out(mix, x, mod, lp["norm2_g"][None], lp["w_out_bf"], T)
    x = _peer_residual(x, h2, mod, lp, T, final_gain)
    return x.reshape(B, T, D_MODEL), (ka, va, kc, vc, s_rw, s_dn)


def kernel(x_prompt, x_sample, c, cache_gqa_k, cache_gqa_v, cache_na_k, cache_na_v, state_rwkv, state_delta, c_ctx, norm1_g, norm2_g, w_mod, b_mod, w_in, w_out, gqa_q_norm, gqa_k_norm, rw_mu, rw_w0, rw_w2, rw_a0, rw_a2, rw_g2, rw_k_k, rw_k_a, rw_r_k, rw_ln_g, rw_ln_b, na_bias, dn_conv, dn_a_log, dn_dt_bias, dn_norm_g, peer_wq, peer_keys, peer_u, peer_v, final_norm_g):
    depth = w_in.shape[0]

    def layer_params(l):
        return {
            "norm1_g": norm1_g[l], "norm2_g": norm2_g[l], "w_mod": w_mod[l], "b_mod": b_mod[l],
            "w_in_bf": jnp.pad(w_in[l].astype(BF16), ((0, 0), (0, IN_COLS_PADDED - IN_COLS))),
            "w_out_bf": w_out[l].astype(BF16), "gqa_q_norm": gqa_q_norm[l], "gqa_k_norm": gqa_k_norm[l],
            "rw_mu": rw_mu[l], "rw_w0": rw_w0[l], "rw_w2": rw_w2[l], "rw_a0": rw_a0[l], "rw_a2": rw_a2[l],
            "rw_g2": rw_g2[l], "rw_k_k": rw_k_k[l], "rw_k_a": rw_k_a[l], "rw_r_k": rw_r_k[l],
            "rw_ln_g": rw_ln_g[l], "rw_ln_b": rw_ln_b[l], "na_bias": na_bias[l],
            "dn_conv": dn_conv[l], "dn_a_log": dn_a_log[l], "dn_dt_bias": dn_dt_bias[l], "dn_norm_g": dn_norm_g[l],
            "peer_wq_bf": peer_wq[l].astype(BF16),
            "peer_keys_bf": peer_keys[l].reshape(2 * PEER_HEADS, PEER_NKEYS, PEER_DQ // 2).astype(BF16),
            "peer_u_bf": peer_u[l].astype(BF16),
            "peer_vt_bf": peer_v[l].astype(BF16).T,
        }

    params = [layer_params(l) for l in range(depth)]

    xp = x_prompt
    cond_ctx = c_ctx[None, :]
    per_layer = []
    closing = lambda l: final_norm_g if l == depth - 1 else None
    for l in range(depth):
        xp, st = _trunk_layer(xp, cond_ctx, params[l], None, closing(l))
        per_layer.append(st)
    y_prompt = xp

    xs = x_sample
    for l in range(depth):
        ctx = (cache_gqa_k[:, l], cache_gqa_v[:, l], cache_na_k[:, l], cache_na_v[:, l],
               state_rwkv[:, l], state_delta[:, l])
        xs, _ = _trunk_layer(xs, c, params[l], ctx, closing(l))
    y_sample = xs

    dt = x_prompt.dtype
    new_gqa_k = jnp.stack([st[0] for st in per_layer], axis=1).astype(dt)
    new_gqa_v = jnp.stack([st[1] for st in per_layer], axis=1).astype(dt)
    new_na_k = jnp.stack([st[2] for st in per_layer], axis=1).astype(dt)
    new_na_v = jnp.stack([st[3] for st in per_layer], axis=1).astype(dt)
    new_state_rwkv = jnp.stack([st[4] for st in per_layer], axis=1).astype(dt)
    new_state_delta = jnp.stack([st[5] for st in per_layer], axis=1).astype(dt)
    return (y_prompt, y_sample, new_gqa_k, new_gqa_v, new_na_k, new_na_v, new_state_rwkv, new_state_delta)
```

```python
import functools
import math

import jax
import jax.numpy as jnp
import numpy as np
from jax import lax
from jax.experimental import pallas as pl
from jax.experimental.pallas import tpu as pltpu

F32 = jnp.float32
BF16 = jnp.bfloat16

D_MODEL = 1024
GRID_W = 64
HEAD_DIM = 64
GROUP_W = D_MODEL // 4
NORM_EPS = 1e-6
NEG_INF = -1e30
GQA_HEADS = GROUP_W // HEAD_DIM
GQA_KV_HEADS = GQA_HEADS // 2
ROPE_THETA = 10000.0
RW_HEADS = GROUP_W // HEAD_DIM
RW_DECAY_RANK = 32
RW_AAA_RANK = 32
RW_GATE_RANK = 64
RW_LN_EPS = 64e-5
NA_HEADS = GROUP_W // HEAD_DIM
NA_ROWS = 8
NA_COLS = 16
DN_HEADS = GROUP_W // HEAD_DIM
DN_CONV = 5
PEER_HEADS = 8
PEER_NKEYS = 128
PEER_EXPERTS = PEER_NKEYS * PEER_NKEYS
PEER_DQ = 256
PEER_TOPK = 16
GQA_PROJ = GROUP_W + 2 * GQA_KV_HEADS * HEAD_DIM
RW_PROJ = 3 * GROUP_W + 2 * RW_DECAY_RANK + 2 * RW_AAA_RANK + RW_GATE_RANK
NA_PROJ = 3 * GROUP_W
DN_PROJ = 3 * GROUP_W + 4 * DN_HEADS + GROUP_W

LANES = 128
VMEM_LIMIT_BYTES = 56 * 1024 * 1024

PEER_SEL_TOKENS = 128
PEER_TOKENS = 256
PEER_ROWS_PER_STEP = 16
PEER_SUB_ROWS = 2
PEER_HALF = 64
PEER_MATMUL_PIECE = 256
PEER_CHUNK = PEER_ROWS_PER_STEP * PEER_NKEYS
PEER_STEPS = PEER_EXPERTS // PEER_CHUNK
INV_SQRT2 = 1.0 / math.sqrt(2.0)

IN_COLS = GQA_PROJ + RW_PROJ + NA_PROJ + DN_PROJ
IN_COLS_PADDED = -(-IN_COLS // LANES) * LANES
MOD_SHIFT1, MOD_SCALE1, MOD_GATE1, MOD_SHIFT2, MOD_SCALE2, MOD_GATE2 = range(6)
PROJ_TOKENS = 512

ATTN_SCALE = HEAD_DIM ** -0.5
ATTN_Q_TILE = 256
NA_ROW_BLOCK = 8

SCAN_CHUNK = 64
SCAN_ROWS = 4


class _Par:
    def __init__(self, xs):
        self.xs = list(xs)

    def __add__(self, o):
        return _lift(jnp.add, self, o)

    __radd__ = __add__

    def __sub__(self, o):
        return _lift(jnp.subtract, self, o)

    def __rsub__(self, o):
        return _lift(lambda x, y: y - x, self, o)

    def __mul__(self, o):
        return _lift(jnp.multiply, self, o)

    __rmul__ = __mul__

    def __neg__(self):
        return _lift(jnp.negative, self)

    def __getitem__(self, idx):
        return _lift(lambda x: x[idx], self)


def _lift(f, *args):
    n = next(len(a.xs) for a in args if isinstance(a, _Par))
    return _Par([f(*[a.xs[i] if isinstance(a, _Par) else a for a in args]) for i in range(n)])


def _exp(x):
    return _lift(jnp.exp, x)


def _stack_rows(x, y):
    return _lift(lambda a, b: jnp.concatenate([a, b], axis=0), x, y)


def _top_k_rows(s, k):
    rows = s.shape[0]
    iota = lax.broadcasted_iota(jnp.int32, s.shape, 0).astype(F32)
    vals = []
    work = s
    for _ in range(k):
        m = jnp.max(work, axis=0, keepdims=True)
        first = jnp.min(jnp.where(work == m, iota, float(rows)), axis=0, keepdims=True)
        work = jnp.where(iota == first, -jnp.inf, work)
        vals.append(m)
    return vals, work


def _peer_select_kernel(h_ref, wq_ref, keys_ref, thr_ref, e1_ref, s2m_ref, e2_ref):
    x = h_ref[...].astype(BF16)
    q = jnp.dot(x, wq_ref[...], preferred_element_type=F32)
    half = PEER_DQ // 2
    for h in range(PEER_HEADS):
        scores, tops, sel = [], [], []
        for p in range(2):
            c0 = (2 * h + p) * half
            qhp = q[:, c0:c0 + half].astype(BF16)
            s = lax.dot_general(keys_ref[2 * h + p], qhp, (((1,), (1,)), ((), ())),
                                preferred_element_type=F32)
            vals, work = _top_k_rows(s, PEER_TOPK)
            scores.append(s)
            tops.append(vals)
            sel.append(work == -jnp.inf)
        m1, m2 = tops
        m2_all = jnp.concatenate(m2, axis=0)
        cand = jnp.concatenate([m1[0] + m2_all] + [m1[i] + m2_all[:8] for i in range(1, 8)]
                               + [jnp.concatenate(m1[8:], axis=0) + m2[0]], axis=0)
        top_s, _ = _top_k_rows(cand, PEER_TOPK)
        mx = top_s[0]
        z = jnp.exp(top_s[0] - mx)
        for r in range(1, PEER_TOPK):
            z = z + jnp.exp(top_s[r] - mx)
        half_inv_z = 0.5 / z
        t3 = top_s[PEER_TOPK - 1]
        s1m = jnp.where(sel[0], scores[0], -jnp.inf)
        thr = jnp.full_like(s1m, jnp.inf)
        for j in range(PEER_TOPK):
            thr = jnp.where(s1m + m2[j] >= t3, m2[j], thr)
        thr_ref[h] = thr
        e1_ref[h] = jnp.where(sel[0], jnp.exp(scores[0] - m1[0]) * half_inv_z, 0.0)
        s2m_ref[h] = jnp.where(sel[1], scores[1], -jnp.inf)
        e2_ref[h] = jnp.exp(scores[1] - m2[0])


def _peer_select(h2, wq_bf, keys_bf):
    n = h2.shape[0]
    tt = PEER_SEL_TOKENS
    big = jax.ShapeDtypeStruct((PEER_HEADS, PEER_NKEYS, n), F32)
    big_spec = pl.BlockSpec((PEER_HEADS, PEER_NKEYS, tt), lambda i: (0, 0, i))
    return pl.pallas_call(
        _peer_select_kernel,
        grid=(n // tt,),
        in_specs=[
            pl.BlockSpec((tt, D_MODEL), lambda i: (i, 0)),
            pl.BlockSpec((D_MODEL, PEER_HEADS * PEER_DQ), lambda i: (0, 0)),
            pl.BlockSpec((2 * PEER_HEADS, PEER_NKEYS, PEER_DQ // 2), lambda i: (0, 0, 0)),
        ],
        out_specs=[big_spec] * 4,
        out_shape=[big] * 4,
        compiler_params=pltpu.CompilerParams(dimension_semantics=("arbitrary",),
                                             vmem_limit_bytes=VMEM_LIMIT_BYTES),
        name="peer_select",
    )(h2, wq_bf, keys_bf)


def _peer_dense_kernel(h_ref, u_ref, vt_ref, thr_ref, e1_ref, s2m_ref, e2_ref, xres_ref, mod_ref, fin_ref,
                       out_ref, ht_ref, acc_ref, p_ref, act_ref, *, final_norm):
    j = pl.program_id(1)
    tt = h_ref.shape[0]

    @pl.when(j == 0)
    def _():
        ht_ref[...] = h_ref[...].T.astype(BF16)
        acc_ref[...] = jnp.zeros_like(acc_ref)

    sub = PEER_SUB_ROWS * PEER_NKEYS
    lane_blocks = tt // LANES
    halves = PEER_NKEYS // PEER_HALF
    n_block = lane_blocks * halves
    kp = PEER_MATMUL_PIECE
    n_piece = D_MODEL // kp
    assert n_block % n_piece == 0
    blocks_per_piece = n_block // n_piece

    def activation_piece(k, i):
        part = jnp.dot(u_ref[k * sub:(k + 1) * sub, i * kp:(i + 1) * kp], ht_ref[i * kp:(i + 1) * kp, :],
                       preferred_element_type=F32)
        if i == 0:
            act_ref[k % 2] = part
        else:
            act_ref[k % 2] += part

    def accumulate_piece(k, i):
        acc_ref[i * kp:(i + 1) * kp, :] += jnp.dot(vt_ref[i * kp:(i + 1) * kp, k * sub:(k + 1) * sub], p_ref[k % 2],
                                                   preferred_element_type=F32)

    def weighted_block(k, i):
        c, hb = divmod(i, halves)
        cs = slice(c * LANES, (c + 1) * LANES)
        bs = slice(hb * PEER_HALF, (hb + 1) * PEER_HALF)
        firsts = [k * PEER_SUB_ROWS + ai for ai in range(PEER_SUB_ROWS)]
        w = [jnp.zeros((PEER_HALF, LANES), F32) for _ in firsts]
        for h in range(PEER_HEADS):
            s2 = s2m_ref[h, bs, cs]
            e2 = e2_ref[h, bs, cs]
            for ai, a in enumerate(firsts):
                w[ai] = w[ai] + jnp.where(s2 >= thr_ref[h, a:a + 1, cs], e2, 0.0) * e1_ref[h, a:a + 1, cs]
        for ai in range(PEER_SUB_ROWS):
            rows = slice(ai * PEER_NKEYS + hb * PEER_HALF, ai * PEER_NKEYS + (hb + 1) * PEER_HALF)
            zz = act_ref[k % 2, rows, cs]
            gelu2 = zz * (1.0 + lax.erf(zz * INV_SQRT2))
            p_ref[k % 2, rows, cs] = (w[ai] * gelu2).astype(BF16)

    n_sub = PEER_ROWS_PER_STEP // PEER_SUB_ROWS
    for i in range(n_piece):
        activation_piece(0, i)
    for k in range(n_sub):
        for blk in range(n_block):
            if blk % blocks_per_piece == 0:
                if k + 1 < n_sub:
                    activation_piece(k + 1, blk // blocks_per_piece)
                if k > 0:
                    accumulate_piece(k - 1, blk // blocks_per_piece)
            weighted_block(k, blk)
    for i in range(n_piece):
        accumulate_piece(n_sub - 1, i)

    @pl.when(j == PEER_STEPS - 1)
    def _():
        y = xres_ref[...] + mod_ref[0, MOD_GATE2:MOD_GATE2 + 1] * acc_ref[...].T
        if final_norm:
            y = y * lax.rsqrt(jnp.mean(y * y, axis=-1, keepdims=True) + NORM_EPS) * fin_ref[...]
        out_ref[...] = y


def _mod_spec(mod, tokens_per_cond, tile):
    tiles_per_cond = tokens_per_cond // tile
    if mod.shape[0] == 1:
        return pl.BlockSpec((1, 6, D_MODEL), lambda i, *_: (0, 0, 0))
    assert tokens_per_cond % tile == 0
    return pl.BlockSpec((1, 6, D_MODEL), lambda i, *_: (i // tiles_per_cond, 0, 0))


def _peer_dense(h2, u_bf, vt_bf, sel, xres, mod, tokens_per_cond, final_gain):
    n = h2.shape[0]
    tt = PEER_TOKENS
    thr, e1, s2m, e2 = sel
    fin = jnp.ones((1, D_MODEL), F32) if final_gain is None else final_gain.astype(F32)[None]
    row_spec = pl.BlockSpec((PEER_HEADS, PEER_ROWS_PER_STEP, tt), lambda i, j: (0, j, i))
    col_spec = pl.BlockSpec((PEER_HEADS, PEER_NKEYS, tt), lambda i, j: (0, 0, i))
    tok_spec = pl.BlockSpec((tt, D_MODEL), lambda i, j: (i, 0))
    return pl.pallas_call(
        functools.partial(_peer_dense_kernel, final_norm=final_gain is not None),
        grid=(n // tt, PEER_STEPS),
        in_specs=[
            tok_spec,
            pl.BlockSpec((PEER_CHUNK, D_MODEL), lambda i, j: (j, 0)),
            pl.BlockSpec((D_MODEL, PEER_CHUNK), lambda i, j: (0, j)),
            row_spec, row_spec, col_spec, col_spec,
            tok_spec,
            _mod_spec(mod, tokens_per_cond, tt),
            pl.BlockSpec((1, D_MODEL), lambda i, j: (0, 0)),
        ],
        out_specs=tok_spec,
        out_shape=jax.ShapeDtypeStruct((n, D_MODEL), F32),
        scratch_shapes=[pltpu.VMEM((D_MODEL, tt), BF16), pltpu.VMEM((D_MODEL, tt), F32),
                        pltpu.VMEM((2, PEER_SUB_ROWS * PEER_NKEYS, tt), BF16),
                        pltpu.VMEM((2, PEER_SUB_ROWS * PEER_NKEYS, tt), F32)],
        compiler_params=pltpu.CompilerParams(dimension_semantics=("arbitrary", "arbitrary"),
                                             vmem_limit_bytes=VMEM_LIMIT_BYTES),
        name="peer_dense",
    )(h2, u_bf, vt_bf, thr, e1, s2m, e2, xres, mod, fin)


def _peer_residual(x, h2, mod, lp, tokens_per_cond, final_gain):
    sel = _peer_select(h2, lp["peer_wq_bf"], lp["peer_keys_bf"])
    return _peer_dense(h2, lp["peer_u_bf"], lp["peer_vt_bf"], sel, x, mod, tokens_per_cond, final_gain)


def _rms_norm(x, g):
    xf = x.astype(F32)
    y = xf * lax.rsqrt(jnp.mean(xf * xf, axis=-1, keepdims=True) + NORM_EPS)
    return (y * g.astype(F32)).astype(x.dtype)


def _l2_norm(x):
    return x * lax.rsqrt(jnp.sum(x * x, axis=-1, keepdims=True) + NORM_EPS)


def _axial_rope_tables(T):
    t = np.arange(T)
    row, col = t // GRID_W, t % GRID_W
    n_freq = HEAD_DIM // 4
    inv = ROPE_THETA ** (-2.0 * np.arange(n_freq) / (HEAD_DIM // 2))
    ang = np.concatenate([row[:, None] * inv[None], col[:, None] * inv[None]], axis=1)
    return jnp.asarray(np.cos(ang), F32), jnp.asarray(np.sin(ang), F32)


def _apply_rope(x, cos, sin):
    xf = x.astype(F32)
    half = HEAD_DIM // 2
    x1, x2 = xf[..., :half], xf[..., half:]
    c, s = cos[None, :, None, :], sin[None, :, None, :]
    return jnp.concatenate([x1 * c - x2 * s, x2 * c + x1 * s], axis=-1).astype(x.dtype)


def _attention_kernel(q_ref, k_ref, v_ref, o_ref, *, group):
    outs = []
    for h in range(q_ref.shape[2] // HEAD_DIM):
        q = q_ref[0, :, h * HEAD_DIM:(h + 1) * HEAD_DIM].astype(BF16)
        s = lax.dot_general(q, k_ref[0, h // group], (((1,), (1,)), ((), ())), preferred_element_type=F32) * ATTN_SCALE
        m = jnp.max(s, axis=-1, keepdims=True)
        p = jnp.exp(s - m)
        l = jnp.sum(p, axis=-1, keepdims=True)
        outs.append(jnp.dot(p.astype(BF16), v_ref[0, h // group], preferred_element_type=F32) / l)
    o_ref[0] = jnp.concatenate(outs, axis=1)


def _attention(q, k, v):
    B, T, width = q.shape
    hkv, L, dh = k.shape[1], k.shape[2], k.shape[3]
    group = width // dh // hkv
    tq = min(T, ATTN_Q_TILE)
    kv_spec = pl.BlockSpec((1, hkv, L, dh), lambda b, i: (b, 0, 0, 0))
    q_spec = pl.BlockSpec((1, tq, width), lambda b, i: (b, i, 0))
    return pl.pallas_call(
        functools.partial(_attention_kernel, group=group),
        grid=(B, T // tq),
        in_specs=[q_spec, kv_spec, kv_spec],
        out_specs=q_spec,
        out_shape=jax.ShapeDtypeStruct((B, T, width), F32),
        compiler_params=pltpu.CompilerParams(dimension_semantics=("arbitrary",) * 2,
                                             vmem_limit_bytes=VMEM_LIMIT_BYTES),
        name="attention",
    )(q, k.astype(BF16), v.astype(BF16))


def _na_bias_table(bias):
    qcols = np.arange(GRID_W)
    kcols = np.arange(GRID_W)
    cstart = np.clip(qcols - NA_COLS // 2, 0, GRID_W - NA_COLS)
    valid = (kcols[None, :] >= cstart[:, None]) & (kcols[None, :] < cstart[:, None] + NA_COLS)
    dc_idx = np.clip(kcols[None, :] - qcols[:, None], -(NA_COLS - 1), NA_COLS - 1) + (NA_COLS - 1)
    onehot = jnp.asarray(dc_idx[:, :, None] == np.arange(2 * NA_COLS - 1), F32)
    rel = jnp.einsum("qkc,hrc->hrqk", onehot, bias.astype(F32), precision=lax.Precision.HIGHEST)
    rel = jnp.where(valid[None, None], rel, NEG_INF)
    return jnp.stack([jnp.concatenate([rel[:, d0 + j] for j in range(NA_ROWS)], axis=-1)
                      for d0 in range(NA_ROWS)], axis=1)


def _na_kernel(q_ref, k_ref, v_ref, kc_ref, vc_ref, bias_ref, o_ref):
    blk = pl.program_id(1)
    n_rows = k_ref.shape[2] // GRID_W
    win = NA_ROWS * GRID_W
    nt = (((1,), (1,)), ((), ()))
    rows = []
    for i in range(NA_ROW_BLOCK):
        r = blk * NA_ROW_BLOCK + i
        start = jnp.clip(r - NA_ROWS // 2, 0, n_rows - NA_ROWS)
        rows.append((pl.multiple_of(start * GRID_W, GRID_W), start - r + (NA_ROWS - 1)))
    outs = []
    for h in range(NA_HEADS):
        q = _Par([q_ref[0, i * GRID_W:(i + 1) * GRID_W, h * HEAD_DIM:(h + 1) * HEAD_DIM].astype(BF16)
                  for i in range(NA_ROW_BLOCK)])
        k = _Par([k_ref[0, h, pl.ds(off, win), :] for off, _ in rows])
        v = _Par([v_ref[0, h, pl.ds(off, win), :] for off, _ in rows])
        bias = _Par([bias_ref[h, d0] for _, d0 in rows])
        s_loc = _lift(lambda q, k, b: lax.dot_general(q, k, nt, preferred_element_type=F32) * ATTN_SCALE + b,
                      q, k, bias)
        s_ctx = _lift(lambda q: lax.dot_general(q, kc_ref[0, h], nt, preferred_element_type=F32) * ATTN_SCALE, q)
        m = _lift(lambda a, b: jnp.maximum(jnp.max(a, axis=-1, keepdims=True), jnp.max(b, axis=-1, keepdims=True)),
                  s_loc, s_ctx)
        p_loc = _exp(s_loc - m)
        p_ctx = _exp(s_ctx - m)
        l = _lift(lambda a, b: jnp.sum(a, axis=-1, keepdims=True) + jnp.sum(b, axis=-1, keepdims=True), p_loc, p_ctx)
        o = _lift(lambda pl_, v, pc, l: (jnp.dot(pl_.astype(BF16), v, preferred_element_type=F32)
                                         + jnp.dot(pc.astype(BF16), vc_ref[0, h], preferred_element_type=F32)) / l,
                  p_loc, v, p_ctx, l)
        outs.append(jnp.concatenate(o.xs, axis=0))
    o_ref[0] = jnp.concatenate(outs, axis=1)


def _neighbourhood_attention(q, k, v, bias, k_ctx, v_ctx):
    B, H, T, dh = k.shape
    P = k_ctx.shape[2]
    rows = T // GRID_W
    assert rows >= NA_ROWS and rows % NA_ROW_BLOCK == 0
    tq = NA_ROW_BLOCK * GRID_W
    full = lambda n: pl.BlockSpec((1, H, n, dh), lambda b, i: (b, 0, 0, 0))
    q_spec = pl.BlockSpec((1, tq, H * dh), lambda b, i: (b, i, 0))
    return pl.pallas_call(
        _na_kernel,
        grid=(B, rows // NA_ROW_BLOCK),
        in_specs=[q_spec, full(T), full(T), full(P), full(P),
                  pl.BlockSpec((H, NA_ROWS, GRID_W, NA_ROWS * GRID_W), lambda b, i: (0, 0, 0, 0))],
        out_specs=q_spec,
        out_shape=jax.ShapeDtypeStruct((B, T, H * dh), F32),
        compiler_params=pltpu.CompilerParams(dimension_semantics=("arbitrary",) * 2,
                                             vmem_limit_bytes=VMEM_LIMIT_BYTES),
        name="na_attention",
    )(q, k.astype(BF16), v.astype(BF16), k_ctx.astype(BF16), v_ctx.astype(BF16), _na_bias_table(bias))


def _gqa_mixer(p, lp, ctx_kv):
    B, T, _ = p.shape
    kvw = GQA_KV_HEADS * HEAD_DIM
    q = _rms_norm(p[..., :GROUP_W].reshape(B, T, GQA_HEADS, HEAD_DIM), lp["gqa_q_norm"])
    k = _rms_norm(p[..., GROUP_W:GROUP_W + kvw].reshape(B, T, GQA_KV_HEADS, HEAD_DIM), lp["gqa_k_norm"])
    v = p[..., GROUP_W + kvw:].reshape(B, T, GQA_KV_HEADS, HEAD_DIM)
    if ctx_kv is not None:
        cos, sin = _axial_rope_tables(T)
        q = _apply_rope(q, cos, sin)
        k = _apply_rope(k, cos, sin)
    qf = q.reshape(B, T, GROUP_W)
    kh = k.transpose(0, 2, 1, 3)
    vh = v.transpose(0, 2, 1, 3)
    if ctx_kv is None:
        o = _attention(qf, kh, vh)
    else:
        k_ctx, v_ctx = ctx_kv
        o = _attention(qf, jnp.concatenate([kh, k_ctx], axis=2), jnp.concatenate([vh, v_ctx], axis=2))
    return o, (kh, vh)


def _na_mixer(p, lp, ctx_kv):
    B, T, _ = p.shape
    q = p[..., :GROUP_W]
    k, v = [p[..., i * GROUP_W:(i + 1) * GROUP_W].reshape(B, T, NA_HEADS, HEAD_DIM).transpose(0, 2, 1, 3)
            for i in (1, 2)]
    if ctx_kv is None:
        o = _attention(q, k, v)
    else:
        o = _neighbourhood_attention(q, k, v, lp["na_bias"], ctx_kv[0], ctx_kv[1])
    return o, (k, v)


def _mm(a, b, dims, two_pass):
    f = lambda x, y: lax.dot_general(x, y, (dims, ((), ())), preferred_element_type=F32)

    def one(a, b):
        a16 = a.astype(BF16)
        hi = b.astype(BF16)
        if not two_pass:
            return f(a16, hi)
        return f(a16, hi) + f(a16, (b - hi.astype(F32)).astype(BF16))

    return _lift(one, a, b)


def _dot(a, b, two_pass=False):
    return _mm(a, b, ((1,), (0,)), two_pass)


def _dot_nt(a, b, two_pass=False):
    return _mm(a, b, ((1,), (1,)), two_pass)


def _dot_tn(a, b):
    return _mm(a, b, ((0,), (0,)), False)


def _chunk_masks(reverse):
    row = lax.broadcasted_iota(jnp.int32, (SCAN_CHUNK, SCAN_CHUNK), 0)
    col = lax.broadcasted_iota(jnp.int32, (SCAN_CHUNK, SCAN_CHUNK), 1)
    earlier = (row < col) if reverse else (row > col)
    eye = row == col
    blk16 = (row >> 4) == (col >> 4)
    blk32 = (row >> 5) == (col >> 5)
    return dict(
        strict=earlier.astype(F32), incl=(earlier | eye).astype(F32), eye=eye.astype(F32),
        d16=blk16.astype(F32), off32=(blk32 & jnp.logical_not(blk16)).astype(F32),
        off64=jnp.logical_not(blk32).astype(F32))


def _scan_chains(heads):
    fwd, bwd = _chunk_masks(False), _chunk_masks(True)
    chains = [(b, d, h) for b in range(SCAN_ROWS) for d in range(2) for h in range(heads)]
    masks = {key: _Par([(bwd if d else fwd)[key] for _, d, _ in chains]) for key in fwd}
    return chains, [d == 1 for _, d, _ in chains], masks


def _last_row(cum, revs):
    return _Par([c[0:1] if rev else c[SCAN_CHUNK - 1:SCAN_CHUNK] for c, rev in zip(cum.xs, revs)])


def _unit_triangular_inverse(n, m):
    eye = m["eye"]
    nd = n * m["d16"]
    n2 = _dot(nd, nd)
    x = eye + nd
    x = x + _dot(x, n2)
    n4 = _dot(n2, n2)
    x = x + _dot(x, n4)
    n8 = _dot(n4, n4)
    x = x + _dot(x, n8)
    x = x + _dot(x, _dot(n * m["off32"], x))
    x = x + _dot(x, _dot(n * m["off64"], x))
    return x


def _rwkv_chunk(r, lw, k, v, a, b, h0, m, revs):
    cum = _dot(m["incl"], lw, two_pass=True)
    last = _last_row(cum, revs)
    g_inv = _exp(-cum)
    at = a * _exp(cum - lw)
    rt = r * _exp(cum)
    bt = b * g_inv
    kt = k * g_inv
    to_end = _exp(last - cum)
    bh = b * to_end
    kh = k * to_end
    ar = _stack_rows(at, rt)
    gb = _dot_nt(ar, bt)
    gk = _dot_nt(ar, kt)
    n_ab = gb[:SCAN_CHUNK] * m["strict"]
    l_ak = gk[:SCAN_CHUNK] * m["strict"]
    m_rb = gb[SCAN_CHUNK:] * m["incl"]
    m_rk = gk[SCAN_CHUNK:] * m["incl"]
    tinv = _unit_triangular_inverse(n_ab, m)
    p1 = _dot(tinv, at)
    u0 = _dot(tinv, _dot(l_ak, v))
    p2 = rt + _dot(m_rb, p1)
    o0 = _dot(m_rb, u0) + _dot(m_rk, v)
    a_c = m["eye"] * _exp(last) + _dot_tn(bh, p1)
    g_c = _dot_tn(bh, u0) + _dot_tn(kh, v)
    return _dot(p2, h0) + o0, _dot(a_c, h0) + g_c


def _rwkv_scan_kernel(*refs):
    ins, h0_ref = refs[:12], refs[12]
    o_refs, hout_ref, h_ref = refs[13:15], refs[15], refs[16]
    c = pl.program_id(1)

    @pl.when(c == 0)
    def _():
        h_ref[...] = h0_ref[...]

    chains, revs, m = _scan_chains(RW_HEADS)
    seqs = [_Par([_head_tile(ins[6 * d + i], b, h) for b, d, h in chains]) for i in range(6)]
    o, h_new = _rwkv_chunk(*seqs, _Par([h_ref[b, d, h] for b, d, h in chains]), m, revs)
    _store_heads(o_refs, o, chains)
    for i, (b, d, h) in enumerate(chains):
        h_ref[b, d, h] = h_new.xs[i]

    @pl.when(c == pl.num_programs(1) - 1)
    def _():
        hout_ref[...] = h_ref[...]


def _head_tile(ref, b, h):
    return ref[b, :, h * HEAD_DIM:(h + 1) * HEAD_DIM]


def _store_heads(o_refs, o, chains):
    for d in range(2):
        for b in range(SCAN_ROWS):
            tiles = [o.xs[i] for i, (bi, di, _) in enumerate(chains) if (bi, di) == (b, d)]
            o_refs[d][b] = jnp.concatenate(tiles, axis=1)


def _scan_specs(B, H, T, dh):
    n = T // SCAN_CHUNK
    fwd = pl.BlockSpec((SCAN_ROWS, SCAN_CHUNK, H * dh), lambda b, c: (b, c, 0))
    bwd = pl.BlockSpec((SCAN_ROWS, SCAN_CHUNK, H * dh), lambda b, c: (b, n - 1 - c, 0))
    state = pl.BlockSpec((SCAN_ROWS, 2, H, dh, dh), lambda b, c: (b, 0, 0, 0, 0))
    return (B // SCAN_ROWS, n), fwd, bwd, state


def _rwkv_scan_pallas(seqs, h0):
    B, T, _ = seqs[0][0].shape
    H, dh = RW_HEADS, HEAD_DIM
    grid, fwd, bwd, state_spec = _scan_specs(B, H, T, dh)
    seq = jax.ShapeDtypeStruct((B, T, H * dh), F32)
    return pl.pallas_call(
        _rwkv_scan_kernel,
        grid=grid,
        in_specs=[fwd] * 6 + [bwd] * 6 + [state_spec],
        out_specs=[fwd, bwd, state_spec],
        out_shape=[seq, seq, jax.ShapeDtypeStruct((B, 2, H, dh, dh), F32)],
        scratch_shapes=[pltpu.VMEM((SCAN_ROWS, 2, H, dh, dh), F32)],
        compiler_params=pltpu.CompilerParams(dimension_semantics=("arbitrary", "arbitrary"),
                                             vmem_limit_bytes=VMEM_LIMIT_BYTES),
        name="rwkv_scan",
    )(*seqs[0], *seqs[1], h0)


def _rwkv_mixer(p, lp, S0):
    B, T, _ = p.shape
    W, H = GROUP_W, RW_HEADS
    o0 = 3 * W
    o1 = o0 + 2 * RW_DECAY_RANK
    o2 = o1 + 2 * RW_AAA_RANK
    rkv = p[..., :o0]
    wd = p[..., o0:o1].reshape(B, T, 2, RW_DECAY_RANK)
    ad = p[..., o1:o2].reshape(B, T, 2, RW_AAA_RANK)
    gate = jax.nn.sigmoid(p[..., o2:]) @ lp["rw_g2"]
    heads = lambda a: a.astype(F32).reshape(B, T, H, HEAD_DIM)
    flat = lambda a: a.reshape(B, T, W)
    seqs, bonuses = [], []
    for d in range(2):
        base = jnp.concatenate([rkv, wd[:, :, d], ad[:, :, d]], axis=-1)
        if d == 0:
            shifted = jnp.pad(base[:, :-1], ((0, 0), (1, 0), (0, 0)))
        else:
            shifted = jnp.pad(base[:, 1:], ((0, 0), (0, 1), (0, 0)))
        xd = base + (shifted - base) * lp["rw_mu"][d]
        r, k, v = xd[..., :W], xd[..., W:2 * W], xd[..., 2 * W:3 * W]
        wl, al = xd[..., 3 * W:3 * W + RW_DECAY_RANK], xd[..., 3 * W + RW_DECAY_RANK:]
        w = -jax.nn.softplus(-(lp["rw_w0"][d] + jnp.tanh(wl) @ lp["rw_w2"][d])) - 0.5
        log_decay = -jnp.exp(w.astype(F32))
        a = jax.nn.sigmoid((lp["rw_a0"][d] + al @ lp["rw_a2"][d]).astype(F32))
        kf = k.astype(F32)
        kk = _l2_norm(heads(kf * lp["rw_k_k"].astype(F32)))
        k_eff = heads(kf * (1.0 + (a - 1.0) * lp["rw_k_a"].astype(F32)))
        rh, vh, ah = heads(r), heads(v), heads(a)
        seqs.append([r.astype(F32), log_decay, flat(k_eff), v.astype(F32), flat(-kk), flat(kk * ah)])
        bonuses.append(jnp.sum(rh * k_eff * lp["rw_r_k"].astype(F32), axis=-1, keepdims=True) * vh)
    o_f, o_b, h_fin = _rwkv_scan_pallas(seqs, jnp.swapaxes(S0.astype(F32), -1, -2))
    o = heads(o_f + o_b)
    finals = jnp.swapaxes(h_fin, -1, -2)
    mu = jnp.mean(o, axis=-1, keepdims=True)
    var = jnp.mean(jnp.square(o - mu), axis=-1, keepdims=True)
    y = ((o - mu) * lax.rsqrt(var + RW_LN_EPS)).reshape(B, T, W) * lp["rw_ln_g"].astype(F32) + lp["rw_ln_b"].astype(F32)
    y = y + (bonuses[0] + bonuses[1]).reshape(B, T, W)
    return y.astype(p.dtype) * gate, finals


def _delta_chunk(q, k, v, gb, betab, s0, m, revs):
    gc = _dot(m["incl"], gb, two_pass=True)
    last = _last_row(gc, revs)
    diff = gc - _dot_nt(m["eye"], gc, two_pass=True)
    dmat = _lift(lambda keep, d: jnp.where(keep > 0.0, jnp.exp(jnp.where(keep > 0.0, d, 0.0)), 0.0), m["incl"], diff)
    kb = k * betab
    gram = _dot_nt(_stack_rows(kb, q), k)
    a_mat = gram[:SCAN_CHUNK] * dmat * m["strict"]
    qk = gram[SCAN_CHUNK:] * dmat
    tm = _unit_triangular_inverse(-a_mat, m)
    u = _dot(tm, v * betab)
    w = _dot(tm, kb * _exp(gc))
    v_new = u - _dot(w, s0)
    o = _dot(q * _exp(gc), s0) + _dot(qk, v_new)
    return o, s0 * _exp(last) + _dot_tn(k * _exp(last - gc), v_new)


def _delta_scan_kernel(*refs):
    ins, s0_ref = refs[:10], refs[10]
    o_refs, sout_ref, s_ref = refs[11:13], refs[13], refs[14]
    c = pl.program_id(1)

    @pl.when(c == 0)
    def _():
        s_ref[...] = s0_ref[...]

    chains, revs, m = _scan_chains(DN_HEADS)
    seqs = [_Par([_head_tile(ins[5 * d + i], b, h) for b, d, h in chains]) for i in range(5)]
    o, s_new = _delta_chunk(*seqs, _Par([s_ref[b, d, h] for b, d, h in chains]), m, revs)
    _store_heads(o_refs, o, chains)
    for i, (b, d, h) in enumerate(chains):
        s_ref[b, d, h] = s_new.xs[i]

    @pl.when(c == pl.num_programs(1) - 1)
    def _():
        sout_ref[...] = s_ref[...]


def _delta_scan_pallas(q, k, v, g, beta, s0):
    assert SCAN_CHUNK == HEAD_DIM
    B, T, _ = q.shape
    H, dh = DN_HEADS, HEAD_DIM
    grid, fwd, bwd, state_spec = _scan_specs(B, H, T, dh)
    seq = jax.ShapeDtypeStruct((B, T, H * dh), F32)
    return pl.pallas_call(
        _delta_scan_kernel,
        grid=grid,
        in_specs=[fwd] * 5 + [bwd] * 5 + [state_spec],
        out_specs=[fwd, bwd, state_spec],
        out_shape=[seq, seq, jax.ShapeDtypeStruct((B, 2, H, dh, dh), F32)],
        scratch_shapes=[pltpu.VMEM((SCAN_ROWS, 2, H, dh, dh), F32)],
        compiler_params=pltpu.CompilerParams(dimension_semantics=("arbitrary", "arbitrary"),
                                             vmem_limit_bytes=VMEM_LIMIT_BYTES),
        name="delta_scan",
    )(q, k, v, g[0], beta[0], q, k, v, g[1], beta[1], s0)


def _deltanet_mixer(p, lp, S0):
    B, T, _ = p.shape
    W, H = GROUP_W, DN_HEADS
    xin = jnp.pad(p[..., :3 * W], ((0, 0), (DN_CONV // 2, DN_CONV // 2), (0, 0)))
    qkv = sum(xin[:, j:j + T] * lp["dn_conv"][j] for j in range(DN_CONV))
    qkv = jax.nn.silu(qkv)
    heads = lambda a: a.astype(F32).reshape(B, T, H, HEAD_DIM)
    flat = lambda a: a.reshape(B, T, W)
    q = _l2_norm(heads(qkv[..., :W])) * (HEAD_DIM ** -0.5)
    k = _l2_norm(heads(qkv[..., W:2 * W]))
    v = heads(qkv[..., 2 * W:])
    beta = jax.nn.sigmoid(p[..., 3 * W:3 * W + 2 * H].astype(F32)).reshape(B, T, 2, H)
    alpha = p[..., 3 * W + 2 * H:3 * W + 4 * H].astype(F32).reshape(B, T, 2, H)
    g = -jnp.exp(lp["dn_a_log"].astype(F32)) * jax.nn.softplus(alpha + lp["dn_dt_bias"].astype(F32))
    z = heads(p[..., 3 * W + 4 * H:])
    lanes = lambda a: [jnp.repeat(a[:, :, d], HEAD_DIM, axis=-1) for d in range(2)]
    o_f, o_b, s_fin = _delta_scan_pallas(flat(q), flat(k), flat(v), lanes(g), lanes(beta), S0.astype(F32))
    o = _rms_norm(heads(o_f + o_b), lp["dn_norm_g"]) * jax.nn.silu(z)
    return o.reshape(B, T, W).astype(p.dtype), s_fin


def _modulated_norm(x, gain, scale, shift):
    y = x * lax.rsqrt(jnp.mean(x * x, axis=-1, keepdims=True) + NORM_EPS)
    return (y * gain) * (1.0 + scale) + shift


def _proj_in_kernel(x_ref, mod_ref, gain_ref, w_ref, o_ref):
    h = _modulated_norm(x_ref[...], gain_ref[...], mod_ref[0, MOD_SCALE1:MOD_SCALE1 + 1], mod_ref[0, MOD_SHIFT1:MOD_SHIFT1 + 1])
    o_ref[...] = jnp.dot(h.astype(BF16), w_ref[...], preferred_element_type=F32)


def _proj_out_kernel(mix_ref, x_ref, mod_ref, gain_ref, w_ref, x_out_ref, h2_ref):
    y = jnp.dot(mix_ref[...].astype(BF16), w_ref[...], preferred_element_type=F32)
    x_new = x_ref[...] + mod_ref[0, MOD_GATE1:MOD_GATE1 + 1] * y
    x_out_ref[...] = x_new
    h2_ref[...] = _modulated_norm(x_new, gain_ref[...], mod_ref[0, MOD_SCALE2:MOD_SCALE2 + 1],
                                  mod_ref[0, MOD_SHIFT2:MOD_SHIFT2 + 1])


def _proj_in(x, mod, gain, w_bf, tokens_per_cond):
    n, cols = x.shape[0], w_bf.shape[1]
    tm = PROJ_TOKENS
    tok = pl.BlockSpec((tm, D_MODEL), lambda i: (i, 0))
    return pl.pallas_call(
        _proj_in_kernel,
        grid=(n // tm,),
        in_specs=[tok, _mod_spec(mod, tokens_per_cond, tm), pl.BlockSpec((1, D_MODEL), lambda i: (0, 0)),
                  pl.BlockSpec((D_MODEL, cols), lambda i: (0, 0))],
        out_specs=pl.BlockSpec((tm, cols), lambda i: (i, 0)),
        out_shape=jax.ShapeDtypeStruct((n, cols), F32),
        compiler_params=pltpu.CompilerParams(dimension_semantics=("arbitrary",), vmem_limit_bytes=VMEM_LIMIT_BYTES),
        name="proj_in",
    )(x, mod, gain, w_bf)


def _proj_out(mix, x, mod, gain, w_bf, tokens_per_cond):
    n = x.shape[0]
    tm = PROJ_TOKENS
    tok = pl.BlockSpec((tm, D_MODEL), lambda i: (i, 0))
    out = jax.ShapeDtypeStruct((n, D_MODEL), F32)
    return pl.pallas_call(
        _proj_out_kernel,
        grid=(n // tm,),
        in_specs=[tok, tok, _mod_spec(mod, tokens_per_cond, tm), pl.BlockSpec((1, D_MODEL), lambda i: (0, 0)),
                  pl.BlockSpec((D_MODEL, D_MODEL), lambda i: (0, 0))],
        out_specs=[tok, tok],
        out_shape=[out, out],
        compiler_params=pltpu.CompilerParams(dimension_semantics=("arbitrary",), vmem_limit_bytes=VMEM_LIMIT_BYTES),
        name="proj_out",
    )(mix, x, mod, gain, w_bf)


def _trunk_layer(x, cond, lp, ctx, final_gain):
    B, T, _ = x.shape
    mod = (jax.nn.silu(cond) @ lp["w_mod"] + lp["b_mod"]).reshape(-1, 6, D_MODEL)
    x = x.reshape(B * T, D_MODEL)
    proj = _proj_in(x, mod, lp["norm1_g"][None], lp["w_in_bf"], T).reshape(B, T, -1)
    pa, pb, pc, pd, _ = jnp.split(proj, [GQA_PROJ, GQA_PROJ + RW_PROJ, GQA_PROJ + RW_PROJ + NA_PROJ, IN_COLS], axis=-1)
    if ctx is None:
        ctx_a = None
        ctx_c = None
        s_rw0 = jnp.zeros((B, 2, RW_HEADS, HEAD_DIM, HEAD_DIM), F32)
        s_dn0 = jnp.zeros((B, 2, DN_HEADS, HEAD_DIM, HEAD_DIM), F32)
    else:
        ka_c, va_c, kc_c, vc_c, s_rw0, s_dn0 = ctx
        ctx_a = (ka_c, va_c)
        ctx_c = (kc_c, vc_c)
    oa, (ka, va) = _gqa_mixer(pa, lp, ctx_a)
    ob, s_rw = _rwkv_mixer(pb, lp, s_rw0)
    oc, (kc, vc) = _na_mixer(pc, lp, ctx_c)
    od, s_dn = _deltanet_mixer(pd, lp, s_dn0)
    mix = jnp.concatenate([oa, ob, oc, od], axis=-1).reshape(B * T, D_MODEL)
    x, h2 = _proj_out(mix, x, mod, lp["norm2_g"][None], lp["w_out_bf"], T)
    x = _peer_residual(x, h2, mod, lp, T, final_gain)
    return x.reshape(B, T, D_MODEL), (ka, va, kc, vc, s_rw, s_dn)


def kernel(x_prompt, x_sample, c, cache_gqa_k, cache_gqa_v, cache_na_k, cache_na_v, state_rwkv, state_delta, c_ctx, norm1_g, norm2_g, w_mod, b_mod, w_in, w_out, gqa_q_norm, gqa_k_norm, rw_mu, rw_w0, rw_w2, rw_a0, rw_a2, rw_g2, rw_k_k, rw_k_a, rw_r_k, rw_ln_g, rw_ln_b, na_bias, dn_conv, dn_a_log, dn_dt_bias, dn_norm_g, peer_wq, peer_keys, peer_u, peer_v, final_norm_g):
    depth = w_in.shape[0]

    def layer_params(l):
        return {
            "norm1_g": norm1_g[l], "norm2_g": norm2_g[l], "w_mod": w_mod[l], "b_mod": b_mod[l],
            "w_in_bf": jnp.pad(w_in[l].astype(BF16), ((0, 0), (0, IN_COLS_PADDED - IN_COLS))),
            "w_out_bf": w_out[l].astype(BF16), "gqa_q_norm": gqa_q_norm[l], "gqa_k_norm": gqa_k_norm[l],
            "rw_mu": rw_mu[l], "rw_w0": rw_w0[l], "rw_w2": rw_w2[l], "rw_a0": rw_a0[l], "rw_a2": rw_a2[l],
            "rw_g2": rw_g2[l], "rw_k_k": rw_k_k[l], "rw_k_a": rw_k_a[l], "rw_r_k": rw_r_k[l],
            "rw_ln_g": rw_ln_g[l], "rw_ln_b": rw_ln_b[l], "na_bias": na_bias[l],
            "dn_conv": dn_conv[l], "dn_a_log": dn_a_log[l], "dn_dt_bias": dn_dt_bias[l], "dn_norm_g": dn_norm_g[l],
            "peer_wq_bf": peer_wq[l].astype(BF16),
            "peer_keys_bf": peer_keys[l].reshape(2 * PEER_HEADS, PEER_NKEYS, PEER_DQ // 2).astype(BF16),
            "peer_u_bf": peer_u[l].astype(BF16),
            "peer_vt_bf": peer_v[l].astype(BF16).T,
        }

    params = [layer_params(l) for l in range(depth)]

    xp = x_prompt
    cond_ctx = c_ctx[None, :]
    per_layer = []
    closing = lambda l: final_norm_g if l == depth - 1 else None
    for l in range(depth):
        xp, st = _trunk_layer(xp, cond_ctx, params[l], None, closing(l))
        per_layer.append(st)
    y_prompt = xp

    xs = x_sample
    for l in range(depth):
        ctx = (cache_gqa_k[:, l], cache_gqa_v[:, l], cache_na_k[:, l], cache_na_v[:, l],
               state_rwkv[:, l], state_delta[:, l])
        xs, _ = _trunk_layer(xs, c, params[l], ctx, closing(l))
    y_sample = xs

    dt = x_prompt.dtype
    new_gqa_k = jnp.stack([st[0] for st in per_layer], axis=1).astype(dt)
    new_gqa_v = jnp.stack([st[1] for st in per_layer], axis=1).astype(dt)
    new_na_k = jnp.stack([st[2] for st in per_layer], axis=1).astype(dt)
    new_na_v = jnp.stack([st[3] for st in per_layer], axis=1).astype(dt)
    new_state_rwkv = jnp.stack([st[4] for st in per_layer], axis=1).astype(dt)
    new_state_delta = jnp.stack([st[5] for st in per_layer], axis=1).astype(dt)
    return (y_prompt, y_sample, new_gqa_k, new_gqa_v, new_na_k, new_na_v, new_state_rwkv, new_state_delta)
```

```python
import functools
import math

import jax
import jax.numpy as jnp
import numpy as np
from jax import lax
from jax.experimental import pallas as pl
from jax.experimental.pallas import tpu as pltpu

F32 = jnp.float32
BF16 = jnp.bfloat16

D_MODEL = 1024
GRID_W = 64
HEAD_DIM = 64
GROUP_W = D_MODEL // 4
NORM_EPS = 1e-6
NEG_INF = -1e30
GQA_HEADS = GROUP_W // HEAD_DIM
GQA_KV_HEADS = GQA_HEADS // 2
ROPE_THETA = 10000.0
RW_HEADS = GROUP_W // HEAD_DIM
RW_DECAY_RANK = 32
RW_AAA_RANK = 32
RW_GATE_RANK = 64
RW_LN_EPS = 64e-5
NA_HEADS = GROUP_W // HEAD_DIM
NA_ROWS = 8
NA_COLS = 16
DN_HEADS = GROUP_W // HEAD_DIM
DN_CONV = 5
PEER_HEADS = 8
PEER_NKEYS = 128
PEER_EXPERTS = PEER_NKEYS * PEER_NKEYS
PEER_DQ = 256
PEER_TOPK = 16
GQA_PROJ = GROUP_W + 2 * GQA_KV_HEADS * HEAD_DIM
RW_PROJ = 3 * GROUP_W + 2 * RW_DECAY_RANK + 2 * RW_AAA_RANK + RW_GATE_RANK
NA_PROJ = 3 * GROUP_W
DN_PROJ = 3 * GROUP_W + 4 * DN_HEADS + GROUP_W

LANES = 128
VMEM_LIMIT_BYTES = 56 * 1024 * 1024

PEER_SEL_TOKENS = 128
PEER_TOKENS = 256
PEER_ROWS_PER_STEP = 16
PEER_SUB_ROWS = 2
PEER_HALF = 64
PEER_MATMUL_PIECE = 256
PEER_CHUNK = PEER_ROWS_PER_STEP * PEER_NKEYS
PEER_STEPS = PEER_EXPERTS // PEER_CHUNK
INV_SQRT2 = 1.0 / math.sqrt(2.0)

IN_COLS = GQA_PROJ + RW_PROJ + NA_PROJ + DN_PROJ
IN_COLS_PADDED = -(-IN_COLS // LANES) * LANES
MOD_SHIFT1, MOD_SCALE1, MOD_GATE1, MOD_SHIFT2, MOD_SCALE2, MOD_GATE2 = range(6)
PROJ_TOKENS = 512

ATTN_SCALE = HEAD_DIM ** -0.5
ATTN_Q_TILE = 256
NA_ROW_BLOCK = 8

SCAN_CHUNK = 64
SCAN_ROWS = 4


class _Par:
    def __init__(self, xs):
        self.xs = list(xs)

    def __add__(self, o):
        return _lift(jnp.add, self, o)

    __radd__ = __add__

    def __sub__(self, o):
        return _lift(jnp.subtract, self, o)

    def __rsub__(self, o):
        return _lift(lambda x, y: y - x, self, o)

    def __mul__(self, o):
        return _lift(jnp.multiply, self, o)

    __rmul__ = __mul__

    def __neg__(self):
        return _lift(jnp.negative, self)

    def __getitem__(self, idx):
        return _lift(lambda x: x[idx], self)


def _lift(f, *args):
    n = next(len(a.xs) for a in args if isinstance(a, _Par))
    return _Par([f(*[a.xs[i] if isinstance(a, _Par) else a for a in args]) for i in range(n)])


def _exp(x):
    return _lift(jnp.exp, x)


def _stack_rows(x, y):
    return _lift(lambda a, b: jnp.concatenate([a, b], axis=0), x, y)


def _top_k_rows(s, k, exact):
    rows = s.shape[0]
    iota = lax.broadcasted_iota(jnp.int32, s.shape, 0).astype(F32)
    vals = []
    work = s
    for _ in range(k):
        m = jnp.max(work, axis=0, keepdims=True)
        if exact:
            first = jnp.min(jnp.where(work == m, iota, float(rows)), axis=0, keepdims=True)
            work = jnp.where(iota == first, -jnp.inf, work)
        else:
            work = jnp.where(work == m, -jnp.inf, work)
        vals.append(m)
    if exact:
        return vals, work, jnp.zeros_like(vals[0])
    removed = jnp.sum(jnp.where(work == -jnp.inf, 1.0, 0.0), axis=0, keepdims=True)
    return vals, work, jnp.where(removed == float(k), 0.0, 1.0)


def _peer_select_body(h_ref, wq_ref, keys_ref, thr_ref, e1_ref, s2m_ref, e2_ref, exact):
    x = h_ref[...].astype(BF16)
    q = jnp.dot(x, wq_ref[...], preferred_element_type=F32)
    half = PEER_DQ // 2
    suspect = None
    for h in range(PEER_HEADS):
        scores, tops, sel = [], [], []
        for p in range(2):
            c0 = (2 * h + p) * half
            qhp = q[:, c0:c0 + half].astype(BF16)
            s = lax.dot_general(keys_ref[2 * h + p], qhp, (((1,), (1,)), ((), ())),
                                preferred_element_type=F32)
            vals, work, flag = _top_k_rows(s, PEER_TOPK, exact)
            suspect = flag if suspect is None else suspect + flag
            scores.append(s)
            tops.append(vals)
            sel.append(work == -jnp.inf)
        m1, m2 = tops
        m2_all = jnp.concatenate(m2, axis=0)
        cand = jnp.concatenate([m1[0] + m2_all] + [m1[i] + m2_all[:8] for i in range(1, 8)]
                               + [jnp.concatenate(m1[8:], axis=0) + m2[0]], axis=0)
        top_s, _, flag = _top_k_rows(cand, PEER_TOPK, exact)
        suspect = suspect + flag
        mx = top_s[0]
        z = jnp.exp(top_s[0] - mx)
        for r in range(1, PEER_TOPK):
            z = z + jnp.exp(top_s[r] - mx)
        half_inv_z = 0.5 / z
        t3 = top_s[PEER_TOPK - 1]
        s1m = jnp.where(sel[0], scores[0], -jnp.inf)
        thr = jnp.full_like(s1m, jnp.inf)
        for j in range(PEER_TOPK):
            thr = jnp.where(s1m + m2[j] >= t3, m2[j], thr)
        thr_ref[h] = thr
        e1_ref[h] = jnp.where(sel[0], jnp.exp(scores[0] - m1[0]) * half_inv_z, 0.0)
        s2m_ref[h] = jnp.where(sel[1], scores[1], -jnp.inf)
        e2_ref[h] = jnp.exp(scores[1] - m2[0])
    return suspect


def _peer_select_kernel(*refs):
    suspect = _peer_select_body(*refs, exact=False)

    @pl.when(jnp.max(suspect) > 0.0)
    def _():
        _peer_select_body(*refs, exact=True)


def _peer_select(h2, wq_bf, keys_bf):
    n = h2.shape[0]
    tt = PEER_SEL_TOKENS
    big = jax.ShapeDtypeStruct((PEER_HEADS, PEER_NKEYS, n), F32)
    big_spec = pl.BlockSpec((PEER_HEADS, PEER_NKEYS, tt), lambda i: (0, 0, i))
    return pl.pallas_call(
        _peer_select_kernel,
        grid=(n // tt,),
        in_specs=[
            pl.BlockSpec((tt, D_MODEL), lambda i: (i, 0)),
            pl.BlockSpec((D_MODEL, PEER_HEADS * PEER_DQ), lambda i: (0, 0)),
            pl.BlockSpec((2 * PEER_HEADS, PEER_NKEYS, PEER_DQ // 2), lambda i: (0, 0, 0)),
        ],
        out_specs=[big_spec] * 4,
        out_shape=[big] * 4,
        compiler_params=pltpu.CompilerParams(dimension_semantics=("arbitrary",),
                                             vmem_limit_bytes=VMEM_LIMIT_BYTES),
        name="peer_select",
    )(h2, wq_bf, keys_bf)


def _peer_dense_kernel(h_ref, u_ref, vt_ref, thr_ref, e1_ref, s2m_ref, e2_ref, xres_ref, mod_ref, fin_ref,
                       out_ref, ht_ref, acc_ref, p_ref, act_ref, *, final_norm):
    j = pl.program_id(1)
    tt = h_ref.shape[0]

    @pl.when(j == 0)
    def _():
        ht_ref[...] = h_ref[...].T.astype(BF16)
        acc_ref[...] = jnp.zeros_like(acc_ref)

    sub = PEER_SUB_ROWS * PEER_NKEYS
    lane_blocks = tt // LANES
    halves = PEER_NKEYS // PEER_HALF
    n_block = lane_blocks * halves
    kp = PEER_MATMUL_PIECE
    n_piece = D_MODEL // kp
    assert n_block % n_piece == 0
    blocks_per_piece = n_block // n_piece

    def activation_piece(k, i):
        part = jnp.dot(u_ref[k * sub:(k + 1) * sub, i * kp:(i + 1) * kp], ht_ref[i * kp:(i + 1) * kp, :],
                       preferred_element_type=F32)
        if i == 0:
            act_ref[k % 2] = part
        else:
            act_ref[k % 2] += part

    def accumulate_piece(k, i):
        acc_ref[i * kp:(i + 1) * kp, :] += jnp.dot(vt_ref[i * kp:(i + 1) * kp, k * sub:(k + 1) * sub], p_ref[k % 2],
                                                   preferred_element_type=F32)

    def weighted_block(k, i):
        c, hb = divmod(i, halves)
        cs = slice(c * LANES, (c + 1) * LANES)
        bs = slice(hb * PEER_HALF, (hb + 1) * PEER_HALF)
        firsts = [k * PEER_SUB_ROWS + ai for ai in range(PEER_SUB_ROWS)]
        w = [jnp.zeros((PEER_HALF, LANES), F32) for _ in firsts]
        for h in range(PEER_HEADS):
            s2 = s2m_ref[h, bs, cs]
            e2 = e2_ref[h, bs, cs]
            for ai, a in enumerate(firsts):
                w[ai] = w[ai] + jnp.where(s2 >= thr_ref[h, a:a + 1, cs], e2, 0.0) * e1_ref[h, a:a + 1, cs]
        for ai in range(PEER_SUB_ROWS):
            rows = slice(ai * PEER_NKEYS + hb * PEER_HALF, ai * PEER_NKEYS + (hb + 1) * PEER_HALF)
            zz = act_ref[k % 2, rows, cs]
            gelu2 = zz * (1.0 + lax.erf(zz * INV_SQRT2))
            p_ref[k % 2, rows, cs] = (w[ai] * gelu2).astype(BF16)

    n_sub = PEER_ROWS_PER_STEP // PEER_SUB_ROWS
    for i in range(n_piece):
        activation_piece(0, i)
    for k in range(n_sub):
        for blk in range(n_block):
            if blk % blocks_per_piece == 0:
                if k + 1 < n_sub:
                    activation_piece(k + 1, blk // blocks_per_piece)
                if k > 0:
                    accumulate_piece(k - 1, blk // blocks_per_piece)
            weighted_block(k, blk)
    for i in range(n_piece):
        accumulate_piece(n_sub - 1, i)

    @pl.when(j == PEER_STEPS - 1)
    def _():
        y = xres_ref[...] + mod_ref[0, MOD_GATE2:MOD_GATE2 + 1] * acc_ref[...].T
        if final_norm:
            y = y * lax.rsqrt(jnp.mean(y * y, axis=-1, keepdims=True) + NORM_EPS) * fin_ref[...]
        out_ref[...] = y


def _mod_spec(mod, tokens_per_cond, tile):
    tiles_per_cond = tokens_per_cond // tile
    if mod.shape[0] == 1:
        return pl.BlockSpec((1, 6, D_MODEL), lambda i, *_: (0, 0, 0))
    assert tokens_per_cond % tile == 0
    return pl.BlockSpec((1, 6, D_MODEL), lambda i, *_: (i // tiles_per_cond, 0, 0))


def _peer_dense(h2, u_bf, vt_bf, sel, xres, mod, tokens_per_cond, final_gain):
    n = h2.shape[0]
    tt = PEER_TOKENS
    thr, e1, s2m, e2 = sel
    fin = jnp.ones((1, D_MODEL), F32) if final_gain is None else final_gain.astype(F32)[None]
    row_spec = pl.BlockSpec((PEER_HEADS, PEER_ROWS_PER_STEP, tt), lambda i, j: (0, j, i))
    col_spec = pl.BlockSpec((PEER_HEADS, PEER_NKEYS, tt), lambda i, j: (0, 0, i))
    tok_spec = pl.BlockSpec((tt, D_MODEL), lambda i, j: (i, 0))
    return pl.pallas_call(
        functools.partial(_peer_dense_kernel, final_norm=final_gain is not None),
        grid=(n // tt, PEER_STEPS),
        in_specs=[
            tok_spec,
            pl.BlockSpec((PEER_CHUNK, D_MODEL), lambda i, j: (j, 0)),
            pl.BlockSpec((D_MODEL, PEER_CHUNK), lambda i, j: (0, j)),
            row_spec, row_spec, col_spec, col_spec,
            tok_spec,
            _mod_spec(mod, tokens_per_cond, tt),
            pl.BlockSpec((1, D_MODEL), lambda i, j: (0, 0)),
        ],
        out_specs=tok_spec,
        out_shape=jax.ShapeDtypeStruct((n, D_MODEL), F32),
        scratch_shapes=[pltpu.VMEM((D_MODEL, tt), BF16), pltpu.VMEM((D_MODEL, tt), F32),
                        pltpu.VMEM((2, PEER_SUB_ROWS * PEER_NKEYS, tt), BF16),
                        pltpu.VMEM((2, PEER_SUB_ROWS * PEER_NKEYS, tt), F32)],
        compiler_params=pltpu.CompilerParams(dimension_semantics=("arbitrary", "arbitrary"),
                                             vmem_limit_bytes=VMEM_LIMIT_BYTES),
        name="peer_dense",
    )(h2, u_bf, vt_bf, thr, e1, s2m, e2, xres, mod, fin)


def _peer_residual(x, h2, mod, lp, tokens_per_cond, final_gain):
    sel = _peer_select(h2, lp["peer_wq_bf"], lp["peer_keys_bf"])
    return _peer_dense(h2, lp["peer_u_bf"], lp["peer_vt_bf"], sel, x, mod, tokens_per_cond, final_gain)


def _rms_norm(x, g):
    xf = x.astype(F32)
    y = xf * lax.rsqrt(jnp.mean(xf * xf, axis=-1, keepdims=True) + NORM_EPS)
    return (y * g.astype(F32)).astype(x.dtype)


def _l2_norm(x):
    return x * lax.rsqrt(jnp.sum(x * x, axis=-1, keepdims=True) + NORM_EPS)


def _axial_rope_tables(T):
    t = np.arange(T)
    row, col = t // GRID_W, t % GRID_W
    n_freq = HEAD_DIM // 4
    inv = ROPE_THETA ** (-2.0 * np.arange(n_freq) / (HEAD_DIM // 2))
    ang = np.concatenate([row[:, None] * inv[None], col[:, None] * inv[None]], axis=1)
    return jnp.asarray(np.cos(ang), F32), jnp.asarray(np.sin(ang), F32)


def _apply_rope(x, cos, sin):
    xf = x.astype(F32)
    half = HEAD_DIM // 2
    x1, x2 = xf[..., :half], xf[..., half:]
    c, s = cos[None, :, None, :], sin[None, :, None, :]
    return jnp.concatenate([x1 * c - x2 * s, x2 * c + x1 * s], axis=-1).astype(x.dtype)


def _attention_kernel(q_ref, k_ref, v_ref, o_ref, *, group):
    outs = []
    for h in range(q_ref.shape[2] // HEAD_DIM):
        q = q_ref[0, :, h * HEAD_DIM:(h + 1) * HEAD_DIM].astype(BF16)
        s = lax.dot_general(q, k_ref[0, h // group], (((1,), (1,)), ((), ())), preferred_element_type=F32) * ATTN_SCALE
        m = jnp.max(s, axis=-1, keepdims=True)
        p = jnp.exp(s - m)
        l = jnp.sum(p, axis=-1, keepdims=True)
        outs.append(jnp.dot(p.astype(BF16), v_ref[0, h // group], preferred_element_type=F32) / l)
    o_ref[0] = jnp.concatenate(outs, axis=1)


def _attention(q, k, v):
    B, T, width = q.shape
    hkv, L, dh = k.shape[1], k.shape[2], k.shape[3]
    group = width // dh // hkv
    tq = min(T, ATTN_Q_TILE)
    kv_spec = pl.BlockSpec((1, hkv, L, dh), lambda b, i: (b, 0, 0, 0))
    q_spec = pl.BlockSpec((1, tq, width), lambda b, i: (b, i, 0))
    return pl.pallas_call(
        functools.partial(_attention_kernel, group=group),
        grid=(B, T // tq),
        in_specs=[q_spec, kv_spec, kv_spec],
        out_specs=q_spec,
        out_shape=jax.ShapeDtypeStruct((B, T, width), F32),
        compiler_params=pltpu.CompilerParams(dimension_semantics=("arbitrary",) * 2,
                                             vmem_limit_bytes=VMEM_LIMIT_BYTES),
        name="attention",
    )(q, k.astype(BF16), v.astype(BF16))


def _na_bias_table(bias):
    qcols = np.arange(GRID_W)
    kcols = np.arange(GRID_W)
    cstart = np.clip(qcols - NA_COLS // 2, 0, GRID_W - NA_COLS)
    valid = (kcols[None, :] >= cstart[:, None]) & (kcols[None, :] < cstart[:, None] + NA_COLS)
    dc_idx = np.clip(kcols[None, :] - qcols[:, None], -(NA_COLS - 1), NA_COLS - 1) + (NA_COLS - 1)
    onehot = jnp.asarray(dc_idx[:, :, None] == np.arange(2 * NA_COLS - 1), F32)
    rel = jnp.einsum("qkc,hrc->hrqk", onehot, bias.astype(F32), precision=lax.Precision.HIGHEST)
    rel = jnp.where(valid[None, None], rel, NEG_INF)
    return jnp.stack([jnp.concatenate([rel[:, d0 + j] for j in range(NA_ROWS)], axis=-1)
                      for d0 in range(NA_ROWS)], axis=1)


def _na_kernel(q_ref, k_ref, v_ref, kc_ref, vc_ref, bias_ref, o_ref):
    blk = pl.program_id(1)
    n_rows = k_ref.shape[2] // GRID_W
    win = NA_ROWS * GRID_W
    nt = (((1,), (1,)), ((), ()))
    rows = []
    for i in range(NA_ROW_BLOCK):
        r = blk * NA_ROW_BLOCK + i
        start = jnp.clip(r - NA_ROWS // 2, 0, n_rows - NA_ROWS)
        rows.append((pl.multiple_of(start * GRID_W, GRID_W), start - r + (NA_ROWS - 1)))
    outs = []
    for h in range(NA_HEADS):
        q = _Par([q_ref[0, i * GRID_W:(i + 1) * GRID_W, h * HEAD_DIM:(h + 1) * HEAD_DIM].astype(BF16)
                  for i in range(NA_ROW_BLOCK)])
        k = _Par([k_ref[0, h, pl.ds(off, win), :] for off, _ in rows])
        v = _Par([v_ref[0, h, pl.ds(off, win), :] for off, _ in rows])
        bias = _Par([bias_ref[h, d0] for _, d0 in rows])
        s_loc = _lift(lambda q, k, b: lax.dot_general(q, k, nt, preferred_element_type=F32) * ATTN_SCALE + b,
                      q, k, bias)
        s_ctx = _lift(lambda q: lax.dot_general(q, kc_ref[0, h], nt, preferred_element_type=F32) * ATTN_SCALE, q)
        m = _lift(lambda a, b: jnp.maximum(jnp.max(a, axis=-1, keepdims=True), jnp.max(b, axis=-1, keepdims=True)),
                  s_loc, s_ctx)
        p_loc = _exp(s_loc - m)
        p_ctx = _exp(s_ctx - m)
        l = _lift(lambda a, b: jnp.sum(a, axis=-1, keepdims=True) + jnp.sum(b, axis=-1, keepdims=True), p_loc, p_ctx)
        o = _lift(lambda pl_, v, pc, l: (jnp.dot(pl_.astype(BF16), v, preferred_element_type=F32)
                                         + jnp.dot(pc.astype(BF16), vc_ref[0, h], preferred_element_type=F32)) / l,
                  p_loc, v, p_ctx, l)
        outs.append(jnp.concatenate(o.xs, axis=0))
    o_ref[0] = jnp.concatenate(outs, axis=1)


def _neighbourhood_attention(q, k, v, bias, k_ctx, v_ctx):
    B, H, T, dh = k.shape
    P = k_ctx.shape[2]
    rows = T // GRID_W
    assert rows >= NA_ROWS and rows % NA_ROW_BLOCK == 0
    tq = NA_ROW_BLOCK * GRID_W
    full = lambda n: pl.BlockSpec((1, H, n, dh), lambda b, i: (b, 0, 0, 0))
    q_spec = pl.BlockSpec((1, tq, H * dh), lambda b, i: (b, i, 0))
    return pl.pallas_call(
        _na_kernel,
        grid=(B, rows // NA_ROW_BLOCK),
        in_specs=[q_spec, full(T), full(T), full(P), full(P),
                  pl.BlockSpec((H, NA_ROWS, GRID_W, NA_ROWS * GRID_W), lambda b, i: (0, 0, 0, 0))],
        out_specs=q_spec,
        out_shape=jax.ShapeDtypeStruct((B, T, H * dh), F32),
        compiler_params=pltpu.CompilerParams(dimension_semantics=("arbitrary",) * 2,
                                             vmem_limit_bytes=VMEM_LIMIT_BYTES),
        name="na_attention",
    )(q, k.astype(BF16), v.astype(BF16), k_ctx.astype(BF16), v_ctx.astype(BF16), _na_bias_table(bias))


def _gqa_mixer(p, lp, ctx_kv):
    B, T, _ = p.shape
    kvw = GQA_KV_HEADS * HEAD_DIM
    q = _rms_norm(p[..., :GROUP_W].reshape(B, T, GQA_HEADS, HEAD_DIM), lp["gqa_q_norm"])
    k = _rms_norm(p[..., GROUP_W:GROUP_W + kvw].reshape(B, T, GQA_KV_HEADS, HEAD_DIM), lp["gqa_k_norm"])
    v = p[..., GROUP_W + kvw:].reshape(B, T, GQA_KV_HEADS, HEAD_DIM)
    if ctx_kv is not None:
        cos, sin = _axial_rope_tables(T)
        q = _apply_rope(q, cos, sin)
        k = _apply_rope(k, cos, sin)
    qf = q.reshape(B, T, GROUP_W)
    kh = k.transpose(0, 2, 1, 3)
    vh = v.transpose(0, 2, 1, 3)
    if ctx_kv is None:
        o = _attention(qf, kh, vh)
    else:
        k_ctx, v_ctx = ctx_kv
        o = _attention(qf, jnp.concatenate([kh, k_ctx], axis=2), jnp.concatenate([vh, v_ctx], axis=2))
    return o, (kh, vh)


def _na_mixer(p, lp, ctx_kv):
    B, T, _ = p.shape
    q = p[..., :GROUP_W]
    k, v = [p[..., i * GROUP_W:(i + 1) * GROUP_W].reshape(B, T, NA_HEADS, HEAD_DIM).transpose(0, 2, 1, 3)
            for i in (1, 2)]
    if ctx_kv is None:
        o = _attention(q, k, v)
    else:
        o = _neighbourhood_attention(q, k, v, lp["na_bias"], ctx_kv[0], ctx_kv[1])
    return o, (k, v)


def _mm(a, b, dims, two_pass):
    f = lambda x, y: lax.dot_general(x, y, (dims, ((), ())), preferred_element_type=F32)

    def one(a, b):
        a16 = a.astype(BF16)
        hi = b.astype(BF16)
        if not two_pass:
            return f(a16, hi)
        return f(a16, hi) + f(a16, (b - hi.astype(F32)).astype(BF16))

    return _lift(one, a, b)


def _dot(a, b, two_pass=False):
    return _mm(a, b, ((1,), (0,)), two_pass)


def _dot_nt(a, b, two_pass=False):
    return _mm(a, b, ((1,), (1,)), two_pass)


def _dot_tn(a, b):
    return _mm(a, b, ((0,), (0,)), False)


def _chunk_masks(reverse):
    row = lax.broadcasted_iota(jnp.int32, (SCAN_CHUNK, SCAN_CHUNK), 0)
    col = lax.broadcasted_iota(jnp.int32, (SCAN_CHUNK, SCAN_CHUNK), 1)
    earlier = (row < col) if reverse else (row > col)
    eye = row == col
    blk16 = (row >> 4) == (col >> 4)
    blk32 = (row >> 5) == (col >> 5)
    return dict(
        strict=earlier.astype(F32), incl=(earlier | eye).astype(F32), eye=eye.astype(F32),
        d16=blk16.astype(F32), off32=(blk32 & jnp.logical_not(blk16)).astype(F32),
        off64=jnp.logical_not(blk32).astype(F32))


def _scan_chains(heads):
    fwd, bwd = _chunk_masks(False), _chunk_masks(True)
    chains = [(b, d, h) for b in range(SCAN_ROWS) for d in range(2) for h in range(heads)]
    masks = {key: _Par([(bwd if d else fwd)[key] for _, d, _ in chains]) for key in fwd}
    return chains, [d == 1 for _, d, _ in chains], masks


def _last_row(cum, revs):
    return _Par([c[0:1] if rev else c[SCAN_CHUNK - 1:SCAN_CHUNK] for c, rev in zip(cum.xs, revs)])


def _unit_triangular_inverse(n, m):
    eye = m["eye"]
    nd = n * m["d16"]
    n2 = _dot(nd, nd)
    x = eye + nd
    x = x + _dot(x, n2)
    n4 = _dot(n2, n2)
    x = x + _dot(x, n4)
    n8 = _dot(n4, n4)
    x = x + _dot(x, n8)
    x = x + _dot(x, _dot(n * m["off32"], x))
    x = x + _dot(x, _dot(n * m["off64"], x))
    return x


def _rwkv_chunk(r, lw, k, v, a, b, h0, m, revs):
    cum = _dot(m["incl"], lw, two_pass=True)
    last = _last_row(cum, revs)
    g_inv = _exp(-cum)
    at = a * _exp(cum - lw)
    rt = r * _exp(cum)
    bt = b * g_inv
    kt = k * g_inv
    to_end = _exp(last - cum)
    bh = b * to_end
    kh = k * to_end
    ar = _stack_rows(at, rt)
    gb = _dot_nt(ar, bt)
    gk = _dot_nt(ar, kt)
    n_ab = gb[:SCAN_CHUNK] * m["strict"]
    l_ak = gk[:SCAN_CHUNK] * m["strict"]
    m_rb = gb[SCAN_CHUNK:] * m["incl"]
    m_rk = gk[SCAN_CHUNK:] * m["incl"]
    tinv = _unit_triangular_inverse(n_ab, m)
    p1 = _dot(tinv, at)
    u0 = _dot(tinv, _dot(l_ak, v))
    p2 = rt + _dot(m_rb, p1)
    o0 = _dot(m_rb, u0) + _dot(m_rk, v)
    a_c = m["eye"] * _exp(last) + _dot_tn(bh, p1)
    g_c = _dot_tn(bh, u0) + _dot_tn(kh, v)
    return _dot(p2, h0) + o0, _dot(a_c, h0) + g_c


def _rwkv_scan_kernel(*refs):
    ins, h0_ref = refs[:12], refs[12]
    o_refs, hout_ref, h_ref = refs[13:15], refs[15], refs[16]
    c = pl.program_id(1)

    @pl.when(c == 0)
    def _():
        h_ref[...] = h0_ref[...]

    chains, revs, m = _scan_chains(RW_HEADS)
    seqs = [_Par([_head_tile(ins[6 * d + i], b, h) for b, d, h in chains]) for i in range(6)]
    o, h_new = _rwkv_chunk(*seqs, _Par([h_ref[b, d, h] for b, d, h in chains]), m, revs)
    _store_heads(o_refs, o, chains)
    for i, (b, d, h) in enumerate(chains):
        h_ref[b, d, h] = h_new.xs[i]

    @pl.when(c == pl.num_programs(1) - 1)
    def _():
        hout_ref[...] = h_ref[...]


def _head_tile(ref, b, h):
    return ref[b, :, h * HEAD_DIM:(h + 1) * HEAD_DIM]


def _store_heads(o_refs, o, chains):
    for d in range(2):
        for b in range(SCAN_ROWS):
            tiles = [o.xs[i] for i, (bi, di, _) in enumerate(chains) if (bi, di) == (b, d)]
            o_refs[d][b] = jnp.concatenate(tiles, axis=1)


def _scan_specs(B, H, T, dh):
    n = T // SCAN_CHUNK
    fwd = pl.BlockSpec((SCAN_ROWS, SCAN_CHUNK, H * dh), lambda b, c: (b, c, 0))
    bwd = pl.BlockSpec((SCAN_ROWS, SCAN_CHUNK, H * dh), lambda b, c: (b, n - 1 - c, 0))
    state = pl.BlockSpec((SCAN_ROWS, 2, H, dh, dh), lambda b, c: (b, 0, 0, 0, 0))
    return (B // SCAN_ROWS, n), fwd, bwd, state


def _rwkv_scan_pallas(seqs, h0):
    B, T, _ = seqs[0][0].shape
    H, dh = RW_HEADS, HEAD_DIM
    grid, fwd, bwd, state_spec = _scan_specs(B, H, T, dh)
    seq = jax.ShapeDtypeStruct((B, T, H * dh), F32)
    return pl.pallas_call(
        _rwkv_scan_kernel,
        grid=grid,
        in_specs=[fwd] * 6 + [bwd] * 6 + [state_spec],
        out_specs=[fwd, bwd, state_spec],
        out_shape=[seq, seq, jax.ShapeDtypeStruct((B, 2, H, dh, dh), F32)],
        scratch_shapes=[pltpu.VMEM((SCAN_ROWS, 2, H, dh, dh), F32)],
        compiler_params=pltpu.CompilerParams(dimension_semantics=("arbitrary", "arbitrary"),
                                             vmem_limit_bytes=VMEM_LIMIT_BYTES),
        name="rwkv_scan",
    )(*seqs[0], *seqs[1], h0)


def _rwkv_mixer(p, lp, S0):
    B, T, _ = p.shape
    W, H = GROUP_W, RW_HEADS
    o0 = 3 * W
    o1 = o0 + 2 * RW_DECAY_RANK
    o2 = o1 + 2 * RW_AAA_RANK
    rkv = p[..., :o0]
    wd = p[..., o0:o1].reshape(B, T, 2, RW_DECAY_RANK)
    ad = p[..., o1:o2].reshape(B, T, 2, RW_AAA_RANK)
    gate = jax.nn.sigmoid(p[..., o2:]) @ lp["rw_g2"]
    heads = lambda a: a.astype(F32).reshape(B, T, H, HEAD_DIM)
    flat = lambda a: a.reshape(B, T, W)
    seqs, bonuses = [], []
    for d in range(2):
        base = jnp.concatenate([rkv, wd[:, :, d], ad[:, :, d]], axis=-1)
        if d == 0:
            shifted = jnp.pad(base[:, :-1], ((0, 0), (1, 0), (0, 0)))
        else:
            shifted = jnp.pad(base[:, 1:], ((0, 0), (0, 1), (0, 0)))
        xd = base + (shifted - base) * lp["rw_mu"][d]
        r, k, v = xd[..., :W], xd[..., W:2 * W], xd[..., 2 * W:3 * W]
        wl, al = xd[..., 3 * W:3 * W + RW_DECAY_RANK], xd[..., 3 * W + RW_DECAY_RANK:]
        w = -jax.nn.softplus(-(lp["rw_w0"][d] + jnp.tanh(wl) @ lp["rw_w2"][d])) - 0.5
        log_decay = -jnp.exp(w.astype(F32))
        a = jax.nn.sigmoid((lp["rw_a0"][d] + al @ lp["rw_a2"][d]).astype(F32))
        kf = k.astype(F32)
        kk = _l2_norm(heads(kf * lp["rw_k_k"].astype(F32)))
        k_eff = heads(kf * (1.0 + (a - 1.0) * lp["rw_k_a"].astype(F32)))
        rh, vh, ah = heads(r), heads(v), heads(a)
        seqs.append([r.astype(F32), log_decay, flat(k_eff), v.astype(F32), flat(-kk), flat(kk * ah)])
        bonuses.append(jnp.sum(rh * k_eff * lp["rw_r_k"].astype(F32), axis=-1, keepdims=True) * vh)
    o_f, o_b, h_fin = _rwkv_scan_pallas(seqs, jnp.swapaxes(S0.astype(F32), -1, -2))
    o = heads(o_f + o_b)
    finals = jnp.swapaxes(h_fin, -1, -2)
    mu = jnp.mean(o, axis=-1, keepdims=True)
    var = jnp.mean(jnp.square(o - mu), axis=-1, keepdims=True)
    y = ((o - mu) * lax.rsqrt(var + RW_LN_EPS)).reshape(B, T, W) * lp["rw_ln_g"].astype(F32) + lp["rw_ln_b"].astype(F32)
    y = y + (bonuses[0] + bonuses[1]).reshape(B, T, W)
    return y.astype(p.dtype) * gate, finals


def _delta_chunk(q, k, v, gb, betab, s0, m, revs):
    gc = _dot(m["incl"], gb, two_pass=True)
    last = _last_row(gc, revs)
    diff = gc - _dot_nt(m["eye"], gc, two_pass=True)
    dmat = _lift(lambda keep, d: jnp.where(keep > 0.0, jnp.exp(jnp.where(keep > 0.0, d, 0.0)), 0.0), m["incl"], diff)
    kb = k * betab
    gram = _dot_nt(_stack_rows(kb, q), k)
    a_mat = gram[:SCAN_CHUNK] * dmat * m["strict"]
    qk = gram[SCAN_CHUNK:] * dmat
    tm = _unit_triangular_inverse(-a_mat, m)
    u = _dot(tm, v * betab)
    w = _dot(tm, kb * _exp(gc))
    v_new = u - _dot(w, s0)
    o = _dot(q * _exp(gc), s0) + _dot(qk, v_new)
    return o, s0 * _exp(last) + _dot_tn(k * _exp(last - gc), v_new)


def _delta_scan_kernel(*refs):
    ins, s0_ref = refs[:10], refs[10]
    o_refs, sout_ref, s_ref = refs[11:13], refs[13], refs[14]
    c = pl.program_id(1)

    @pl.when(c == 0)
    def _():
        s_ref[...] = s0_ref[...]

    chains, revs, m = _scan_chains(DN_HEADS)
    seqs = [_Par([_head_tile(ins[5 * d + i], b, h) for b, d, h in chains]) for i in range(5)]
    o, s_new = _delta_chunk(*seqs, _Par([s_ref[b, d, h] for b, d, h in chains]), m, revs)
    _store_heads(o_refs, o, chains)
    for i, (b, d, h) in enumerate(chains):
        s_ref[b, d, h] = s_new.xs[i]

    @pl.when(c == pl.num_programs(1) - 1)
    def _():
        sout_ref[...] = s_ref[...]


def _delta_scan_pallas(q, k, v, g, beta, s0):
    assert SCAN_CHUNK == HEAD_DIM
    B, T, _ = q.shape
    H, dh = DN_HEADS, HEAD_DIM
    grid, fwd, bwd, state_spec = _scan_specs(B, H, T, dh)
    seq = jax.ShapeDtypeStruct((B, T, H * dh), F32)
    return pl.pallas_call(
        _delta_scan_kernel,
        grid=grid,
        in_specs=[fwd] * 5 + [bwd] * 5 + [state_spec],
        out_specs=[fwd, bwd, state_spec],
        out_shape=[seq, seq, jax.ShapeDtypeStruct((B, 2, H, dh, dh), F32)],
        scratch_shapes=[pltpu.VMEM((SCAN_ROWS, 2, H, dh, dh), F32)],
        compiler_params=pltpu.CompilerParams(dimension_semantics=("arbitrary", "arbitrary"),
                                             vmem_limit_bytes=VMEM_LIMIT_BYTES),
        name="delta_scan",
    )(q, k, v, g[0], beta[0], q, k, v, g[1], beta[1], s0)


def _deltanet_mixer(p, lp, S0):
    B, T, _ = p.shape
    W, H = GROUP_W, DN_HEADS
    xin = jnp.pad(p[..., :3 * W], ((0, 0), (DN_CONV // 2, DN_CONV // 2), (0, 0)))
    qkv = sum(xin[:, j:j + T] * lp["dn_conv"][j] for j in range(DN_CONV))
    qkv = jax.nn.silu(qkv)
    heads = lambda a: a.astype(F32).reshape(B, T, H, HEAD_DIM)
    flat = lambda a: a.reshape(B, T, W)
    q = _l2_norm(heads(qkv[..., :W])) * (HEAD_DIM ** -0.5)
    k = _l2_norm(heads(qkv[..., W:2 * W]))
    v = heads(qkv[..., 2 * W:])
    beta = jax.nn.sigmoid(p[..., 3 * W:3 * W + 2 * H].astype(F32)).reshape(B, T, 2, H)
    alpha = p[..., 3 * W + 2 * H:3 * W + 4 * H].astype(F32).reshape(B, T, 2, H)
    g = -jnp.exp(lp["dn_a_log"].astype(F32)) * jax.nn.softplus(alpha + lp["dn_dt_bias"].astype(F32))
    z = heads(p[..., 3 * W + 4 * H:])
    lanes = lambda a: [jnp.repeat(a[:, :, d], HEAD_DIM, axis=-1) for d in range(2)]
    o_f, o_b, s_fin = _delta_scan_pallas(flat(q), flat(k), flat(v), lanes(g), lanes(beta), S0.astype(F32))
    o = _rms_norm(heads(o_f + o_b), lp["dn_norm_g"]) * jax.nn.silu(z)
    return o.reshape(B, T, W).astype(p.dtype), s_fin


def _modulated_norm(x, gain, scale, shift):
    y = x * lax.rsqrt(jnp.mean(x * x, axis=-1, keepdims=True) + NORM_EPS)
    return (y * gain) * (1.0 + scale) + shift


def _proj_in_kernel(x_ref, mod_ref, gain_ref, w_ref, o_ref):
    h = _modulated_norm(x_ref[...], gain_ref[...], mod_ref[0, MOD_SCALE1:MOD_SCALE1 + 1], mod_ref[0, MOD_SHIFT1:MOD_SHIFT1 + 1])
    o_ref[...] = jnp.dot(h.astype(BF16), w_ref[...], preferred_element_type=F32)


def _proj_out_kernel(mix_ref, x_ref, mod_ref, gain_ref, w_ref, x_out_ref, h2_ref):
    y = jnp.dot(mix_ref[...].astype(BF16), w_ref[...], preferred_element_type=F32)
    x_new = x_ref[...] + mod_ref[0, MOD_GATE1:MOD_GATE1 + 1] * y
    x_out_ref[...] = x_new
    h2_ref[...] = _modulated_norm(x_new, gain_ref[...], mod_ref[0, MOD_SCALE2:MOD_SCALE2 + 1],
                                  mod_ref[0, MOD_SHIFT2:MOD_SHIFT2 + 1])


def _proj_in(x, mod, gain, w_bf, tokens_per_cond):
    n, cols = x.shape[0], w_bf.shape[1]
    tm = PROJ_TOKENS
    tok = pl.BlockSpec((tm, D_MODEL), lambda i: (i, 0))
    return pl.pallas_call(
        _proj_in_kernel,
        grid=(n // tm,),
        in_specs=[tok, _mod_spec(mod, tokens_per_cond, tm), pl.BlockSpec((1, D_MODEL), lambda i: (0, 0)),
                  pl.BlockSpec((D_MODEL, cols), lambda i: (0, 0))],
        out_specs=pl.BlockSpec((tm, cols), lambda i: (i, 0)),
        out_shape=jax.ShapeDtypeStruct((n, cols), F32),
        compiler_params=pltpu.CompilerParams(dimension_semantics=("arbitrary",), vmem_limit_bytes=VMEM_LIMIT_BYTES),
        name="proj_in",
    )(x, mod, gain, w_bf)


def _proj_out(mix, x, mod, gain, w_bf, tokens_per_cond):
    n = x.shape[0]
    tm = PROJ_TOKENS
    tok = pl.BlockSpec((tm, D_MODEL), lambda i: (i, 0))
    out = jax.ShapeDtypeStruct((n, D_MODEL), F32)
    return pl.pallas_call(
        _proj_out_kernel,
        grid=(n // tm,),
        in_specs=[tok, tok, _mod_spec(mod, tokens_per_cond, tm), pl.BlockSpec((1, D_MODEL), lambda i: (0, 0)),
                  pl.BlockSpec((D_MODEL, D_MODEL), lambda i: (0, 0))],
        out_specs=[tok, tok],
        out_shape=[out, out],
        compiler_params=pltpu.CompilerParams(dimension_semantics=("arbitrary",), vmem_limit_bytes=VMEM_LIMIT_BYTES),
        name="proj_out",
    )(mix, x, mod, gain, w_bf)


def _trunk_layer(x, cond, lp, ctx, final_gain):
    B, T, _ = x.shape
    mod = (jax.nn.silu(cond) @ lp["w_mod"] + lp["b_mod"]).reshape(-1, 6, D_MODEL)
    x = x.reshape(B * T, D_MODEL)
    proj = _proj_in(x, mod, lp["norm1_g"][None], lp["w_in_bf"], T).reshape(B, T, -1)
    pa, pb, pc, pd, _ = jnp.split(proj, [GQA_PROJ, GQA_PROJ + RW_PROJ, GQA_PROJ + RW_PROJ + NA_PROJ, IN_COLS], axis=-1)
    if ctx is None:
        ctx_a = None
        ctx_c = None
        s_rw0 = jnp.zeros((B, 2, RW_HEADS, HEAD_DIM, HEAD_DIM), F32)
        s_dn0 = jnp.zeros((B, 2, DN_HEADS, HEAD_DIM, HEAD_DIM), F32)
    else:
        ka_c, va_c, kc_c, vc_c, s_rw0, s_dn0 = ctx
        ctx_a = (ka_c, va_c)
        ctx_c = (kc_c, vc_c)
    oa, (ka, va) = _gqa_mixer(pa, lp, ctx_a)
    ob, s_rw = _rwkv_mixer(pb, lp, s_rw0)
    oc, (kc, vc) = _na_mixer(pc, lp, ctx_c)
    od, s_dn = _deltanet_mixer(pd, lp, s_dn0)
    mix = jnp.concatenate([oa, ob, oc, od], axis=-1).reshape(B * T, D_MODEL)
    x, h2 = _proj_out(mix, x, mod, lp["norm2_g"][None], lp["w_out_bf"], T)
    x = _peer_residual(x, h2, mod, lp, T, final_gain)
    return x.reshape(B, T, D_MODEL), (ka, va, kc, vc, s_rw, s_dn)


def kernel(x_prompt, x_sample, c, cache_gqa_k, cache_gqa_v, cache_na_k, cache_na_v, state_rwkv, state_delta, c_ctx, norm1_g, norm2_g, w_mod, b_mod, w_in, w_out, gqa_q_norm, gqa_k_norm, rw_mu, rw_w0, rw_w2, rw_a0, rw_a2, rw_g2, rw_k_k, rw_k_a, rw_r_k, rw_ln_g, rw_ln_b, na_bias, dn_conv, dn_a_log, dn_dt_bias, dn_norm_g, peer_wq, peer_keys, peer_u, peer_v, final_norm_g):
    depth = w_in.shape[0]

    def layer_params(l):
        return {
            "norm1_g": norm1_g[l], "norm2_g": norm2_g[l], "w_mod": w_mod[l], "b_mod": b_mod[l],
            "w_in_bf": jnp.pad(w_in[l].astype(BF16), ((0, 0), (0, IN_COLS_PADDED - IN_COLS))),
            "w_out_bf": w_out[l].astype(BF16), "gqa_q_norm": gqa_q_norm[l], "gqa_k_norm": gqa_k_norm[l],
            "rw_mu": rw_mu[l], "rw_w0": rw_w0[l], "rw_w2": rw_w2[l], "rw_a0": rw_a0[l], "rw_a2": rw_a2[l],
            "rw_g2": rw_g2[l], "rw_k_k": rw_k_k[l], "rw_k_a": rw_k_a[l], "rw_r_k": rw_r_k[l],
            "rw_ln_g": rw_ln_g[l], "rw_ln_b": rw_ln_b[l], "na_bias": na_bias[l],
            "dn_conv": dn_conv[l], "dn_a_log": dn_a_log[l], "dn_dt_bias": dn_dt_bias[l], "dn_norm_g": dn_norm_g[l],
            "peer_wq_bf": peer_wq[l].astype(BF16),
            "peer_keys_bf": peer_keys[l].reshape(2 * PEER_HEADS, PEER_NKEYS, PEER_DQ // 2).astype(BF16),
            "peer_u_bf": peer_u[l].astype(BF16),
            "peer_vt_bf": peer_v[l].astype(BF16).T,
        }

    params = [layer_params(l) for l in range(depth)]

    xp = x_prompt
    cond_ctx = c_ctx[None, :]
    per_layer = []
    closing = lambda l: final_norm_g if l == depth - 1 else None
    for l in range(depth):
        xp, st = _trunk_layer(xp, cond_ctx, params[l], None, closing(l))
        per_layer.append(st)
    y_prompt = xp

    xs = x_sample
    for l in range(depth):
        ctx = (cache_gqa_k[:, l], cache_gqa_v[:, l], cache_na_k[:, l], cache_na_v[:, l],
               state_rwkv[:, l], state_delta[:, l])
        xs, _ = _trunk_layer(xs, c, params[l], ctx, closing(l))
    y_sample = xs

    dt = x_prompt.dtype
    new_gqa_k = jnp.stack([st[0] for st in per_layer], axis=1).astype(dt)
    new_gqa_v = jnp.stack([st[1] for st in per_layer], axis=1).astype(dt)
    new_na_k = jnp.stack([st[2] for st in per_layer], axis=1).astype(dt)
    new_na_v = jnp.stack([st[3] for st in per_layer], axis=1).astype(dt)
    new_state_rwkv = jnp.stack([st[4] for st in per_layer], axis=1).astype(dt)
    new_state_delta = jnp.stack([st[5] for st in per_layer], axis=1).astype(dt)
    return (y_prompt, y_sample, new_gqa_k, new_gqa_v, new_na_k, new_na_v, new_state_rwkv, new_state_delta)
```

```python
import functools
import math

import jax
import jax.numpy as jnp
import numpy as np
from jax import lax
from jax.experimental import pallas as pl
from jax.experimental.pallas import tpu as pltpu

F32 = jnp.float32
BF16 = jnp.bfloat16

D_MODEL = 1024
GRID_W = 64
HEAD_DIM = 64
GROUP_W = D_MODEL // 4
NORM_EPS = 1e-6
NEG_INF = -1e30
GQA_HEADS = GROUP_W // HEAD_DIM
GQA_KV_HEADS = GQA_HEADS // 2
ROPE_THETA = 10000.0
RW_HEADS = GROUP_W // HEAD_DIM
RW_DECAY_RANK = 32
RW_AAA_RANK = 32
RW_GATE_RANK = 64
RW_LN_EPS = 64e-5
NA_HEADS = GROUP_W // HEAD_DIM
NA_ROWS = 8
NA_COLS = 16
DN_HEADS = GROUP_W // HEAD_DIM
DN_CONV = 5
PEER_HEADS = 8
PEER_NKEYS = 128
PEER_EXPERTS = PEER_NKEYS * PEER_NKEYS
PEER_DQ = 256
PEER_TOPK = 16
GQA_PROJ = GROUP_W + 2 * GQA_KV_HEADS * HEAD_DIM
RW_PROJ = 3 * GROUP_W + 2 * RW_DECAY_RANK + 2 * RW_AAA_RANK + RW_GATE_RANK
NA_PROJ = 3 * GROUP_W
DN_PROJ = 3 * GROUP_W + 4 * DN_HEADS + GROUP_W

LANES = 128
VMEM_LIMIT_BYTES = 56 * 1024 * 1024

PEER_SEL_TOKENS = 128
PEER_TOKENS = 256
PEER_ROWS_PER_STEP = 16
PEER_SUB_ROWS = 2
PEER_HALF = 64
PEER_MATMUL_PIECE = 256
PEER_CHUNK = PEER_ROWS_PER_STEP * PEER_NKEYS
PEER_STEPS = PEER_EXPERTS // PEER_CHUNK
INV_SQRT2 = 1.0 / math.sqrt(2.0)

IN_COLS = GQA_PROJ + RW_PROJ + NA_PROJ + DN_PROJ
IN_COLS_PADDED = -(-IN_COLS // LANES) * LANES
MOD_SHIFT1, MOD_SCALE1, MOD_GATE1, MOD_SHIFT2, MOD_SCALE2, MOD_GATE2 = range(6)
PROJ_TOKENS = 512

ATTN_SCALE = HEAD_DIM ** -0.5
ATTN_Q_TILE = 256
NA_ROW_BLOCK = 8

SCAN_CHUNK = 64
SCAN_ROWS = 4


class _Par:
    def __init__(self, xs):
        self.xs = list(xs)

    def __add__(self, o):
        return _lift(jnp.add, self, o)

    __radd__ = __add__

    def __sub__(self, o):
        return _lift(jnp.subtract, self, o)

    def __rsub__(self, o):
        return _lift(lambda x, y: y - x, self, o)

    def __mul__(self, o):
        return _lift(jnp.multiply, self, o)

    __rmul__ = __mul__

    def __neg__(self):
        return _lift(jnp.negative, self)

    def __getitem__(self, idx):
        return _lift(lambda x: x[idx], self)


def _lift(f, *args):
    n = next(len(a.xs) for a in args if isinstance(a, _Par))
    return _Par([f(*[a.xs[i] if isinstance(a, _Par) else a for a in args]) for i in range(n)])


def _exp(x):
    return _lift(jnp.exp, x)


def _stack_rows(x, y):
    return _lift(lambda a, b: jnp.concatenate([a, b], axis=0), x, y)


def _top_k_rows(s, k, exact):
    rows = s.shape[0]
    iota = lax.broadcasted_iota(jnp.int32, s.shape, 0).astype(F32)
    vals = []
    work = s
    for _ in range(k):
        m = jnp.max(work, axis=0, keepdims=True)
        if exact:
            first = jnp.min(jnp.where(work == m, iota, float(rows)), axis=0, keepdims=True)
            work = jnp.where(iota == first, -jnp.inf, work)
        else:
            work = jnp.where(work == m, -jnp.inf, work)
        vals.append(m)
    if exact:
        return vals, work, jnp.zeros_like(vals[0])
    removed = jnp.sum(jnp.where(work == -jnp.inf, 1.0, 0.0), axis=0, keepdims=True)
    return vals, work, jnp.where(removed == float(k), 0.0, 1.0)


def _peer_select_head(q, keys_ref, h, thr_ref, e1_ref, s2m_ref, e2_ref, exact):
    half = PEER_DQ // 2
    scores, tops, sel = [], [], []
    suspect = None
    for p in range(2):
        c0 = (2 * h + p) * half
        qhp = q[:, c0:c0 + half].astype(BF16)
        s = lax.dot_general(keys_ref[2 * h + p], qhp, (((1,), (1,)), ((), ())),
                            preferred_element_type=F32)
        vals, work, flag = _top_k_rows(s, PEER_TOPK, exact)
        suspect = flag if suspect is None else suspect + flag
        scores.append(s)
        tops.append(vals)
        sel.append(work == -jnp.inf)
    m1, m2 = tops
    m2_all = jnp.concatenate(m2, axis=0)
    cand = jnp.concatenate([m1[0] + m2_all] + [m1[i] + m2_all[:8] for i in range(1, 8)]
                           + [jnp.concatenate(m1[8:], axis=0) + m2[0]], axis=0)
    top_s, _, flag = _top_k_rows(cand, PEER_TOPK, exact)
    suspect = suspect + flag
    mx = top_s[0]
    z = jnp.exp(top_s[0] - mx)
    for r in range(1, PEER_TOPK):
        z = z + jnp.exp(top_s[r] - mx)
    half_inv_z = 0.5 / z
    t3 = top_s[PEER_TOPK - 1]
    s1m = jnp.where(sel[0], scores[0], -jnp.inf)
    thr = jnp.full_like(s1m, jnp.inf)
    for j in range(PEER_TOPK):
        thr = jnp.where(s1m + m2[j] >= t3, m2[j], thr)
    thr_ref[h] = thr
    e1_ref[h] = jnp.where(sel[0], jnp.exp(scores[0] - m1[0]) * half_inv_z, 0.0)
    s2m_ref[h] = jnp.where(sel[1], scores[1], -jnp.inf)
    e2_ref[h] = jnp.exp(scores[1] - m2[0])
    return suspect


def _peer_select_kernel(h_ref, wq_ref, keys_ref, *out_refs):
    x = h_ref[...].astype(BF16)
    q = jnp.dot(x, wq_ref[...], preferred_element_type=F32)
    flags = [_peer_select_head(q, keys_ref, h, *out_refs, exact=False) for h in range(PEER_HEADS)]

    @pl.when(jnp.max(sum(flags)) > 0.0)
    def _():
        for h in range(PEER_HEADS):
            @pl.when(jnp.max(flags[h]) > 0.0)
            def _():
                _peer_select_head(q, keys_ref, h, *out_refs, exact=True)


def _peer_select(h2, wq_bf, keys_bf):
    n = h2.shape[0]
    tt = PEER_SEL_TOKENS
    big = jax.ShapeDtypeStruct((PEER_HEADS, PEER_NKEYS, n), F32)
    big_spec = pl.BlockSpec((PEER_HEADS, PEER_NKEYS, tt), lambda i: (0, 0, i))
    return pl.pallas_call(
        _peer_select_kernel,
        grid=(n // tt,),
        in_specs=[
            pl.BlockSpec((tt, D_MODEL), lambda i: (i, 0)),
            pl.BlockSpec((D_MODEL, PEER_HEADS * PEER_DQ), lambda i: (0, 0)),
            pl.BlockSpec((2 * PEER_HEADS, PEER_NKEYS, PEER_DQ // 2), lambda i: (0, 0, 0)),
        ],
        out_specs=[big_spec] * 4,
        out_shape=[big] * 4,
        compiler_params=pltpu.CompilerParams(dimension_semantics=("arbitrary",),
                                             vmem_limit_bytes=VMEM_LIMIT_BYTES),
        name="peer_select",
    )(h2, wq_bf, keys_bf)


def _peer_dense_kernel(h_ref, u_ref, vt_ref, thr_ref, e1_ref, s2m_ref, e2_ref, xres_ref, mod_ref, fin_ref,
                       out_ref, ht_ref, acc_ref, p_ref, act_ref, *, final_norm):
    j = pl.program_id(1)
    tt = h_ref.shape[0]

    @pl.when(j == 0)
    def _():
        ht_ref[...] = h_ref[...].T.astype(BF16)
        acc_ref[...] = jnp.zeros_like(acc_ref)

    sub = PEER_SUB_ROWS * PEER_NKEYS
    lane_blocks = tt // LANES
    halves = PEER_NKEYS // PEER_HALF
    n_block = lane_blocks * halves
    kp = PEER_MATMUL_PIECE
    n_piece = D_MODEL // kp
    assert n_block % n_piece == 0
    blocks_per_piece = n_block // n_piece

    def activation_piece(k, i):
        part = jnp.dot(u_ref[k * sub:(k + 1) * sub, i * kp:(i + 1) * kp], ht_ref[i * kp:(i + 1) * kp, :],
                       preferred_element_type=F32)
        if i == 0:
            act_ref[k % 2] = part
        else:
            act_ref[k % 2] += part

    def accumulate_piece(k, i):
        acc_ref[i * kp:(i + 1) * kp, :] += jnp.dot(vt_ref[i * kp:(i + 1) * kp, k * sub:(k + 1) * sub], p_ref[k % 2],
                                                   preferred_element_type=F32)

    def weighted_block(k, i):
        c, hb = divmod(i, halves)
        cs = slice(c * LANES, (c + 1) * LANES)
        bs = slice(hb * PEER_HALF, (hb + 1) * PEER_HALF)
        firsts = [k * PEER_SUB_ROWS + ai for ai in range(PEER_SUB_ROWS)]
        w = [jnp.zeros((PEER_HALF, LANES), F32) for _ in firsts]
        for h in range(PEER_HEADS):
            s2 = s2m_ref[h, bs, cs]
            e2 = e2_ref[h, bs, cs]
            for ai, a in enumerate(firsts):
                w[ai] = w[ai] + jnp.where(s2 >= thr_ref[h, a:a + 1, cs], e2, 0.0) * e1_ref[h, a:a + 1, cs]
        for ai in range(PEER_SUB_ROWS):
            rows = slice(ai * PEER_NKEYS + hb * PEER_HALF, ai * PEER_NKEYS + (hb + 1) * PEER_HALF)
            zz = act_ref[k % 2, rows, cs]
            gelu2 = zz * (1.0 + lax.erf(zz * INV_SQRT2))
            p_ref[k % 2, rows, cs] = (w[ai] * gelu2).astype(BF16)

    n_sub = PEER_ROWS_PER_STEP // PEER_SUB_ROWS
    for i in range(n_piece):
        activation_piece(0, i)
    for k in range(n_sub):
        for blk in range(n_block):
            if blk % blocks_per_piece == 0:
                if k + 1 < n_sub:
                    activation_piece(k + 1, blk // blocks_per_piece)
                if k > 0:
                    accumulate_piece(k - 1, blk // blocks_per_piece)
            weighted_block(k, blk)
    for i in range(n_piece):
        accumulate_piece(n_sub - 1, i)

    @pl.when(j == PEER_STEPS - 1)
    def _():
        y = xres_ref[...] + mod_ref[0, MOD_GATE2:MOD_GATE2 + 1] * acc_ref[...].T
        if final_norm:
            y = y * lax.rsqrt(jnp.mean(y * y, axis=-1, keepdims=True) + NORM_EPS) * fin_ref[...]
        out_ref[...] = y


def _mod_spec(mod, tokens_per_cond, tile):
    tiles_per_cond = tokens_per_cond // tile
    if mod.shape[0] == 1:
        return pl.BlockSpec((1, 6, D_MODEL), lambda i, *_: (0, 0, 0))
    assert tokens_per_cond % tile == 0
    return pl.BlockSpec((1, 6, D_MODEL), lambda i, *_: (i // tiles_per_cond, 0, 0))


def _peer_dense(h2, u_bf, vt_bf, sel, xres, mod, tokens_per_cond, final_gain):
    n = h2.shape[0]
    tt = PEER_TOKENS
    thr, e1, s2m, e2 = sel
    fin = jnp.ones((1, D_MODEL), F32) if final_gain is None else final_gain.astype(F32)[None]
    row_spec = pl.BlockSpec((PEER_HEADS, PEER_ROWS_PER_STEP, tt), lambda i, j: (0, j, i))
    col_spec = pl.BlockSpec((PEER_HEADS, PEER_NKEYS, tt), lambda i, j: (0, 0, i))
    tok_spec = pl.BlockSpec((tt, D_MODEL), lambda i, j: (i, 0))
    return pl.pallas_call(
        functools.partial(_peer_dense_kernel, final_norm=final_gain is not None),
        grid=(n // tt, PEER_STEPS),
        in_specs=[
            tok_spec,
            pl.BlockSpec((PEER_CHUNK, D_MODEL), lambda i, j: (j, 0)),
            pl.BlockSpec((D_MODEL, PEER_CHUNK), lambda i, j: (0, j)),
            row_spec, row_spec, col_spec, col_spec,
            tok_spec,
            _mod_spec(mod, tokens_per_cond, tt),
            pl.BlockSpec((1, D_MODEL), lambda i, j: (0, 0)),
        ],
        out_specs=tok_spec,
        out_shape=jax.ShapeDtypeStruct((n, D_MODEL), F32),
        scratch_shapes=[pltpu.VMEM((D_MODEL, tt), BF16), pltpu.VMEM((D_MODEL, tt), F32),
                        pltpu.VMEM((2, PEER_SUB_ROWS * PEER_NKEYS, tt), BF16),
                        pltpu.VMEM((2, PEER_SUB_ROWS * PEER_NKEYS, tt), F32)],
        compiler_params=pltpu.CompilerParams(dimension_semantics=("arbitrary", "arbitrary"),
                                             vmem_limit_bytes=VMEM_LIMIT_BYTES),
        name="peer_dense",
    )(h2, u_bf, vt_bf, thr, e1, s2m, e2, xres, mod, fin)


def _peer_residual(x, h2, mod, lp, tokens_per_cond, final_gain):
    sel = _peer_select(h2, lp["peer_wq_bf"], lp["peer_keys_bf"])
    return _peer_dense(h2, lp["peer_u_bf"], lp["peer_vt_bf"], sel, x, mod, tokens_per_cond, final_gain)


def _rms_norm(x, g):
    xf = x.astype(F32)
    y = xf * lax.rsqrt(jnp.mean(xf * xf, axis=-1, keepdims=True) + NORM_EPS)
    return (y * g.astype(F32)).astype(x.dtype)


def _l2_norm(x):
    return x * lax.rsqrt(jnp.sum(x * x, axis=-1, keepdims=True) + NORM_EPS)


def _axial_rope_tables(T):
    t = np.arange(T)
    row, col = t // GRID_W, t % GRID_W
    n_freq = HEAD_DIM // 4
    inv = ROPE_THETA ** (-2.0 * np.arange(n_freq) / (HEAD_DIM // 2))
    ang = np.concatenate([row[:, None] * inv[None], col[:, None] * inv[None]], axis=1)
    return jnp.asarray(np.cos(ang), F32), jnp.asarray(np.sin(ang), F32)


def _apply_rope(x, cos, sin):
    xf = x.astype(F32)
    half = HEAD_DIM // 2
    x1, x2 = xf[..., :half], xf[..., half:]
    c, s = cos[None, :, None, :], sin[None, :, None, :]
    return jnp.concatenate([x1 * c - x2 * s, x2 * c + x1 * s], axis=-1).astype(x.dtype)


def _attention_kernel(q_ref, k_ref, v_ref, o_ref, *, group):
    outs = []
    for h in range(q_ref.shape[2] // HEAD_DIM):
        q = q_ref[0, :, h * HEAD_DIM:(h + 1) * HEAD_DIM].astype(BF16)
        s = lax.dot_general(q, k_ref[0, h // group], (((1,), (1,)), ((), ())), preferred_element_type=F32) * ATTN_SCALE
        m = jnp.max(s, axis=-1, keepdims=True)
        p = jnp.exp(s - m)
        l = jnp.sum(p, axis=-1, keepdims=True)
        outs.append(jnp.dot(p.astype(BF16), v_ref[0, h // group], preferred_element_type=F32) / l)
    o_ref[0] = jnp.concatenate(outs, axis=1)


def _attention(q, k, v):
    B, T, width = q.shape
    hkv, L, dh = k.shape[1], k.shape[2], k.shape[3]
    group = width // dh // hkv
    tq = min(T, ATTN_Q_TILE)
    kv_spec = pl.BlockSpec((1, hkv, L, dh), lambda b, i: (b, 0, 0, 0))
    q_spec = pl.BlockSpec((1, tq, width), lambda b, i: (b, i, 0))
    return pl.pallas_call(
        functools.partial(_attention_kernel, group=group),
        grid=(B, T // tq),
        in_specs=[q_spec, kv_spec, kv_spec],
        out_specs=q_spec,
        out_shape=jax.ShapeDtypeStruct((B, T, width), F32),
        compiler_params=pltpu.CompilerParams(dimension_semantics=("arbitrary",) * 2,
                                             vmem_limit_bytes=VMEM_LIMIT_BYTES),
        name="attention",
    )(q, k.astype(BF16), v.astype(BF16))


def _na_bias_table(bias):
    qcols = np.arange(GRID_W)
    kcols = np.arange(GRID_W)
    cstart = np.clip(qcols - NA_COLS // 2, 0, GRID_W - NA_COLS)
    valid = (kcols[None, :] >= cstart[:, None]) & (kcols[None, :] < cstart[:, None] + NA_COLS)
    dc_idx = np.clip(kcols[None, :] - qcols[:, None], -(NA_COLS - 1), NA_COLS - 1) + (NA_COLS - 1)
    onehot = jnp.asarray(dc_idx[:, :, None] == np.arange(2 * NA_COLS - 1), F32)
    rel = jnp.einsum("qkc,hrc->hrqk", onehot, bias.astype(F32), precision=lax.Precision.HIGHEST)
    rel = jnp.where(valid[None, None], rel, NEG_INF)
    return jnp.stack([jnp.concatenate([rel[:, d0 + j] for j in range(NA_ROWS)], axis=-1)
                      for d0 in range(NA_ROWS)], axis=1)


def _na_kernel(q_ref, k_ref, v_ref, kc_ref, vc_ref, bias_ref, o_ref):
    blk = pl.program_id(1)
    n_rows = k_ref.shape[2] // GRID_W
    win = NA_ROWS * GRID_W
    nt = (((1,), (1,)), ((), ()))
    rows = []
    for i in range(NA_ROW_BLOCK):
        r = blk * NA_ROW_BLOCK + i
        start = jnp.clip(r - NA_ROWS // 2, 0, n_rows - NA_ROWS)
        rows.append((pl.multiple_of(start * GRID_W, GRID_W), start - r + (NA_ROWS - 1)))
    outs = []
    for h in range(NA_HEADS):
        q = _Par([q_ref[0, i * GRID_W:(i + 1) * GRID_W, h * HEAD_DIM:(h + 1) * HEAD_DIM].astype(BF16)
                  for i in range(NA_ROW_BLOCK)])
        k = _Par([k_ref[0, h, pl.ds(off, win), :] for off, _ in rows])
        v = _Par([v_ref[0, h, pl.ds(off, win), :] for off, _ in rows])
        bias = _Par([bias_ref[h, d0] for _, d0 in rows])
        s_loc = _lift(lambda q, k, b: lax.dot_general(q, k, nt, preferred_element_type=F32) * ATTN_SCALE + b,
                      q, k, bias)
        s_ctx = _lift(lambda q: lax.dot_general(q, kc_ref[0, h], nt, preferred_element_type=F32) * ATTN_SCALE, q)
        m = _lift(lambda a, b: jnp.maximum(jnp.max(a, axis=-1, keepdims=True), jnp.max(b, axis=-1, keepdims=True)),
                  s_loc, s_ctx)
        p_loc = _exp(s_loc - m)
        p_ctx = _exp(s_ctx - m)
        l = _lift(lambda a, b: jnp.sum(a, axis=-1, keepdims=True) + jnp.sum(b, axis=-1, keepdims=True), p_loc, p_ctx)
        o = _lift(lambda pl_, v, pc, l: (jnp.dot(pl_.astype(BF16), v, preferred_element_type=F32)
                                         + jnp.dot(pc.astype(BF16), vc_ref[0, h], preferred_element_type=F32)) / l,
                  p_loc, v, p_ctx, l)
        outs.append(jnp.concatenate(o.xs, axis=0))
    o_ref[0] = jnp.concatenate(outs, axis=1)


def _neighbourhood_attention(q, k, v, bias, k_ctx, v_ctx):
    B, H, T, dh = k.shape
    P = k_ctx.shape[2]
    rows = T // GRID_W
    assert rows >= NA_ROWS and rows % NA_ROW_BLOCK == 0
    tq = NA_ROW_BLOCK * GRID_W
    full = lambda n: pl.BlockSpec((1, H, n, dh), lambda b, i: (b, 0, 0, 0))
    q_spec = pl.BlockSpec((1, tq, H * dh), lambda b, i: (b, i, 0))
    return pl.pallas_call(
        _na_kernel,
        grid=(B, rows // NA_ROW_BLOCK),
        in_specs=[q_spec, full(T), full(T), full(P), full(P),
                  pl.BlockSpec((H, NA_ROWS, GRID_W, NA_ROWS * GRID_W), lambda b, i: (0, 0, 0, 0))],
        out_specs=q_spec,
        out_shape=jax.ShapeDtypeStruct((B, T, H * dh), F32),
        compiler_params=pltpu.CompilerParams(dimension_semantics=("arbitrary",) * 2,
                                             vmem_limit_bytes=VMEM_LIMIT_BYTES),
        name="na_attention",
    )(q, k.astype(BF16), v.astype(BF16), k_ctx.astype(BF16), v_ctx.astype(BF16), _na_bias_table(bias))


def _gqa_mixer(p, lp, ctx_kv):
    B, T, _ = p.shape
    kvw = GQA_KV_HEADS * HEAD_DIM
    q = _rms_norm(p[..., :GROUP_W].reshape(B, T, GQA_HEADS, HEAD_DIM), lp["gqa_q_norm"])
    k = _rms_norm(p[..., GROUP_W:GROUP_W + kvw].reshape(B, T, GQA_KV_HEADS, HEAD_DIM), lp["gqa_k_norm"])
    v = p[..., GROUP_W + kvw:].reshape(B, T, GQA_KV_HEADS, HEAD_DIM)
    if ctx_kv is not None:
        cos, sin = _axial_rope_tables(T)
        q = _apply_rope(q, cos, sin)
        k = _apply_rope(k, cos, sin)
    qf = q.reshape(B, T, GROUP_W)
    kh = k.transpose(0, 2, 1, 3)
    vh = v.transpose(0, 2, 1, 3)
    if ctx_kv is None:
        o = _attention(qf, kh, vh)
    else:
        k_ctx, v_ctx = ctx_kv
        o = _attention(qf, jnp.concatenate([kh, k_ctx], axis=2), jnp.concatenate([vh, v_ctx], axis=2))
    return o, (kh, vh)


def _na_mixer(p, lp, ctx_kv):
    B, T, _ = p.shape
    q = p[..., :GROUP_W]
    k, v = [p[..., i * GROUP_W:(i + 1) * GROUP_W].reshape(B, T, NA_HEADS, HEAD_DIM).transpose(0, 2, 1, 3)
            for i in (1, 2)]
    if ctx_kv is None:
        o = _attention(q, k, v)
    else:
        o = _neighbourhood_attention(q, k, v, lp["na_bias"], ctx_kv[0], ctx_kv[1])
    return o, (k, v)


def _mm(a, b, dims, two_pass):
    f = lambda x, y: lax.dot_general(x, y, (dims, ((), ())), preferred_element_type=F32)

    def one(a, b):
        a16 = a.astype(BF16)
        hi = b.astype(BF16)
        if not two_pass:
            return f(a16, hi)
        return f(a16, hi) + f(a16, (b - hi.astype(F32)).astype(BF16))

    return _lift(one, a, b)


def _dot(a, b, two_pass=False):
    return _mm(a, b, ((1,), (0,)), two_pass)


def _dot_nt(a, b, two_pass=False):
    return _mm(a, b, ((1,), (1,)), two_pass)


def _dot_tn(a, b):
    return _mm(a, b, ((0,), (0,)), False)


def _chunk_masks(reverse):
    row = lax.broadcasted_iota(jnp.int32, (SCAN_CHUNK, SCAN_CHUNK), 0)
    col = lax.broadcasted_iota(jnp.int32, (SCAN_CHUNK, SCAN_CHUNK), 1)
    earlier = (row < col) if reverse else (row > col)
    eye = row == col
    blk16 = (row >> 4) == (col >> 4)
    blk32 = (row >> 5) == (col >> 5)
    return dict(
        strict=earlier.astype(F32), incl=(earlier | eye).astype(F32), eye=eye.astype(F32),
        d16=blk16.astype(F32), off32=(blk32 & jnp.logical_not(blk16)).astype(F32),
        off64=jnp.logical_not(blk32).astype(F32))


def _scan_chains(heads):
    fwd, bwd = _chunk_masks(False), _chunk_masks(True)
    chains = [(b, d, h) for b in range(SCAN_ROWS) for d in range(2) for h in range(heads)]
    masks = {key: _Par([(bwd if d else fwd)[key] for _, d, _ in chains]) for key in fwd}
    return chains, [d == 1 for _, d, _ in chains], masks


def _last_row(cum, revs):
    return _Par([c[0:1] if rev else c[SCAN_CHUNK - 1:SCAN_CHUNK] for c, rev in zip(cum.xs, revs)])


def _unit_triangular_inverse(n, m):
    eye = m["eye"]
    nd = n * m["d16"]
    n2 = _dot(nd, nd)
    x = eye + nd
    x = x + _dot(x, n2)
    n4 = _dot(n2, n2)
    x = x + _dot(x, n4)
    n8 = _dot(n4, n4)
    x = x + _dot(x, n8)
    x = x + _dot(x, _dot(n * m["off32"], x))
    x = x + _dot(x, _dot(n * m["off64"], x))
    return x


def _rwkv_chunk(r, lw, k, v, a, b, h0, m, revs):
    cum = _dot(m["incl"], lw, two_pass=True)
    last = _last_row(cum, revs)
    g_inv = _exp(-cum)
    at = a * _exp(cum - lw)
    rt = r * _exp(cum)
    bt = b * g_inv
    kt = k * g_inv
    to_end = _exp(last - cum)
    bh = b * to_end
    kh = k * to_end
    ar = _stack_rows(at, rt)
    gb = _dot_nt(ar, bt)
    gk = _dot_nt(ar, kt)
    n_ab = gb[:SCAN_CHUNK] * m["strict"]
    l_ak = gk[:SCAN_CHUNK] * m["strict"]
    m_rb = gb[SCAN_CHUNK:] * m["incl"]
    m_rk = gk[SCAN_CHUNK:] * m["incl"]
    tinv = _unit_triangular_inverse(n_ab, m)
    p1 = _dot(tinv, at)
    u0 = _dot(tinv, _dot(l_ak, v))
    p2 = rt + _dot(m_rb, p1)
    o0 = _dot(m_rb, u0) + _dot(m_rk, v)
    a_c = m["eye"] * _exp(last) + _dot_tn(bh, p1)
    g_c = _dot_tn(bh, u0) + _dot_tn(kh, v)
    return _dot(p2, h0) + o0, _dot(a_c, h0) + g_c


def _rwkv_scan_kernel(*refs):
    ins, h0_ref = refs[:12], refs[12]
    o_refs, hout_ref, h_ref = refs[13:15], refs[15], refs[16]
    c = pl.program_id(1)

    @pl.when(c == 0)
    def _():
        h_ref[...] = h0_ref[...]

    chains, revs, m = _scan_chains(RW_HEADS)
    seqs = [_Par([_head_tile(ins[6 * d + i], b, h) for b, d, h in chains]) for i in range(6)]
    o, h_new = _rwkv_chunk(*seqs, _Par([h_ref[b, d, h] for b, d, h in chains]), m, revs)
    _store_heads(o_refs, o, chains)
    for i, (b, d, h) in enumerate(chains):
        h_ref[b, d, h] = h_new.xs[i]

    @pl.when(c == pl.num_programs(1) - 1)
    def _():
        hout_ref[...] = h_ref[...]


def _head_tile(ref, b, h):
    return ref[b, :, h * HEAD_DIM:(h + 1) * HEAD_DIM]


def _store_heads(o_refs, o, chains):
    for d in range(2):
        for b in range(SCAN_ROWS):
            tiles = [o.xs[i] for i, (bi, di, _) in enumerate(chains) if (bi, di) == (b, d)]
            o_refs[d][b] = jnp.concatenate(tiles, axis=1)


def _scan_specs(B, H, T, dh):
    n = T // SCAN_CHUNK
    fwd = pl.BlockSpec((SCAN_ROWS, SCAN_CHUNK, H * dh), lambda b, c: (b, c, 0))
    bwd = pl.BlockSpec((SCAN_ROWS, SCAN_CHUNK, H * dh), lambda b, c: (b, n - 1 - c, 0))
    state = pl.BlockSpec((SCAN_ROWS, 2, H, dh, dh), lambda b, c: (b, 0, 0, 0, 0))
    return (B // SCAN_ROWS, n), fwd, bwd, state


def _rwkv_scan_pallas(seqs, h0):
    B, T, _ = seqs[0][0].shape
    H, dh = RW_HEADS, HEAD_DIM
    grid, fwd, bwd, state_spec = _scan_specs(B, H, T, dh)
    seq = jax.ShapeDtypeStruct((B, T, H * dh), F32)
    return pl.pallas_call(
        _rwkv_scan_kernel,
        grid=grid,
        in_specs=[fwd] * 6 + [bwd] * 6 + [state_spec],
        out_specs=[fwd, bwd, state_spec],
        out_shape=[seq, seq, jax.ShapeDtypeStruct((B, 2, H, dh, dh), F32)],
        scratch_shapes=[pltpu.VMEM((SCAN_ROWS, 2, H, dh, dh), F32)],
        compiler_params=pltpu.CompilerParams(dimension_semantics=("arbitrary", "arbitrary"),
                                             vmem_limit_bytes=VMEM_LIMIT_BYTES),
        name="rwkv_scan",
    )(*seqs[0], *seqs[1], h0)


def _rwkv_mixer(p, lp, S0):
    B, T, _ = p.shape
    W, H = GROUP_W, RW_HEADS
    o0 = 3 * W
    o1 = o0 + 2 * RW_DECAY_RANK
    o2 = o1 + 2 * RW_AAA_RANK
    rkv = p[..., :o0]
    wd = p[..., o0:o1].reshape(B, T, 2, RW_DECAY_RANK)
    ad = p[..., o1:o2].reshape(B, T, 2, RW_AAA_RANK)
    gate = jax.nn.sigmoid(p[..., o2:]) @ lp["rw_g2"]
    heads = lambda a: a.astype(F32).reshape(B, T, H, HEAD_DIM)
    flat = lambda a: a.reshape(B, T, W)
    seqs, bonuses = [], []
    for d in range(2):
        base = jnp.concatenate([rkv, wd[:, :, d], ad[:, :, d]], axis=-1)
        if d == 0:
            shifted = jnp.pad(base[:, :-1], ((0, 0), (1, 0), (0, 0)))
        else:
            shifted = jnp.pad(base[:, 1:], ((0, 0), (0, 1), (0, 0)))
        xd = base + (shifted - base) * lp["rw_mu"][d]
        r, k, v = xd[..., :W], xd[..., W:2 * W], xd[..., 2 * W:3 * W]
        wl, al = xd[..., 3 * W:3 * W + RW_DECAY_RANK], xd[..., 3 * W + RW_DECAY_RANK:]
        w = -jax.nn.softplus(-(lp["rw_w0"][d] + jnp.tanh(wl) @ lp["rw_w2"][d])) - 0.5
        log_decay = -jnp.exp(w.astype(F32))
        a = jax.nn.sigmoid((lp["rw_a0"][d] + al @ lp["rw_a2"][d]).astype(F32))
        kf = k.astype(F32)
        kk = _l2_norm(heads(kf * lp["rw_k_k"].astype(F32)))
        k_eff = heads(kf * (1.0 + (a - 1.0) * lp["rw_k_a"].astype(F32)))
        rh, vh, ah = heads(r), heads(v), heads(a)
        seqs.append([r.astype(F32), log_decay, flat(k_eff), v.astype(F32), flat(-kk), flat(kk * ah)])
        bonuses.append(jnp.sum(rh * k_eff * lp["rw_r_k"].astype(F32), axis=-1, keepdims=True) * vh)
    o_f, o_b, h_fin = _rwkv_scan_pallas(seqs, jnp.swapaxes(S0.astype(F32), -1, -2))
    o = heads(o_f + o_b)
    finals = jnp.swapaxes(h_fin, -1, -2)
    mu = jnp.mean(o, axis=-1, keepdims=True)
    var = jnp.mean(jnp.square(o - mu), axis=-1, keepdims=True)
    y = ((o - mu) * lax.rsqrt(var + RW_LN_EPS)).reshape(B, T, W) * lp["rw_ln_g"].astype(F32) + lp["rw_ln_b"].astype(F32)
    y = y + (bonuses[0] + bonuses[1]).reshape(B, T, W)
    return y.astype(p.dtype) * gate, finals


def _delta_chunk(q, k, v, gb, betab, s0, m, revs):
    gc = _dot(m["incl"], gb, two_pass=True)
    last = _last_row(gc, revs)
    diff = gc - _dot_nt(m["eye"], gc, two_pass=True)
    dmat = _lift(lambda keep, d: jnp.where(keep > 0.0, jnp.exp(jnp.where(keep > 0.0, d, 0.0)), 0.0), m["incl"], diff)
    kb = k * betab
    gram = _dot_nt(_stack_rows(kb, q), k)
    a_mat = gram[:SCAN_CHUNK] * dmat * m["strict"]
    qk = gram[SCAN_CHUNK:] * dmat
    tm = _unit_triangular_inverse(-a_mat, m)
    u = _dot(tm, v * betab)
    w = _dot(tm, kb * _exp(gc))
    v_new = u - _dot(w, s0)
    o = _dot(q * _exp(gc), s0) + _dot(qk, v_new)
    return o, s0 * _exp(last) + _dot_tn(k * _exp(last - gc), v_new)


def _delta_scan_kernel(*refs):
    ins, s0_ref = refs[:10], refs[10]
    o_refs, sout_ref, s_ref = refs[11:13], refs[13], refs[14]
    c = pl.program_id(1)

    @pl.when(c == 0)
    def _():
        s_ref[...] = s0_ref[...]

    chains, revs, m = _scan_chains(DN_HEADS)
    seqs = [_Par([_head_tile(ins[5 * d + i], b, h) for b, d, h in chains]) for i in range(5)]
    o, s_new = _delta_chunk(*seqs, _Par([s_ref[b, d, h] for b, d, h in chains]), m, revs)
    _store_heads(o_refs, o, chains)
    for i, (b, d, h) in enumerate(chains):
        s_ref[b, d, h] = s_new.xs[i]

    @pl.when(c == pl.num_programs(1) - 1)
    def _():
        sout_ref[...] = s_ref[...]


def _delta_scan_pallas(q, k, v, g, beta, s0):
    assert SCAN_CHUNK == HEAD_DIM
    B, T, _ = q.shape
    H, dh = DN_HEADS, HEAD_DIM
    grid, fwd, bwd, state_spec = _scan_specs(B, H, T, dh)
    seq = jax.ShapeDtypeStruct((B, T, H * dh), F32)
    return pl.pallas_call(
        _delta_scan_kernel,
        grid=grid,
        in_specs=[fwd] * 5 + [bwd] * 5 + [state_spec],
        out_specs=[fwd, bwd, state_spec],
        out_shape=[seq, seq, jax.ShapeDtypeStruct((B, 2, H, dh, dh), F32)],
        scratch_shapes=[pltpu.VMEM((SCAN_ROWS, 2, H, dh, dh), F32)],
        compiler_params=pltpu.CompilerParams(dimension_semantics=("arbitrary", "arbitrary"),
                                             vmem_limit_bytes=VMEM_LIMIT_BYTES),
        name="delta_scan",
    )(q, k, v, g[0], beta[0], q, k, v, g[1], beta[1], s0)


def _deltanet_mixer(p, lp, S0):
    B, T, _ = p.shape
    W, H = GROUP_W, DN_HEADS
    xin = jnp.pad(p[..., :3 * W], ((0, 0), (DN_CONV // 2, DN_CONV // 2), (0, 0)))
    qkv = sum(xin[:, j:j + T] * lp["dn_conv"][j] for j in range(DN_CONV))
    qkv = jax.nn.silu(qkv)
    heads = lambda a: a.astype(F32).reshape(B, T, H, HEAD_DIM)
    flat = lambda a: a.reshape(B, T, W)
    q = _l2_norm(heads(qkv[..., :W])) * (HEAD_DIM ** -0.5)
    k = _l2_norm(heads(qkv[..., W:2 * W]))
    v = heads(qkv[..., 2 * W:])
    beta = jax.nn.sigmoid(p[..., 3 * W:3 * W + 2 * H].astype(F32)).reshape(B, T, 2, H)
    alpha = p[..., 3 * W + 2 * H:3 * W + 4 * H].astype(F32).reshape(B, T, 2, H)
    g = -jnp.exp(lp["dn_a_log"].astype(F32)) * jax.nn.softplus(alpha + lp["dn_dt_bias"].astype(F32))
    z = heads(p[..., 3 * W + 4 * H:])
    lanes = lambda a: [jnp.repeat(a[:, :, d], HEAD_DIM, axis=-1) for d in range(2)]
    o_f, o_b, s_fin = _delta_scan_pallas(flat(q), flat(k), flat(v), lanes(g), lanes(beta), S0.astype(F32))
    o = _rms_norm(heads(o_f + o_b), lp["dn_norm_g"]) * jax.nn.silu(z)
    return o.reshape(B, T, W).astype(p.dtype), s_fin


def _modulated_norm(x, gain, scale, shift):
    y = x * lax.rsqrt(jnp.mean(x * x, axis=-1, keepdims=True) + NORM_EPS)
    return (y * gain) * (1.0 + scale) + shift


def _proj_in_kernel(x_ref, mod_ref, gain_ref, w_ref, o_ref):
    h = _modulated_norm(x_ref[...], gain_ref[...], mod_ref[0, MOD_SCALE1:MOD_SCALE1 + 1], mod_ref[0, MOD_SHIFT1:MOD_SHIFT1 + 1])
    o_ref[...] = jnp.dot(h.astype(BF16), w_ref[...], preferred_element_type=F32)


def _proj_out_kernel(mix_ref, x_ref, mod_ref, gain_ref, w_ref, x_out_ref, h2_ref):
    y = jnp.dot(mix_ref[...].astype(BF16), w_ref[...], preferred_element_type=F32)
    x_new = x_ref[...] + mod_ref[0, MOD_GATE1:MOD_GATE1 + 1] * y
    x_out_ref[...] = x_new
    h2_ref[...] = _modulated_norm(x_new, gain_ref[...], mod_ref[0, MOD_SCALE2:MOD_SCALE2 + 1],
                                  mod_ref[0, MOD_SHIFT2:MOD_SHIFT2 + 1])


def _proj_in(x, mod, gain, w_bf, tokens_per_cond):
    n, cols = x.shape[0], w_bf.shape[1]
    tm = PROJ_TOKENS
    tok = pl.BlockSpec((tm, D_MODEL), lambda i: (i, 0))
    return pl.pallas_call(
        _proj_in_kernel,
        grid=(n // tm,),
        in_specs=[tok, _mod_spec(mod, tokens_per_cond, tm), pl.BlockSpec((1, D_MODEL), lambda i: (0, 0)),
                  pl.BlockSpec((D_MODEL, cols), lambda i: (0, 0))],
        out_specs=pl.BlockSpec((tm, cols), lambda i: (i, 0)),
        out_shape=jax.ShapeDtypeStruct((n, cols), F32),
        compiler_params=pltpu.CompilerParams(dimension_semantics=("arbitrary",), vmem_limit_bytes=VMEM_LIMIT_BYTES),
        name="proj_in",
    )(x, mod, gain, w_bf)


def _proj_out(mix, x, mod, gain, w_bf, tokens_per_cond):
    n = x.shape[0]
    tm = PROJ_TOKENS
    tok = pl.BlockSpec((tm, D_MODEL), lambda i: (i, 0))
    out = jax.ShapeDtypeStruct((n, D_MODEL), F32)
    return pl.pallas_call(
        _proj_out_kernel,
        grid=(n // tm,),
        in_specs=[tok, tok, _mod_spec(mod, tokens_per_cond, tm), pl.BlockSpec((1, D_MODEL), lambda i: (0, 0)),
                  pl.BlockSpec((D_MODEL, D_MODEL), lambda i: (0, 0))],
        out_specs=[tok, tok],
        out_shape=[out, out],
        compiler_params=pltpu.CompilerParams(dimension_semantics=("arbitrary",), vmem_limit_bytes=VMEM_LIMIT_BYTES),
        name="proj_out",
    )(mix, x, mod, gain, w_bf)


def _trunk_layer(x, cond, lp, ctx, final_gain):
    B, T, _ = x.shape
    mod = (jax.nn.silu(cond) @ lp["w_mod"] + lp["b_mod"]).reshape(-1, 6, D_MODEL)
    x = x.reshape(B * T, D_MODEL)
    proj = _proj_in(x, mod, lp["norm1_g"][None], lp["w_in_bf"], T).reshape(B, T, -1)
    pa, pb, pc, pd, _ = jnp.split(proj, [GQA_PROJ, GQA_PROJ + RW_PROJ, GQA_PROJ + RW_PROJ + NA_PROJ, IN_COLS], axis=-1)
    if ctx is None:
        ctx_a = None
        ctx_c = None
        s_rw0 = jnp.zeros((B, 2, RW_HEADS, HEAD_DIM, HEAD_DIM), F32)
        s_dn0 = jnp.zeros((B, 2, DN_HEADS, HEAD_DIM, HEAD_DIM), F32)
    else:
        ka_c, va_c, kc_c, vc_c, s_rw0, s_dn0 = ctx
        ctx_a = (ka_c, va_c)
        ctx_c = (kc_c, vc_c)
    oa, (ka, va) = _gqa_mixer(pa, lp, ctx_a)
    ob, s_rw = _rwkv_mixer(pb, lp, s_rw0)
    oc, (kc, vc) = _na_mixer(pc, lp, ctx_c)
    od, s_dn = _deltanet_mixer(pd, lp, s_dn0)
    mix = jnp.concatenate([oa, ob, oc, od], axis=-1).reshape(B * T, D_MODEL)
    x, h2 = _proj_out(mix, x, mod, lp["norm2_g"][None], lp["w_out_bf"], T)
    x = _peer_residual(x, h2, mod, lp, T, final_gain)
    return x.reshape(B, T, D_MODEL), (ka, va, kc, vc, s_rw, s_dn)


def kernel(x_prompt, x_sample, c, cache_gqa_k, cache_gqa_v, cache_na_k, cache_na_v, state_rwkv, state_delta, c_ctx, norm1_g, norm2_g, w_mod, b_mod, w_in, w_out, gqa_q_norm, gqa_k_norm, rw_mu, rw_w0, rw_w2, rw_a0, rw_a2, rw_g2, rw_k_k, rw_k_a, rw_r_k, rw_ln_g, rw_ln_b, na_bias, dn_conv, dn_a_log, dn_dt_bias, dn_norm_g, peer_wq, peer_keys, peer_u, peer_v, final_norm_g):
    depth = w_in.shape[0]

    def layer_params(l):
        return {
            "norm1_g": norm1_g[l], "norm2_g": norm2_g[l], "w_mod": w_mod[l], "b_mod": b_mod[l],
            "w_in_bf": jnp.pad(w_in[l].astype(BF16), ((0, 0), (0, IN_COLS_PADDED - IN_COLS))),
            "w_out_bf": w_out[l].astype(BF16), "gqa_q_norm": gqa_q_norm[l], "gqa_k_norm": gqa_k_norm[l],
            "rw_mu": rw_mu[l], "rw_w0": rw_w0[l], "rw_w2": rw_w2[l], "rw_a0": rw_a0[l], "rw_a2": rw_a2[l],
            "rw_g2": rw_g2[l], "rw_k_k": rw_k_k[l], "rw_k_a": rw_k_a[l], "rw_r_k": rw_r_k[l],
            "rw_ln_g": rw_ln_g[l], "rw_ln_b": rw_ln_b[l], "na_bias": na_bias[l],
            "dn_conv": dn_conv[l], "dn_a_log": dn_a_log[l], "dn_dt_bias": dn_dt_bias[l], "dn_norm_g": dn_norm_g[l],
            "peer_wq_bf": peer_wq[l].astype(BF16),
            "peer_keys_bf": peer_keys[l].reshape(2 * PEER_HEADS, PEER_NKEYS, PEER_DQ // 2).astype(BF16),
            "peer_u_bf": peer_u[l].astype(BF16),
            "peer_vt_bf": peer_v[l].astype(BF16).T,
        }

    params = [layer_params(l) for l in range(depth)]

    xp = x_prompt
    cond_ctx = c_ctx[None, :]
    per_layer = []
    closing = lambda l: final_norm_g if l == depth - 1 else None
    for l in range(depth):
        xp, st = _trunk_layer(xp, cond_ctx, params[l], None, closing(l))
        per_layer.append(st)
    y_prompt = xp

    xs = x_sample
    for l in range(depth):
        ctx = (cache_gqa_k[:, l], cache_gqa_v[:, l], cache_na_k[:, l], cache_na_v[:, l],
               state_rwkv[:, l], state_delta[:, l])
        xs, _ = _trunk_layer(xs, c, params[l], ctx, closing(l))
    y_sample = xs

    dt = x_prompt.dtype
    new_gqa_k = jnp.stack([st[0] for st in per_layer], axis=1).astype(dt)
    new_gqa_v = jnp.stack([st[1] for st in per_layer], axis=1).astype(dt)
    new_na_k = jnp.stack([st[2] for st in per_layer], axis=1).astype(dt)
    new_na_v = jnp.stack([st[3] for st in per_layer], axis=1).astype(dt)
    new_state_rwkv = jnp.stack([st[4] for st in per_layer], axis=1).astype(dt)
    new_state_delta = jnp.stack([st[5] for st in per_layer], axis=1).astype(dt)
    return (y_prompt, y_sample, new_gqa_k, new_gqa_v, new_na_k, new_na_v, new_state_rwkv, new_state_delta)
```

```python
import functools
import math

import jax
import jax.numpy as jnp
import numpy as np
from jax import lax
from jax.experimental import pallas as pl
from jax.experimental.pallas import tpu as pltpu

F32 = jnp.float32
BF16 = jnp.bfloat16

D_MODEL = 1024
GRID_W = 64
HEAD_DIM = 64
GROUP_W = D_MODEL // 4
NORM_EPS = 1e-6
NEG_INF = -1e30
GQA_HEADS = GROUP_W // HEAD_DIM
GQA_KV_HEADS = GQA_HEADS // 2
ROPE_THETA = 10000.0
RW_HEADS = GROUP_W // HEAD_DIM
RW_DECAY_RANK = 32
RW_AAA_RANK = 32
RW_GATE_RANK = 64
RW_LN_EPS = 64e-5
NA_HEADS = GROUP_W // HEAD_DIM
NA_ROWS = 8
NA_COLS = 16
DN_HEADS = GROUP_W // HEAD_DIM
DN_CONV = 5
PEER_HEADS = 8
PEER_NKEYS = 128
PEER_EXPERTS = PEER_NKEYS * PEER_NKEYS
PEER_DQ = 256
PEER_TOPK = 16
GQA_PROJ = GROUP_W + 2 * GQA_KV_HEADS * HEAD_DIM
RW_PROJ = 3 * GROUP_W + 2 * RW_DECAY_RANK + 2 * RW_AAA_RANK + RW_GATE_RANK
NA_PROJ = 3 * GROUP_W
DN_PROJ = 3 * GROUP_W + 4 * DN_HEADS + GROUP_W

LANES = 128
VMEM_LIMIT_BYTES = 56 * 1024 * 1024

PEER_SEL_TOKENS = 128
PEER_TOKENS = 256
PEER_ROWS_PER_STEP = 16
PEER_SUB_ROWS = 2
PEER_HALF = 64
PEER_MATMUL_PIECE = 256
PEER_CHUNK = PEER_ROWS_PER_STEP * PEER_NKEYS
PEER_STEPS = PEER_EXPERTS // PEER_CHUNK
INV_SQRT2 = 1.0 / math.sqrt(2.0)

IN_COLS = GQA_PROJ + RW_PROJ + NA_PROJ + DN_PROJ
IN_COLS_PADDED = -(-IN_COLS // LANES) * LANES
MOD_SHIFT1, MOD_SCALE1, MOD_GATE1, MOD_SHIFT2, MOD_SCALE2, MOD_GATE2 = range(6)
PROJ_TOKENS = 512

ATTN_SCALE = HEAD_DIM ** -0.5
ATTN_Q_TILE = 256
NA_ROW_BLOCK = 8

SCAN_CHUNK = 64
SCAN_ROWS = 4


class _Par:
    def __init__(self, xs):
        self.xs = list(xs)

    def __add__(self, o):
        return _lift(jnp.add, self, o)

    __radd__ = __add__

    def __sub__(self, o):
        return _lift(jnp.subtract, self, o)

    def __rsub__(self, o):
        return _lift(lambda x, y: y - x, self, o)

    def __mul__(self, o):
        return _lift(jnp.multiply, self, o)

    __rmul__ = __mul__

    def __neg__(self):
        return _lift(jnp.negative, self)

    def __getitem__(self, idx):
        return _lift(lambda x: x[idx], self)


def _lift(f, *args):
    n = next(len(a.xs) for a in args if isinstance(a, _Par))
    return _Par([f(*[a.xs[i] if isinstance(a, _Par) else a for a in args]) for i in range(n)])


def _exp(x):
    return _lift(jnp.exp, x)


def _stack_rows(x, y):
    return _lift(lambda a, b: jnp.concatenate([a, b], axis=0), x, y)


def _top_k_rows(s, k, exact):
    rows = s.shape[0]
    iota = lax.broadcasted_iota(jnp.int32, s.shape, 0).astype(F32)
    vals = []
    work = s
    for _ in range(k):
        m = jnp.max(work, axis=0, keepdims=True)
        if exact:
            first = jnp.min(jnp.where(work == m, iota, float(rows)), axis=0, keepdims=True)
            work = jnp.where(iota == first, -jnp.inf, work)
        else:
            work = jnp.where(work == m, -jnp.inf, work)
        vals.append(m)
    if exact:
        return vals, work, jnp.zeros_like(vals[0])
    removed = jnp.sum(jnp.where(work == -jnp.inf, 1.0, 0.0), axis=0, keepdims=True)
    return vals, work, jnp.where(removed == float(k), 0.0, 1.0)


def _peer_select_head(q, keys_ref, h, thr_ref, e1_ref, s2m_ref, e2_ref, exact):
    half = PEER_DQ // 2
    scores, tops, sel = [], [], []
    suspect = None
    for p in range(2):
        c0 = (2 * h + p) * half
        qhp = q[:, c0:c0 + half].astype(BF16)
        s = lax.dot_general(keys_ref[2 * h + p], qhp, (((1,), (1,)), ((), ())),
                            preferred_element_type=F32)
        vals, work, flag = _top_k_rows(s, PEER_TOPK, exact)
        suspect = flag if suspect is None else suspect + flag
        scores.append(s)
        tops.append(vals)
        sel.append(work == -jnp.inf)
    m1, m2 = tops
    m2_all = jnp.concatenate(m2, axis=0)
    cand = jnp.concatenate([m1[0] + m2_all] + [m1[i] + m2_all[:8] for i in range(1, 8)]
                           + [jnp.concatenate(m1[8:], axis=0) + m2[0]], axis=0)
    top_s, _, flag = _top_k_rows(cand, PEER_TOPK, exact)
    suspect = suspect + flag
    mx = top_s[0]
    z = jnp.exp(top_s[0] - mx)
    for r in range(1, PEER_TOPK):
        z = z + jnp.exp(top_s[r] - mx)
    half_inv_z = 0.5 / z
    t3 = top_s[PEER_TOPK - 1]
    s1m = jnp.where(sel[0], scores[0], -jnp.inf)
    thr = jnp.full_like(s1m, jnp.inf)
    for j in range(PEER_TOPK // 2):
        thr = jnp.where(s1m + m2[j] >= t3, m2[j], thr)
    thr_best = jnp.full_like(t3, jnp.inf)
    for j in range(PEER_TOPK // 2, PEER_TOPK):
        thr_best = jnp.where(m1[0] + m2[j] >= t3, m2[j], thr_best)
    thr = jnp.where(s1m == m1[0], jnp.minimum(thr, thr_best), thr)
    thr_ref[h] = thr
    e1_ref[h] = jnp.where(sel[0], jnp.exp(scores[0] - m1[0]) * half_inv_z, 0.0)
    s2m_ref[h] = jnp.where(sel[1], scores[1], -jnp.inf)
    e2_ref[h] = jnp.exp(scores[1] - m2[0])
    return suspect


def _peer_select_kernel(h_ref, wq_ref, keys_ref, *out_refs):
    x = h_ref[...].astype(BF16)
    q = jnp.dot(x, wq_ref[...], preferred_element_type=F32)
    flags = [_peer_select_head(q, keys_ref, h, *out_refs, exact=False) for h in range(PEER_HEADS)]

    @pl.when(jnp.max(sum(flags)) > 0.0)
    def _():
        for h in range(PEER_HEADS):
            @pl.when(jnp.max(flags[h]) > 0.0)
            def _():
                _peer_select_head(q, keys_ref, h, *out_refs, exact=True)


def _peer_select(h2, wq_bf, keys_bf):
    n = h2.shape[0]
    tt = PEER_SEL_TOKENS
    big = jax.ShapeDtypeStruct((PEER_HEADS, PEER_NKEYS, n), F32)
    big_spec = pl.BlockSpec((PEER_HEADS, PEER_NKEYS, tt), lambda i: (0, 0, i))
    return pl.pallas_call(
        _peer_select_kernel,
        grid=(n // tt,),
        in_specs=[
            pl.BlockSpec((tt, D_MODEL), lambda i: (i, 0)),
            pl.BlockSpec((D_MODEL, PEER_HEADS * PEER_DQ), lambda i: (0, 0)),
            pl.BlockSpec((2 * PEER_HEADS, PEER_NKEYS, PEER_DQ // 2), lambda i: (0, 0, 0)),
        ],
        out_specs=[big_spec] * 4,
        out_shape=[big] * 4,
        compiler_params=pltpu.CompilerParams(dimension_semantics=("arbitrary",),
                                             vmem_limit_bytes=VMEM_LIMIT_BYTES),
        name="peer_select",
    )(h2, wq_bf, keys_bf)


def _peer_dense_kernel(h_ref, u_ref, vt_ref, thr_ref, e1_ref, s2m_ref, e2_ref, xres_ref, mod_ref, fin_ref,
                       out_ref, ht_ref, acc_ref, p_ref, act_ref, *, final_norm):
    j = pl.program_id(1)
    tt = h_ref.shape[0]

    @pl.when(j == 0)
    def _():
        ht_ref[...] = h_ref[...].T.astype(BF16)
        acc_ref[...] = jnp.zeros_like(acc_ref)

    sub = PEER_SUB_ROWS * PEER_NKEYS
    lane_blocks = tt // LANES
    halves = PEER_NKEYS // PEER_HALF
    n_block = lane_blocks * halves
    kp = PEER_MATMUL_PIECE
    n_piece = D_MODEL // kp
    assert n_block % n_piece == 0
    blocks_per_piece = n_block // n_piece

    def activation_piece(k, i):
        part = jnp.dot(u_ref[k * sub:(k + 1) * sub, i * kp:(i + 1) * kp], ht_ref[i * kp:(i + 1) * kp, :],
                       preferred_element_type=F32)
        if i == 0:
            act_ref[k % 2] = part
        else:
            act_ref[k % 2] += part

    def accumulate_piece(k, i):
        acc_ref[i * kp:(i + 1) * kp, :] += jnp.dot(vt_ref[i * kp:(i + 1) * kp, k * sub:(k + 1) * sub], p_ref[k % 2],
                                                   preferred_element_type=F32)

    def weighted_block(k, i):
        c, hb = divmod(i, halves)
        cs = slice(c * LANES, (c + 1) * LANES)
        bs = slice(hb * PEER_HALF, (hb + 1) * PEER_HALF)
        firsts = [k * PEER_SUB_ROWS + ai for ai in range(PEER_SUB_ROWS)]
        w = [None for _ in firsts]
        for h in range(PEER_HEADS):
            s2 = s2m_ref[h, bs, cs]
            e2 = e2_ref[h, bs, cs]
            for ai, a in enumerate(firsts):
                term = jnp.where(s2 >= thr_ref[h, a:a + 1, cs], e2, 0.0) * e1_ref[h, a:a + 1, cs]
                w[ai] = term if w[ai] is None else w[ai] + term
        for ai in range(PEER_SUB_ROWS):
            rows = slice(ai * PEER_NKEYS + hb * PEER_HALF, ai * PEER_NKEYS + (hb + 1) * PEER_HALF)
            zz = act_ref[k % 2, rows, cs]
            gelu2 = zz * (1.0 + lax.erf(zz * INV_SQRT2))
            p_ref[k % 2, rows, cs] = (w[ai] * gelu2).astype(BF16)

    n_sub = PEER_ROWS_PER_STEP // PEER_SUB_ROWS
    for i in range(n_piece):
        activation_piece(0, i)
    for k in range(n_sub):
        for blk in range(n_block):
            if blk % blocks_per_piece == 0:
                if k + 1 < n_sub:
                    activation_piece(k + 1, blk // blocks_per_piece)
                if k > 0:
                    accumulate_piece(k - 1, blk // blocks_per_piece)
            weighted_block(k, blk)
    for i in range(n_piece):
        accumulate_piece(n_sub - 1, i)

    @pl.when(j == PEER_STEPS - 1)
    def _():
        y = xres_ref[...] + mod_ref[0, MOD_GATE2:MOD_GATE2 + 1] * acc_ref[...].T
        if final_norm:
            y = y * lax.rsqrt(jnp.mean(y * y, axis=-1, keepdims=True) + NORM_EPS) * fin_ref[...]
        out_ref[...] = y


def _mod_spec(mod, tokens_per_cond, tile):
    tiles_per_cond = tokens_per_cond // tile
    if mod.shape[0] == 1:
        return pl.BlockSpec((1, 6, D_MODEL), lambda i, *_: (0, 0, 0))
    assert tokens_per_cond % tile == 0
    return pl.BlockSpec((1, 6, D_MODEL), lambda i, *_: (i // tiles_per_cond, 0, 0))


def _peer_dense(h2, u_bf, vt_bf, sel, xres, mod, tokens_per_cond, final_gain):
    n = h2.shape[0]
    tt = PEER_TOKENS
    thr, e1, s2m, e2 = sel
    fin = jnp.ones((1, D_MODEL), F32) if final_gain is None else final_gain.astype(F32)[None]
    row_spec = pl.BlockSpec((PEER_HEADS, PEER_ROWS_PER_STEP, tt), lambda i, j: (0, j, i))
    col_spec = pl.BlockSpec((PEER_HEADS, PEER_NKEYS, tt), lambda i, j: (0, 0, i))
    tok_spec = pl.BlockSpec((tt, D_MODEL), lambda i, j: (i, 0))
    return pl.pallas_call(
        functools.partial(_peer_dense_kernel, final_norm=final_gain is not None),
        grid=(n // tt, PEER_STEPS),
        in_specs=[
            tok_spec,
            pl.BlockSpec((PEER_CHUNK, D_MODEL), lambda i, j: (j, 0)),
            pl.BlockSpec((D_MODEL, PEER_CHUNK), lambda i, j: (0, j)),
            row_spec, row_spec, col_spec, col_spec,
            tok_spec,
            _mod_spec(mod, tokens_per_cond, tt),
            pl.BlockSpec((1, D_MODEL), lambda i, j: (0, 0)),
        ],
        out_specs=tok_spec,
        out_shape=jax.ShapeDtypeStruct((n, D_MODEL), F32),
        scratch_shapes=[pltpu.VMEM((D_MODEL, tt), BF16), pltpu.VMEM((D_MODEL, tt), F32),
                        pltpu.VMEM((2, PEER_SUB_ROWS * PEER_NKEYS, tt), BF16),
                        pltpu.VMEM((2, PEER_SUB_ROWS * PEER_NKEYS, tt), F32)],
        compiler_params=pltpu.CompilerParams(dimension_semantics=("arbitrary", "arbitrary"),
                                             vmem_limit_bytes=VMEM_LIMIT_BYTES),
        name="peer_dense",
    )(h2, u_bf, vt_bf, thr, e1, s2m, e2, xres, mod, fin)


def _peer_residual(x, h2, mod, lp, tokens_per_cond, final_gain):
    sel = _peer_select(h2, lp["peer_wq_bf"], lp["peer_keys_bf"])
    return _peer_dense(h2, lp["peer_u_bf"], lp["peer_vt_bf"], sel, x, mod, tokens_per_cond, final_gain)


def _rms_norm(x, g):
    xf = x.astype(F32)
    y = xf * lax.rsqrt(jnp.mean(xf * xf, axis=-1, keepdims=True) + NORM_EPS)
    return (y * g.astype(F32)).astype(x.dtype)


def _l2_norm(x):
    return x * lax.rsqrt(jnp.sum(x * x, axis=-1, keepdims=True) + NORM_EPS)


def _axial_rope_tables(T):
    t = np.arange(T)
    row, col = t // GRID_W, t % GRID_W
    n_freq = HEAD_DIM // 4
    inv = ROPE_THETA ** (-2.0 * np.arange(n_freq) / (HEAD_DIM // 2))
    ang = np.concatenate([row[:, None] * inv[None], col[:, None] * inv[None]], axis=1)
    return jnp.asarray(np.cos(ang), F32), jnp.asarray(np.sin(ang), F32)


def _apply_rope(x, cos, sin):
    xf = x.astype(F32)
    half = HEAD_DIM // 2
    x1, x2 = xf[..., :half], xf[..., half:]
    c, s = cos[None, :, None, :], sin[None, :, None, :]
    return jnp.concatenate([x1 * c - x2 * s, x2 * c + x1 * s], axis=-1).astype(x.dtype)


def _attention_kernel(q_ref, k_ref, v_ref, o_ref, *, group):
    outs = []
    for h in range(q_ref.shape[2] // HEAD_DIM):
        q = q_ref[0, :, h * HEAD_DIM:(h + 1) * HEAD_DIM].astype(BF16)
        s = lax.dot_general(q, k_ref[0, h // group], (((1,), (1,)), ((), ())), preferred_element_type=F32) * ATTN_SCALE
        m = jnp.max(s, axis=-1, keepdims=True)
        p = jnp.exp(s - m)
        l = jnp.sum(p, axis=-1, keepdims=True)
        outs.append(jnp.dot(p.astype(BF16), v_ref[0, h // group], preferred_element_type=F32) / l)
    o_ref[0] = jnp.concatenate(outs, axis=1)


def _attention(q, k, v):
    B, T, width = q.shape
    hkv, L, dh = k.shape[1], k.shape[2], k.shape[3]
    group = width // dh // hkv
    tq = min(T, ATTN_Q_TILE)
    kv_spec = pl.BlockSpec((1, hkv, L, dh), lambda b, i: (b, 0, 0, 0))
    q_spec = pl.BlockSpec((1, tq, width), lambda b, i: (b, i, 0))
    return pl.pallas_call(
        functools.partial(_attention_kernel, group=group),
        grid=(B, T // tq),
        in_specs=[q_spec, kv_spec, kv_spec],
        out_specs=q_spec,
        out_shape=jax.ShapeDtypeStruct((B, T, width), F32),
        compiler_params=pltpu.CompilerParams(dimension_semantics=("arbitrary",) * 2,
                                             vmem_limit_bytes=VMEM_LIMIT_BYTES),
        name="attention",
    )(q, k.astype(BF16), v.astype(BF16))


def _na_bias_table(bias):
    qcols = np.arange(GRID_W)
    kcols = np.arange(GRID_W)
    cstart = np.clip(qcols - NA_COLS // 2, 0, GRID_W - NA_COLS)
    valid = (kcols[None, :] >= cstart[:, None]) & (kcols[None, :] < cstart[:, None] + NA_COLS)
    dc_idx = np.clip(kcols[None, :] - qcols[:, None], -(NA_COLS - 1), NA_COLS - 1) + (NA_COLS - 1)
    onehot = jnp.asarray(dc_idx[:, :, None] == np.arange(2 * NA_COLS - 1), F32)
    rel = jnp.einsum("qkc,hrc->hrqk", onehot, bias.astype(F32), precision=lax.Precision.HIGHEST)
    rel = jnp.where(valid[None, None], rel, NEG_INF)
    return jnp.stack([jnp.concatenate([rel[:, d0 + j] for j in range(NA_ROWS)], axis=-1)
                      for d0 in range(NA_ROWS)], axis=1)


def _na_kernel(q_ref, k_ref, v_ref, kc_ref, vc_ref, bias_ref, o_ref):
    blk = pl.program_id(1)
    n_rows = k_ref.shape[2] // GRID_W
    win = NA_ROWS * GRID_W
    nt = (((1,), (1,)), ((), ()))
    rows = []
    for i in range(NA_ROW_BLOCK):
        r = blk * NA_ROW_BLOCK + i
        start = jnp.clip(r - NA_ROWS // 2, 0, n_rows - NA_ROWS)
        rows.append((pl.multiple_of(start * GRID_W, GRID_W), start - r + (NA_ROWS - 1)))
    outs = []
    for h in range(NA_HEADS):
        q = _Par([q_ref[0, i * GRID_W:(i + 1) * GRID_W, h * HEAD_DIM:(h + 1) * HEAD_DIM].astype(BF16)
                  for i in range(NA_ROW_BLOCK)])
        k = _Par([k_ref[0, h, pl.ds(off, win), :] for off, _ in rows])
        v = _Par([v_ref[0, h, pl.ds(off, win), :] for off, _ in rows])
        bias = _Par([bias_ref[h, d0] for _, d0 in rows])
        s_loc = _lift(lambda q, k, b: lax.dot_general(q, k, nt, preferred_element_type=F32) * ATTN_SCALE + b,
                      q, k, bias)
        s_ctx = _lift(lambda q: lax.dot_general(q, kc_ref[0, h], nt, preferred_element_type=F32) * ATTN_SCALE, q)
        m = _lift(lambda a, b: jnp.maximum(jnp.max(a, axis=-1, keepdims=True), jnp.max(b, axis=-1, keepdims=True)),
                  s_loc, s_ctx)
        p_loc = _exp(s_loc - m)
        p_ctx = _exp(s_ctx - m)
        l = _lift(lambda a, b: jnp.sum(a, axis=-1, keepdims=True) + jnp.sum(b, axis=-1, keepdims=True), p_loc, p_ctx)
        o = _lift(lambda pl_, v, pc, l: (jnp.dot(pl_.astype(BF16), v, preferred_element_type=F32)
                                         + jnp.dot(pc.astype(BF16), vc_ref[0, h], preferred_element_type=F32)) / l,
                  p_loc, v, p_ctx, l)
        outs.append(jnp.concatenate(o.xs, axis=0))
    o_ref[0] = jnp.concatenate(outs, axis=1)


def _neighbourhood_attention(q, k, v, bias, k_ctx, v_ctx):
    B, H, T, dh = k.shape
    P = k_ctx.shape[2]
    rows = T // GRID_W
    assert rows >= NA_ROWS and rows % NA_ROW_BLOCK == 0
    tq = NA_ROW_BLOCK * GRID_W
    full = lambda n: pl.BlockSpec((1, H, n, dh), lambda b, i: (b, 0, 0, 0))
    q_spec = pl.BlockSpec((1, tq, H * dh), lambda b, i: (b, i, 0))
    return pl.pallas_call(
        _na_kernel,
        grid=(B, rows // NA_ROW_BLOCK),
        in_specs=[q_spec, full(T), full(T), full(P), full(P),
                  pl.BlockSpec((H, NA_ROWS, GRID_W, NA_ROWS * GRID_W), lambda b, i: (0, 0, 0, 0))],
        out_specs=q_spec,
        out_shape=jax.ShapeDtypeStruct((B, T, H * dh), F32),
        compiler_params=pltpu.CompilerParams(dimension_semantics=("arbitrary",) * 2,
                                             vmem_limit_bytes=VMEM_LIMIT_BYTES),
        name="na_attention",
    )(q, k.astype(BF16), v.astype(BF16), k_ctx.astype(BF16), v_ctx.astype(BF16), _na_bias_table(bias))


def _gqa_mixer(p, lp, ctx_kv):
    B, T, _ = p.shape
    kvw = GQA_KV_HEADS * HEAD_DIM
    q = _rms_norm(p[..., :GROUP_W].reshape(B, T, GQA_HEADS, HEAD_DIM), lp["gqa_q_norm"])
    k = _rms_norm(p[..., GROUP_W:GROUP_W + kvw].reshape(B, T, GQA_KV_HEADS, HEAD_DIM), lp["gqa_k_norm"])
    v = p[..., GROUP_W + kvw:].reshape(B, T, GQA_KV_HEADS, HEAD_DIM)
    if ctx_kv is not None:
        cos, sin = _axial_rope_tables(T)
        q = _apply_rope(q, cos, sin)
        k = _apply_rope(k, cos, sin)
    qf = q.reshape(B, T, GROUP_W)
    kh = k.transpose(0, 2, 1, 3)
    vh = v.transpose(0, 2, 1, 3)
    if ctx_kv is None:
        o = _attention(qf, kh, vh)
    else:
        k_ctx, v_ctx = ctx_kv
        o = _attention(qf, jnp.concatenate([kh, k_ctx], axis=2), jnp.concatenate([vh, v_ctx], axis=2))
    return o, (kh, vh)


def _na_mixer(p, lp, ctx_kv):
    B, T, _ = p.shape
    q = p[..., :GROUP_W]
    k, v = [p[..., i * GROUP_W:(i + 1) * GROUP_W].reshape(B, T, NA_HEADS, HEAD_DIM).transpose(0, 2, 1, 3)
            for i in (1, 2)]
    if ctx_kv is None:
        o = _attention(q, k, v)
    else:
        o = _neighbourhood_attention(q, k, v, lp["na_bias"], ctx_kv[0], ctx_kv[1])
    return o, (k, v)


def _mm(a, b, dims, two_pass):
    f = lambda x, y: lax.dot_general(x, y, (dims, ((), ())), preferred_element_type=F32)

    def one(a, b):
        a16 = a.astype(BF16)
        hi = b.astype(BF16)
        if not two_pass:
            return f(a16, hi)
        return f(a16, hi) + f(a16, (b - hi.astype(F32)).astype(BF16))

    return _lift(one, a, b)


def _dot(a, b, two_pass=False):
    return _mm(a, b, ((1,), (0,)), two_pass)


def _dot_nt(a, b, two_pass=False):
    return _mm(a, b, ((1,), (1,)), two_pass)


def _dot_tn(a, b):
    return _mm(a, b, ((0,), (0,)), False)


def _chunk_masks(reverse):
    row = lax.broadcasted_iota(jnp.int32, (SCAN_CHUNK, SCAN_CHUNK), 0)
    col = lax.broadcasted_iota(jnp.int32, (SCAN_CHUNK, SCAN_CHUNK), 1)
    earlier = (row < col) if reverse else (row > col)
    eye = row == col
    blk16 = (row >> 4) == (col >> 4)
    blk32 = (row >> 5) == (col >> 5)
    return dict(
        strict=earlier.astype(F32), incl=(earlier | eye).astype(F32), eye=eye.astype(F32),
        d16=blk16.astype(F32), off32=(blk32 & jnp.logical_not(blk16)).astype(F32),
        off64=jnp.logical_not(blk32).astype(F32))


def _scan_chains(heads):
    fwd, bwd = _chunk_masks(False), _chunk_masks(True)
    chains = [(b, d, h) for b in range(SCAN_ROWS) for d in range(2) for h in range(heads)]
    masks = {key: _Par([(bwd if d else fwd)[key] for _, d, _ in chains]) for key in fwd}
    return chains, [d == 1 for _, d, _ in chains], masks


def _last_row(cum, revs):
    return _Par([c[0:1] if rev else c[SCAN_CHUNK - 1:SCAN_CHUNK] for c, rev in zip(cum.xs, revs)])


def _unit_triangular_inverse(n, m):
    eye = m["eye"]
    nd = n * m["d16"]
    n2 = _dot(nd, nd)
    x = eye + nd
    x = x + _dot(x, n2)
    n4 = _dot(n2, n2)
    x = x + _dot(x, n4)
    n8 = _dot(n4, n4)
    x = x + _dot(x, n8)
    x = x + _dot(x, _dot(n * m["off32"], x))
    x = x + _dot(x, _dot(n * m["off64"], x))
    return x


def _rwkv_chunk(r, lw, k, v, a, b, h0, m, revs):
    cum = _dot(m["incl"], lw, two_pass=True)
    last = _last_row(cum, revs)
    g_inv = _exp(-cum)
    at = a * _exp(cum - lw)
    rt = r * _exp(cum)
    bt = b * g_inv
    kt = k * g_inv
    to_end = _exp(last - cum)
    bh = b * to_end
    kh = k * to_end
    ar = _stack_rows(at, rt)
    gb = _dot_nt(ar, bt)
    gk = _dot_nt(ar, kt)
    n_ab = gb[:SCAN_CHUNK] * m["strict"]
    l_ak = gk[:SCAN_CHUNK] * m["strict"]
    m_rb = gb[SCAN_CHUNK:] * m["incl"]
    m_rk = gk[SCAN_CHUNK:] * m["incl"]
    tinv = _unit_triangular_inverse(n_ab, m)
    p1 = _dot(tinv, at)
    u0 = _dot(tinv, _dot(l_ak, v))
    p2 = rt + _dot(m_rb, p1)
    o0 = _dot(m_rb, u0) + _dot(m_rk, v)
    a_c = m["eye"] * _exp(last) + _dot_tn(bh, p1)
    g_c = _dot_tn(bh, u0) + _dot_tn(kh, v)
    return _dot(p2, h0) + o0, _dot(a_c, h0) + g_c


def _rwkv_scan_kernel(*refs):
    ins, h0_ref = refs[:12], refs[12]
    o_refs, hout_ref, h_ref = refs[13:15], refs[15], refs[16]
    c = pl.program_id(1)

    @pl.when(c == 0)
    def _():
        h_ref[...] = h0_ref[...]

    chains, revs, m = _scan_chains(RW_HEADS)
    seqs = [_Par([_head_tile(ins[6 * d + i], b, h) for b, d, h in chains]) for i in range(6)]
    o, h_new = _rwkv_chunk(*seqs, _Par([h_ref[b, d, h] for b, d, h in chains]), m, revs)
    _store_heads(o_refs, o, chains)
    for i, (b, d, h) in enumerate(chains):
        h_ref[b, d, h] = h_new.xs[i]

    @pl.when(c == pl.num_programs(1) - 1)
    def _():
        hout_ref[...] = h_ref[...]


def _head_tile(ref, b, h):
    return ref[b, :, h * HEAD_DIM:(h + 1) * HEAD_DIM]


def _store_heads(o_refs, o, chains):
    for d in range(2):
        for b in range(SCAN_ROWS):
            tiles = [o.xs[i] for i, (bi, di, _) in enumerate(chains) if (bi, di) == (b, d)]
            o_refs[d][b] = jnp.concatenate(tiles, axis=1)


def _scan_specs(B, H, T, dh):
    n = T // SCAN_CHUNK
    fwd = pl.BlockSpec((SCAN_ROWS, SCAN_CHUNK, H * dh), lambda b, c: (b, c, 0))
    bwd = pl.BlockSpec((SCAN_ROWS, SCAN_CHUNK, H * dh), lambda b, c: (b, n - 1 - c, 0))
    state = pl.BlockSpec((SCAN_ROWS, 2, H, dh, dh), lambda b, c: (b, 0, 0, 0, 0))
    return (B // SCAN_ROWS, n), fwd, bwd, state


def _rwkv_scan_pallas(seqs, h0):
    B, T, _ = seqs[0][0].shape
    H, dh = RW_HEADS, HEAD_DIM
    grid, fwd, bwd, state_spec = _scan_specs(B, H, T, dh)
    seq = jax.ShapeDtypeStruct((B, T, H * dh), F32)
    return pl.pallas_call(
        _rwkv_scan_kernel,
        grid=grid,
        in_specs=[fwd] * 6 + [bwd] * 6 + [state_spec],
        out_specs=[fwd, bwd, state_spec],
        out_shape=[seq, seq, jax.ShapeDtypeStruct((B, 2, H, dh, dh), F32)],
        scratch_shapes=[pltpu.VMEM((SCAN_ROWS, 2, H, dh, dh), F32)],
        compiler_params=pltpu.CompilerParams(dimension_semantics=("arbitrary", "arbitrary"),
                                             vmem_limit_bytes=VMEM_LIMIT_BYTES),
        name="rwkv_scan",
    )(*seqs[0], *seqs[1], h0)


def _rwkv_mixer(p, lp, S0):
    B, T, _ = p.shape
    W, H = GROUP_W, RW_HEADS
    o0 = 3 * W
    o1 = o0 + 2 * RW_DECAY_RANK
    o2 = o1 + 2 * RW_AAA_RANK
    rkv = p[..., :o0]
    wd = p[..., o0:o1].reshape(B, T, 2, RW_DECAY_RANK)
    ad = p[..., o1:o2].reshape(B, T, 2, RW_AAA_RANK)
    gate = jax.nn.sigmoid(p[..., o2:]) @ lp["rw_g2"]
    heads = lambda a: a.astype(F32).reshape(B, T, H, HEAD_DIM)
    flat = lambda a: a.reshape(B, T, W)
    seqs, bonuses = [], []
    for d in range(2):
        base = jnp.concatenate([rkv, wd[:, :, d], ad[:, :, d]], axis=-1)
        if d == 0:
            shifted = jnp.pad(base[:, :-1], ((0, 0), (1, 0), (0, 0)))
        else:
            shifted = jnp.pad(base[:, 1:], ((0, 0), (0, 1), (0, 0)))
        xd = base + (shifted - base) * lp["rw_mu"][d]
        r, k, v = xd[..., :W], xd[..., W:2 * W], xd[..., 2 * W:3 * W]
        wl, al = xd[..., 3 * W:3 * W + RW_DECAY_RANK], xd[..., 3 * W + RW_DECAY_RANK:]
        w = -jax.nn.softplus(-(lp["rw_w0"][d] + jnp.tanh(wl) @ lp["rw_w2"][d])) - 0.5
        log_decay = -jnp.exp(w.astype(F32))
        a = jax.nn.sigmoid((lp["rw_a0"][d] + al @ lp["rw_a2"][d]).astype(F32))
        kf = k.astype(F32)
        kk = _l2_norm(heads(kf * lp["rw_k_k"].astype(F32)))
        k_eff = heads(kf * (1.0 + (a - 1.0) * lp["rw_k_a"].astype(F32)))
        rh, vh, ah = heads(r), heads(v), heads(a)
        seqs.append([r.astype(F32), log_decay, flat(k_eff), v.astype(F32), flat(-kk), flat(kk * ah)])
        bonuses.append(jnp.sum(rh * k_eff * lp["rw_r_k"].astype(F32), axis=-1, keepdims=True) * vh)
    o_f, o_b, h_fin = _rwkv_scan_pallas(seqs, jnp.swapaxes(S0.astype(F32), -1, -2))
    o = heads(o_f + o_b)
    finals = jnp.swapaxes(h_fin, -1, -2)
    mu = jnp.mean(o, axis=-1, keepdims=True)
    var = jnp.mean(jnp.square(o - mu), axis=-1, keepdims=True)
    y = ((o - mu) * lax.rsqrt(var + RW_LN_EPS)).reshape(B, T, W) * lp["rw_ln_g"].astype(F32) + lp["rw_ln_b"].astype(F32)
    y = y + (bonuses[0] + bonuses[1]).reshape(B, T, W)
    return y.astype(p.dtype) * gate, finals


def _delta_chunk(q, k, v, gb, betab, s0, m, revs):
    gc = _dot(m["incl"], gb, two_pass=True)
    last = _last_row(gc, revs)
    diff = gc - _dot_nt(m["eye"], gc, two_pass=True)
    dmat = _lift(lambda keep, d: jnp.where(keep > 0.0, jnp.exp(jnp.where(keep > 0.0, d, 0.0)), 0.0), m["incl"], diff)
    kb = k * betab
    gram = _dot_nt(_stack_rows(kb, q), k)
    a_mat = gram[:SCAN_CHUNK] * dmat * m["strict"]
    qk = gram[SCAN_CHUNK:] * dmat
    tm = _unit_triangular_inverse(-a_mat, m)
    u = _dot(tm, v * betab)
    w = _dot(tm, kb * _exp(gc))
    v_new = u - _dot(w, s0)
    o = _dot(q * _exp(gc), s0) + _dot(qk, v_new)
    return o, s0 * _exp(last) + _dot_tn(k * _exp(last - gc), v_new)


def _delta_scan_kernel(*refs):
    ins, s0_ref = refs[:10], refs[10]
    o_refs, sout_ref, s_ref = refs[11:13], refs[13], refs[14]
    c = pl.program_id(1)

    @pl.when(c == 0)
    def _():
        s_ref[...] = s0_ref[...]

    chains, revs, m = _scan_chains(DN_HEADS)
    seqs = [_Par([_head_tile(ins[5 * d + i], b, h) for b, d, h in chains]) for i in range(5)]
    o, s_new = _delta_chunk(*seqs, _Par([s_ref[b, d, h] for b, d, h in chains]), m, revs)
    _store_heads(o_refs, o, chains)
    for i, (b, d, h) in enumerate(chains):
        s_ref[b, d, h] = s_new.xs[i]

    @pl.when(c == pl.num_programs(1) - 1)
    def _():
        sout_ref[...] = s_ref[...]


def _delta_scan_pallas(q, k, v, g, beta, s0):
    assert SCAN_CHUNK == HEAD_DIM
    B, T, _ = q.shape
    H, dh = DN_HEADS, HEAD_DIM
    grid, fwd, bwd, state_spec = _scan_specs(B, H, T, dh)
    seq = jax.ShapeDtypeStruct((B, T, H * dh), F32)
    return pl.pallas_call(
        _delta_scan_kernel,
        grid=grid,
        in_specs=[fwd] * 5 + [bwd] * 5 + [state_spec],
        out_specs=[fwd, bwd, state_spec],
        out_shape=[seq, seq, jax.ShapeDtypeStruct((B, 2, H, dh, dh), F32)],
        scratch_shapes=[pltpu.VMEM((SCAN_ROWS, 2, H, dh, dh), F32)],
        compiler_params=pltpu.CompilerParams(dimension_semantics=("arbitrary", "arbitrary"),
                                             vmem_limit_bytes=VMEM_LIMIT_BYTES),
        name="delta_scan",
    )(q, k, v, g[0], beta[0], q, k, v, g[1], beta[1], s0)


def _deltanet_mixer(p, lp, S0):
    B, T, _ = p.shape
    W, H = GROUP_W, DN_HEADS
    xin = jnp.pad(p[..., :3 * W], ((0, 0), (DN_CONV // 2, DN_CONV // 2), (0, 0)))
    qkv = sum(xin[:, j:j + T] * lp["dn_conv"][j] for j in range(DN_CONV))
    qkv = jax.nn.silu(qkv)
    heads = lambda a: a.astype(F32).reshape(B, T, H, HEAD_DIM)
    flat = lambda a: a.reshape(B, T, W)
    q = _l2_norm(heads(qkv[..., :W])) * (HEAD_DIM ** -0.5)
    k = _l2_norm(heads(qkv[..., W:2 * W]))
    v = heads(qkv[..., 2 * W:])
    beta = jax.nn.sigmoid(p[..., 3 * W:3 * W + 2 * H].astype(F32)).reshape(B, T, 2, H)
    alpha = p[..., 3 * W + 2 * H:3 * W + 4 * H].astype(F32).reshape(B, T, 2, H)
    g = -jnp.exp(lp["dn_a_log"].astype(F32)) * jax.nn.softplus(alpha + lp["dn_dt_bias"].astype(F32))
    z = heads(p[..., 3 * W + 4 * H:])
    lanes = lambda a: [jnp.repeat(a[:, :, d], HEAD_DIM, axis=-1) for d in range(2)]
    o_f, o_b, s_fin = _delta_scan_pallas(flat(q), flat(k), flat(v), lanes(g), lanes(beta), S0.astype(F32))
    o = _rms_norm(heads(o_f + o_b), lp["dn_norm_g"]) * jax.nn.silu(z)
    return o.reshape(B, T, W).astype(p.dtype), s_fin


def _modulated_norm(x, gain, scale, shift):
    y = x * lax.rsqrt(jnp.mean(x * x, axis=-1, keepdims=True) + NORM_EPS)
    return (y * gain) * (1.0 + scale) + shift


def _proj_in_kernel(x_ref, mod_ref, gain_ref, w_ref, o_ref):
    h = _modulated_norm(x_ref[...], gain_ref[...], mod_ref[0, MOD_SCALE1:MOD_SCALE1 + 1], mod_ref[0, MOD_SHIFT1:MOD_SHIFT1 + 1])
    o_ref[...] = jnp.dot(h.astype(BF16), w_ref[...], preferred_element_type=F32)


def _proj_out_kernel(mix_ref, x_ref, mod_ref, gain_ref, w_ref, x_out_ref, h2_ref):
    y = jnp.dot(mix_ref[...].astype(BF16), w_ref[...], preferred_element_type=F32)
    x_new = x_ref[...] + mod_ref[0, MOD_GATE1:MOD_GATE1 + 1] * y
    x_out_ref[...] = x_new
    h2_ref[...] = _modulated_norm(x_new, gain_ref[...], mod_ref[0, MOD_SCALE2:MOD_SCALE2 + 1],
                                  mod_ref[0, MOD_SHIFT2:MOD_SHIFT2 + 1])


def _proj_in(x, mod, gain, w_bf, tokens_per_cond):
    n, cols = x.shape[0], w_bf.shape[1]
    tm = PROJ_TOKENS
    tok = pl.BlockSpec((tm, D_MODEL), lambda i: (i, 0))
    return pl.pallas_call(
        _proj_in_kernel,
        grid=(n // tm,),
        in_specs=[tok, _mod_spec(mod, tokens_per_cond, tm), pl.BlockSpec((1, D_MODEL), lambda i: (0, 0)),
                  pl.BlockSpec((D_MODEL, cols), lambda i: (0, 0))],
        out_specs=pl.BlockSpec((tm, cols), lambda i: (i, 0)),
        out_shape=jax.ShapeDtypeStruct((n, cols), F32),
        compiler_params=pltpu.CompilerParams(dimension_semantics=("arbitrary",), vmem_limit_bytes=VMEM_LIMIT_BYTES),
        name="proj_in",
    )(x, mod, gain, w_bf)


def _proj_out(mix, x, mod, gain, w_bf, tokens_per_cond):
    n = x.shape[0]
    tm = PROJ_TOKENS
    tok = pl.BlockSpec((tm, D_MODEL), lambda i: (i, 0))
    out = jax.ShapeDtypeStruct((n, D_MODEL), F32)
    return pl.pallas_call(
        _proj_out_kernel,
        grid=(n // tm,),
        in_specs=[tok, tok, _mod_spec(mod, tokens_per_cond, tm), pl.BlockSpec((1, D_MODEL), lambda i: (0, 0)),
                  pl.BlockSpec((D_MODEL, D_MODEL), lambda i: (0, 0))],
        out_specs=[tok, tok],
        out_shape=[out, out],
        compiler_params=pltpu.CompilerParams(dimension_semantics=("arbitrary",), vmem_limit_bytes=VMEM_LIMIT_BYTES),
        name="proj_out",
    )(mix, x, mod, gain, w_bf)


def _trunk_layer(x, cond, lp, ctx, final_gain):
    B, T, _ = x.shape
    mod = (jax.nn.silu(cond) @ lp["w_mod"] + lp["b_mod"]).reshape(-1, 6, D_MODEL)
    x = x.reshape(B * T, D_MODEL)
    proj = _proj_in(x, mod, lp["norm1_g"][None], lp["w_in_bf"], T).reshape(B, T, -1)
    pa, pb, pc, pd, _ = jnp.split(proj, [GQA_PROJ, GQA_PROJ + RW_PROJ, GQA_PROJ + RW_PROJ + NA_PROJ, IN_COLS], axis=-1)
    if ctx is None:
        ctx_a = None
        ctx_c = None
        s_rw0 = jnp.zeros((B, 2, RW_HEADS, HEAD_DIM, HEAD_DIM), F32)
        s_dn0 = jnp.zeros((B, 2, DN_HEADS, HEAD_DIM, HEAD_DIM), F32)
    else:
        ka_c, va_c, kc_c, vc_c, s_rw0, s_dn0 = ctx
        ctx_a = (ka_c, va_c)
        ctx_c = (kc_c, vc_c)
    oa, (ka, va) = _gqa_mixer(pa, lp, ctx_a)
    ob, s_rw = _rwkv_mixer(pb, lp, s_rw0)
    oc, (kc, vc) = _na_mixer(pc, lp, ctx_c)
    od, s_dn = _deltanet_mixer(pd, lp, s_dn0)
    mix = jnp.concatenate([oa, ob, oc, od], axis=-1).reshape(B * T, D_MODEL)
    x, h2 = _proj_out(mix, x, mod, lp["norm2_g"][None], lp["w_out_bf"], T)
    x = _peer_residual(x, h2, mod, lp, T, final_gain)
    return x.reshape(B, T, D_MODEL), (ka, va, kc, vc, s_rw, s_dn)


def kernel(x_prompt, x_sample, c, cache_gqa_k, cache_gqa_v, cache_na_k, cache_na_v, state_rwkv, state_delta, c_ctx, norm1_g, norm2_g, w_mod, b_mod, w_in, w_out, gqa_q_norm, gqa_k_norm, rw_mu, rw_w0, rw_w2, rw_a0, rw_a2, rw_g2, rw_k_k, rw_k_a, rw_r_k, rw_ln_g, rw_ln_b, na_bias, dn_conv, dn_a_log, dn_dt_bias, dn_norm_g, peer_wq, peer_keys, peer_u, peer_v, final_norm_g):
    depth = w_in.shape[0]

    def layer_params(l):
        return {
            "norm1_g": norm1_g[l], "norm2_g": norm2_g[l], "w_mod": w_mod[l], "b_mod": b_mod[l],
            "w_in_bf": jnp.pad(w_in[l].astype(BF16), ((0, 0), (0, IN_COLS_PADDED - IN_COLS))),
            "w_out_bf": w_out[l].astype(BF16), "gqa_q_norm": gqa_q_norm[l], "gqa_k_norm": gqa_k_norm[l],
            "rw_mu": rw_mu[l], "rw_w0": rw_w0[l], "rw_w2": rw_w2[l], "rw_a0": rw_a0[l], "rw_a2": rw_a2[l],
            "rw_g2": rw_g2[l], "rw_k_k": rw_k_k[l], "rw_k_a": rw_k_a[l], "rw_r_k": rw_r_k[l],
            "rw_ln_g": rw_ln_g[l], "rw_ln_b": rw_ln_b[l], "na_bias": na_bias[l],
            "dn_conv": dn_conv[l], "dn_a_log": dn_a_log[l], "dn_dt_bias": dn_dt_bias[l], "dn_norm_g": dn_norm_g[l],
            "peer_wq_bf": peer_wq[l].astype(BF16),
            "peer_keys_bf": peer_keys[l].reshape(2 * PEER_HEADS, PEER_NKEYS, PEER_DQ // 2).astype(BF16),
            "peer_u_bf": peer_u[l].astype(BF16),
            "peer_vt_bf": peer_v[l].astype(BF16).T,
        }

    params = [layer_params(l) for l in range(depth)]

    xp = x_prompt
    cond_ctx = c_ctx[None, :]
    per_layer = []
    closing = lambda l: final_norm_g if l == depth - 1 else None
    for l in range(depth):
        xp, st = _trunk_layer(xp, cond_ctx, params[l], None, closing(l))
        per_layer.append(st)
    y_prompt = xp

    xs = x_sample
    for l in range(depth):
        ctx = (cache_gqa_k[:, l], cache_gqa_v[:, l], cache_na_k[:, l], cache_na_v[:, l],
               state_rwkv[:, l], state_delta[:, l])
        xs, _ = _trunk_layer(xs, c, params[l], ctx, closing(l))
    y_sample = xs

    dt = x_prompt.dtype
    new_gqa_k = jnp.stack([st[0] for st in per_layer], axis=1).astype(dt)
    new_gqa_v = jnp.stack([st[1] for st in per_layer], axis=1).astype(dt)
    new_na_k = jnp.stack([st[2] for st in per_layer], axis=1).astype(dt)
    new_na_v = jnp.stack([st[3] for st in per_layer], axis=1).astype(dt)
    new_state_rwkv = jnp.stack([st[4] for st in per_layer], axis=1).astype(dt)
    new_state_delta = jnp.stack([st[5] for st in per_layer], axis=1).astype(dt)
    return (y_prompt, y_sample, new_gqa_k, new_gqa_v, new_na_k, new_na_v, new_state_rwkv, new_state_delta)
```

```python
import functools
import math

import jax
import jax.numpy as jnp
import numpy as np
from jax import lax
from jax.experimental import pallas as pl
from jax.experimental.pallas import tpu as pltpu

F32 = jnp.float32
BF16 = jnp.bfloat16

D_MODEL = 1024
GRID_W = 64
HEAD_DIM = 64
GROUP_W = D_MODEL // 4
NORM_EPS = 1e-6
NEG_INF = -1e30
GQA_HEADS = GROUP_W // HEAD_DIM
GQA_KV_HEADS = GQA_HEADS // 2
ROPE_THETA = 10000.0
RW_HEADS = GROUP_W // HEAD_DIM
RW_DECAY_RANK = 32
RW_AAA_RANK = 32
RW_GATE_RANK = 64
RW_LN_EPS = 64e-5
NA_HEADS = GROUP_W // HEAD_DIM
NA_ROWS = 8
NA_COLS = 16
DN_HEADS = GROUP_W // HEAD_DIM
DN_CONV = 5
PEER_HEADS = 8
PEER_NKEYS = 128
PEER_EXPERTS = PEER_NKEYS * PEER_NKEYS
PEER_DQ = 256
PEER_TOPK = 16
GQA_PROJ = GROUP_W + 2 * GQA_KV_HEADS * HEAD_DIM
RW_PROJ = 3 * GROUP_W + 2 * RW_DECAY_RANK + 2 * RW_AAA_RANK + RW_GATE_RANK
NA_PROJ = 3 * GROUP_W
DN_PROJ = 3 * GROUP_W + 4 * DN_HEADS + GROUP_W

LANES = 128
VMEM_LIMIT_BYTES = 56 * 1024 * 1024

PEER_SEL_TOKENS = 128
PEER_TOKENS = 256
PEER_ROWS_PER_STEP = 16
PEER_SUB_ROWS = 2
PEER_HALF = 64
PEER_MATMUL_PIECE = 256
PEER_CHUNK = PEER_ROWS_PER_STEP * PEER_NKEYS
PEER_STEPS = PEER_EXPERTS // PEER_CHUNK
INV_SQRT2 = 1.0 / math.sqrt(2.0)

IN_COLS = GQA_PROJ + RW_PROJ + NA_PROJ + DN_PROJ
IN_COLS_PADDED = -(-IN_COLS // LANES) * LANES
MOD_SHIFT1, MOD_SCALE1, MOD_GATE1, MOD_SHIFT2, MOD_SCALE2, MOD_GATE2 = range(6)
PROJ_TOKENS = 512

ATTN_SCALE = HEAD_DIM ** -0.5
ATTN_Q_TILE = 256
NA_ROW_BLOCK = 8

SCAN_CHUNK = 64
SCAN_ROWS = 4


class _Par:
    def __init__(self, xs):
        self.xs = list(xs)

    def __add__(self, o):
        return _lift(jnp.add, self, o)

    __radd__ = __add__

    def __sub__(self, o):
        return _lift(jnp.subtract, self, o)

    def __rsub__(self, o):
        return _lift(lambda x, y: y - x, self, o)

    def __mul__(self, o):
        return _lift(jnp.multiply, self, o)

    __rmul__ = __mul__

    def __neg__(self):
        return _lift(jnp.negative, self)

    def __getitem__(self, idx):
        return _lift(lambda x: x[idx], self)


def _lift(f, *args):
    n = next(len(a.xs) for a in args if isinstance(a, _Par))
    return _Par([f(*[a.xs[i] if isinstance(a, _Par) else a for a in args]) for i in range(n)])


def _exp(x):
    return _lift(jnp.exp, x)


def _stack_rows(x, y):
    return _lift(lambda a, b: jnp.concatenate([a, b], axis=0), x, y)


def _top_k_rows(s, k, exact):
    rows = s.shape[0]
    iota = lax.broadcasted_iota(jnp.int32, s.shape, 0).astype(F32)
    vals = []
    work = s
    for _ in range(k):
        m = jnp.max(work, axis=0, keepdims=True)
        if exact:
            first = jnp.min(jnp.where(work == m, iota, float(rows)), axis=0, keepdims=True)
            work = jnp.where(iota == first, -jnp.inf, work)
        else:
            work = jnp.where(work == m, -jnp.inf, work)
        vals.append(m)
    if exact:
        return vals, work, jnp.zeros_like(vals[0])
    removed = jnp.sum(jnp.where(work == -jnp.inf, 1.0, 0.0), axis=0, keepdims=True)
    return vals, work, jnp.where(removed == float(k), 0.0, 1.0)


def _peer_select_head(q, keys_ref, h, thr_ref, e1_ref, s2m_ref, e2_ref, exact):
    half = PEER_DQ // 2
    scores, tops, sel = [], [], []
    suspect = None
    for p in range(2):
        c0 = (2 * h + p) * half
        qhp = q[:, c0:c0 + half].astype(BF16)
        s = lax.dot_general(keys_ref[2 * h + p], qhp, (((1,), (1,)), ((), ())),
                            preferred_element_type=F32)
        vals, work, flag = _top_k_rows(s, PEER_TOPK, exact)
        suspect = flag if suspect is None else suspect + flag
        scores.append(s)
        tops.append(vals)
        sel.append(work == -jnp.inf)
    m1, m2 = tops
    m2_all = jnp.concatenate(m2, axis=0)
    cand = jnp.concatenate([m1[0] + m2_all] + [m1[i] + m2_all[:8] for i in range(1, 8)]
                           + [jnp.concatenate(m1[8:], axis=0) + m2[0]], axis=0)
    top_s, _, flag = _top_k_rows(cand, PEER_TOPK, exact)
    suspect = suspect + flag
    mx = top_s[0]
    z = jnp.exp(top_s[0] - mx)
    for r in range(1, PEER_TOPK):
        z = z + jnp.exp(top_s[r] - mx)
    half_inv_z = 0.5 / z
    t3 = top_s[PEER_TOPK - 1]
    s1m = jnp.where(sel[0], scores[0], -jnp.inf)
    thr = jnp.full_like(s1m, jnp.inf)
    for j in range(PEER_TOPK // 2):
        thr = jnp.where(s1m + m2[j] >= t3, m2[j], thr)
    thr_best = jnp.full_like(t3, jnp.inf)
    for j in range(PEER_TOPK // 2, PEER_TOPK):
        thr_best = jnp.where(m1[0] + m2[j] >= t3, m2[j], thr_best)
    thr = jnp.where(s1m == m1[0], jnp.minimum(thr, thr_best), thr)
    thr_ref[h] = thr
    e1_ref[h] = jnp.where(sel[0], jnp.exp(scores[0] - m1[0]) * half_inv_z, 0.0)
    s2m_ref[h] = jnp.where(sel[1], scores[1], -jnp.inf)
    e2_ref[h] = jnp.exp(scores[1] - m2[0])
    return suspect


def _peer_select_kernel(h_ref, wq_ref, keys_ref, *out_refs):
    x = h_ref[...].astype(BF16)
    q = jnp.dot(x, wq_ref[...], preferred_element_type=F32)
    flags = [_peer_select_head(q, keys_ref, h, *out_refs, exact=False) for h in range(PEER_HEADS)]

    @pl.when(jnp.max(sum(flags)) > 0.0)
    def _():
        for h in range(PEER_HEADS):
            @pl.when(jnp.max(flags[h]) > 0.0)
            def _():
                _peer_select_head(q, keys_ref, h, *out_refs, exact=True)


def _peer_select(h2, wq_bf, keys_bf):
    n = h2.shape[0]
    tt = PEER_SEL_TOKENS
    big = jax.ShapeDtypeStruct((PEER_HEADS, PEER_NKEYS, n), F32)
    big_spec = pl.BlockSpec((PEER_HEADS, PEER_NKEYS, tt), lambda i: (0, 0, i))
    return pl.pallas_call(
        _peer_select_kernel,
        grid=(n // tt,),
        in_specs=[
            pl.BlockSpec((tt, D_MODEL), lambda i: (i, 0)),
            pl.BlockSpec((D_MODEL, PEER_HEADS * PEER_DQ), lambda i: (0, 0)),
            pl.BlockSpec((2 * PEER_HEADS, PEER_NKEYS, PEER_DQ // 2), lambda i: (0, 0, 0)),
        ],
        out_specs=[big_spec] * 4,
        out_shape=[big] * 4,
        compiler_params=pltpu.CompilerParams(dimension_semantics=("arbitrary",),
                                             vmem_limit_bytes=VMEM_LIMIT_BYTES),
        name="peer_select",
    )(h2, wq_bf, keys_bf)


def _peer_dense_kernel(h_ref, u_ref, v_ref, thr_ref, e1_ref, s2m_ref, e2_ref, xres_ref, mod_ref, fin_ref,
                       out_ref, ht_ref, acc_ref, p_ref, act_ref, *, final_norm):
    j = pl.program_id(1)
    tt = h_ref.shape[0]

    @pl.when(j == 0)
    def _():
        ht_ref[...] = h_ref[...].T.astype(BF16)
        acc_ref[...] = jnp.zeros_like(acc_ref)

    sub = PEER_SUB_ROWS * PEER_NKEYS
    lane_blocks = tt // LANES
    halves = PEER_NKEYS // PEER_HALF
    n_block = lane_blocks * halves
    kp = PEER_MATMUL_PIECE
    n_piece = D_MODEL // kp
    assert n_block % n_piece == 0
    blocks_per_piece = n_block // n_piece

    def activation_piece(k, i):
        part = jnp.dot(u_ref[k * sub:(k + 1) * sub, i * kp:(i + 1) * kp], ht_ref[i * kp:(i + 1) * kp, :],
                       preferred_element_type=F32)
        if i == 0:
            act_ref[k % 2] = part
        else:
            act_ref[k % 2] += part

    def accumulate_piece(k, i):
        acc_ref[i * kp:(i + 1) * kp, :] += lax.dot_general(
            v_ref[k * sub:(k + 1) * sub, i * kp:(i + 1) * kp], p_ref[k % 2], (((0,), (0,)), ((), ())),
            preferred_element_type=F32)

    def weighted_block(k, i):
        c, hb = divmod(i, halves)
        cs = slice(c * LANES, (c + 1) * LANES)
        bs = slice(hb * PEER_HALF, (hb + 1) * PEER_HALF)
        firsts = [k * PEER_SUB_ROWS + ai for ai in range(PEER_SUB_ROWS)]
        w = [None for _ in firsts]
        for h in range(PEER_HEADS):
            s2 = s2m_ref[h, bs, cs]
            e2 = e2_ref[h, bs, cs]
            for ai, a in enumerate(firsts):
                term = jnp.where(s2 >= thr_ref[h, a:a + 1, cs], e2, 0.0) * e1_ref[h, a:a + 1, cs]
                w[ai] = term if w[ai] is None else w[ai] + term
        for ai in range(PEER_SUB_ROWS):
            rows = slice(ai * PEER_NKEYS + hb * PEER_HALF, ai * PEER_NKEYS + (hb + 1) * PEER_HALF)
            zz = act_ref[k % 2, rows, cs]
            gelu2 = zz * (1.0 + lax.erf(zz * INV_SQRT2))
            p_ref[k % 2, rows, cs] = (w[ai] * gelu2).astype(BF16)

    n_sub = PEER_ROWS_PER_STEP // PEER_SUB_ROWS
    for i in range(n_piece):
        activation_piece(0, i)
    for k in range(n_sub):
        for blk in range(n_block):
            if blk % blocks_per_piece == 0:
                if k + 1 < n_sub:
                    activation_piece(k + 1, blk // blocks_per_piece)
                if k > 0:
                    accumulate_piece(k - 1, blk // blocks_per_piece)
            weighted_block(k, blk)
    for i in range(n_piece):
        accumulate_piece(n_sub - 1, i)

    @pl.when(j == PEER_STEPS - 1)
    def _():
        y = xres_ref[...] + mod_ref[0, MOD_GATE2:MOD_GATE2 + 1] * acc_ref[...].T
        if final_norm:
            y = y * lax.rsqrt(jnp.mean(y * y, axis=-1, keepdims=True) + NORM_EPS) * fin_ref[...]
        out_ref[...] = y


def _mod_spec(mod, tokens_per_cond, tile):
    tiles_per_cond = tokens_per_cond // tile
    if mod.shape[0] == 1:
        return pl.BlockSpec((1, 6, D_MODEL), lambda i, *_: (0, 0, 0))
    assert tokens_per_cond % tile == 0
    return pl.BlockSpec((1, 6, D_MODEL), lambda i, *_: (i // tiles_per_cond, 0, 0))


def _peer_dense(h2, u_bf, v_bf, sel, xres, mod, tokens_per_cond, final_gain):
    n = h2.shape[0]
    tt = PEER_TOKENS
    thr, e1, s2m, e2 = sel
    fin = jnp.ones((1, D_MODEL), F32) if final_gain is None else final_gain.astype(F32)[None]
    row_spec = pl.BlockSpec((PEER_HEADS, PEER_ROWS_PER_STEP, tt), lambda i, j: (0, j, i))
    col_spec = pl.BlockSpec((PEER_HEADS, PEER_NKEYS, tt), lambda i, j: (0, 0, i))
    tok_spec = pl.BlockSpec((tt, D_MODEL), lambda i, j: (i, 0))
    return pl.pallas_call(
        functools.partial(_peer_dense_kernel, final_norm=final_gain is not None),
        grid=(n // tt, PEER_STEPS),
        in_specs=[
            tok_spec,
            pl.BlockSpec((PEER_CHUNK, D_MODEL), lambda i, j: (j, 0)),
            pl.BlockSpec((PEER_CHUNK, D_MODEL), lambda i, j: (j, 0)),
            row_spec, row_spec, col_spec, col_spec,
            tok_spec,
            _mod_spec(mod, tokens_per_cond, tt),
            pl.BlockSpec((1, D_MODEL), lambda i, j: (0, 0)),
        ],
        out_specs=tok_spec,
        out_shape=jax.ShapeDtypeStruct((n, D_MODEL), F32),
        scratch_shapes=[pltpu.VMEM((D_MODEL, tt), BF16), pltpu.VMEM((D_MODEL, tt), F32),
                        pltpu.VMEM((2, PEER_SUB_ROWS * PEER_NKEYS, tt), BF16),
                        pltpu.VMEM((2, PEER_SUB_ROWS * PEER_NKEYS, tt), F32)],
        compiler_params=pltpu.CompilerParams(dimension_semantics=("arbitrary", "arbitrary"),
                                             vmem_limit_bytes=VMEM_LIMIT_BYTES),
        name="peer_dense",
    )(h2, u_bf, v_bf, thr, e1, s2m, e2, xres, mod, fin)


def _peer_residual(x, h2, mod, lp, tokens_per_cond, final_gain):
    sel = _peer_select(h2, lp["peer_wq_bf"], lp["peer_keys_bf"])
    return _peer_dense(h2, lp["peer_u_bf"], lp["peer_v_bf"], sel, x, mod, tokens_per_cond, final_gain)


def _rms_norm(x, g):
    xf = x.astype(F32)
    y = xf * lax.rsqrt(jnp.mean(xf * xf, axis=-1, keepdims=True) + NORM_EPS)
    return (y * g.astype(F32)).astype(x.dtype)


def _l2_norm(x):
    return x * lax.rsqrt(jnp.sum(x * x, axis=-1, keepdims=True) + NORM_EPS)


def _axial_rope_tables(T):
    t = np.arange(T)
    row, col = t // GRID_W, t % GRID_W
    n_freq = HEAD_DIM // 4
    inv = ROPE_THETA ** (-2.0 * np.arange(n_freq) / (HEAD_DIM // 2))
    ang = np.concatenate([row[:, None] * inv[None], col[:, None] * inv[None]], axis=1)
    return jnp.asarray(np.cos(ang), F32), jnp.asarray(np.sin(ang), F32)


def _apply_rope(x, cos, sin):
    xf = x.astype(F32)
    half = HEAD_DIM // 2
    x1, x2 = xf[..., :half], xf[..., half:]
    c, s = cos[None, :, None, :], sin[None, :, None, :]
    return jnp.concatenate([x1 * c - x2 * s, x2 * c + x1 * s], axis=-1).astype(x.dtype)


def _attention_kernel(q_ref, k_ref, v_ref, o_ref, *, group):
    outs = []
    for h in range(q_ref.shape[2] // HEAD_DIM):
        q = q_ref[0, :, h * HEAD_DIM:(h + 1) * HEAD_DIM].astype(BF16)
        s = lax.dot_general(q, k_ref[0, h // group], (((1,), (1,)), ((), ())), preferred_element_type=F32) * ATTN_SCALE
        m = jnp.max(s, axis=-1, keepdims=True)
        p = jnp.exp(s - m)
        l = jnp.sum(p, axis=-1, keepdims=True)
        outs.append(jnp.dot(p.astype(BF16), v_ref[0, h // group], preferred_element_type=F32) / l)
    o_ref[0] = jnp.concatenate(outs, axis=1)


def _attention(q, k, v):
    B, T, width = q.shape
    hkv, L, dh = k.shape[1], k.shape[2], k.shape[3]
    group = width // dh // hkv
    tq = min(T, ATTN_Q_TILE)
    kv_spec = pl.BlockSpec((1, hkv, L, dh), lambda b, i: (b, 0, 0, 0))
    q_spec = pl.BlockSpec((1, tq, width), lambda b, i: (b, i, 0))
    return pl.pallas_call(
        functools.partial(_attention_kernel, group=group),
        grid=(B, T // tq),
        in_specs=[q_spec, kv_spec, kv_spec],
        out_specs=q_spec,
        out_shape=jax.ShapeDtypeStruct((B, T, width), F32),
        compiler_params=pltpu.CompilerParams(dimension_semantics=("arbitrary",) * 2,
                                             vmem_limit_bytes=VMEM_LIMIT_BYTES),
        name="attention",
    )(q, k.astype(BF16), v.astype(BF16))


def _na_bias_table(bias):
    qcols = np.arange(GRID_W)
    kcols = np.arange(GRID_W)
    cstart = np.clip(qcols - NA_COLS // 2, 0, GRID_W - NA_COLS)
    valid = (kcols[None, :] >= cstart[:, None]) & (kcols[None, :] < cstart[:, None] + NA_COLS)
    dc_idx = np.clip(kcols[None, :] - qcols[:, None], -(NA_COLS - 1), NA_COLS - 1) + (NA_COLS - 1)
    onehot = jnp.asarray(dc_idx[:, :, None] == np.arange(2 * NA_COLS - 1), F32)
    rel = jnp.einsum("qkc,hrc->hrqk", onehot, bias.astype(F32), precision=lax.Precision.HIGHEST)
    rel = jnp.where(valid[None, None], rel, NEG_INF)
    return jnp.stack([jnp.concatenate([rel[:, d0 + j] for j in range(NA_ROWS)], axis=-1)
                      for d0 in range(NA_ROWS)], axis=1)


def _na_kernel(q_ref, k_ref, v_ref, kc_ref, vc_ref, bias_ref, o_ref):
    blk = pl.program_id(1)
    n_rows = k_ref.shape[2] // GRID_W
    win = NA_ROWS * GRID_W
    nt = (((1,), (1,)), ((), ()))
    rows = []
    for i in range(NA_ROW_BLOCK):
        r = blk * NA_ROW_BLOCK + i
        start = jnp.clip(r - NA_ROWS // 2, 0, n_rows - NA_ROWS)
        rows.append((pl.multiple_of(start * GRID_W, GRID_W), start - r + (NA_ROWS - 1)))
    outs = []
    for h in range(NA_HEADS):
        q = _Par([q_ref[0, i * GRID_W:(i + 1) * GRID_W, h * HEAD_DIM:(h + 1) * HEAD_DIM].astype(BF16)
                  for i in range(NA_ROW_BLOCK)])
        k = _Par([k_ref[0, h, pl.ds(off, win), :] for off, _ in rows])
        v = _Par([v_ref[0, h, pl.ds(off, win), :] for off, _ in rows])
        bias = _Par([bias_ref[h, d0] for _, d0 in rows])
        s_loc = _lift(lambda q, k, b: lax.dot_general(q, k, nt, preferred_element_type=F32) * ATTN_SCALE + b,
                      q, k, bias)
        s_ctx = _lift(lambda q: lax.dot_general(q, kc_ref[0, h], nt, preferred_element_type=F32) * ATTN_SCALE, q)
        m = _lift(lambda a, b: jnp.maximum(jnp.max(a, axis=-1, keepdims=True), jnp.max(b, axis=-1, keepdims=True)),
                  s_loc, s_ctx)
        p_loc = _exp(s_loc - m)
        p_ctx = _exp(s_ctx - m)
        l = _lift(lambda a, b: jnp.sum(a, axis=-1, keepdims=True) + jnp.sum(b, axis=-1, keepdims=True), p_loc, p_ctx)
        o = _lift(lambda pl_, v, pc, l: (jnp.dot(pl_.astype(BF16), v, preferred_element_type=F32)
                                         + jnp.dot(pc.astype(BF16), vc_ref[0, h], preferred_element_type=F32)) / l,
                  p_loc, v, p_ctx, l)
        outs.append(jnp.concatenate(o.xs, axis=0))
    o_ref[0] = jnp.concatenate(outs, axis=1)


def _neighbourhood_attention(q, k, v, bias, k_ctx, v_ctx):
    B, H, T, dh = k.shape
    P = k_ctx.shape[2]
    rows = T // GRID_W
    assert rows >= NA_ROWS and rows % NA_ROW_BLOCK == 0
    tq = NA_ROW_BLOCK * GRID_W
    full = lambda n: pl.BlockSpec((1, H, n, dh), lambda b, i: (b, 0, 0, 0))
    q_spec = pl.BlockSpec((1, tq, H * dh), lambda b, i: (b, i, 0))
    return pl.pallas_call(
        _na_kernel,
        grid=(B, rows // NA_ROW_BLOCK),
        in_specs=[q_spec, full(T), full(T), full(P), full(P),
                  pl.BlockSpec((H, NA_ROWS, GRID_W, NA_ROWS * GRID_W), lambda b, i: (0, 0, 0, 0))],
        out_specs=q_spec,
        out_shape=jax.ShapeDtypeStruct((B, T, H * dh), F32),
        compiler_params=pltpu.CompilerParams(dimension_semantics=("arbitrary",) * 2,
                                             vmem_limit_bytes=VMEM_LIMIT_BYTES),
        name="na_attention",
    )(q, k.astype(BF16), v.astype(BF16), k_ctx.astype(BF16), v_ctx.astype(BF16), _na_bias_table(bias))


def _gqa_mixer(p, lp, ctx_kv):
    B, T, _ = p.shape
    kvw = GQA_KV_HEADS * HEAD_DIM
    q = _rms_norm(p[..., :GROUP_W].reshape(B, T, GQA_HEADS, HEAD_DIM), lp["gqa_q_norm"])
    k = _rms_norm(p[..., GROUP_W:GROUP_W + kvw].reshape(B, T, GQA_KV_HEADS, HEAD_DIM), lp["gqa_k_norm"])
    v = p[..., GROUP_W + kvw:].reshape(B, T, GQA_KV_HEADS, HEAD_DIM)
    if ctx_kv is not None:
        cos, sin = _axial_rope_tables(T)
        q = _apply_rope(q, cos, sin)
        k = _apply_rope(k, cos, sin)
    qf = q.reshape(B, T, GROUP_W)
    kh = k.transpose(0, 2, 1, 3)
    vh = v.transpose(0, 2, 1, 3)
    if ctx_kv is None:
        o = _attention(qf, kh, vh)
    else:
        k_ctx, v_ctx = ctx_kv
        o = _attention(qf, jnp.concatenate([kh, k_ctx], axis=2), jnp.concatenate([vh, v_ctx], axis=2))
    return o, (kh, vh)


def _na_mixer(p, lp, ctx_kv):
    B, T, _ = p.shape
    q = p[..., :GROUP_W]
    k, v = [p[..., i * GROUP_W:(i + 1) * GROUP_W].reshape(B, T, NA_HEADS, HEAD_DIM).transpose(0, 2, 1, 3)
            for i in (1, 2)]
    if ctx_kv is None:
        o = _attention(q, k, v)
    else:
        o = _neighbourhood_attention(q, k, v, lp["na_bias"], ctx_kv[0], ctx_kv[1])
    return o, (k, v)


def _mm(a, b, dims, two_pass):
    f = lambda x, y: lax.dot_general(x, y, (dims, ((), ())), preferred_element_type=F32)

    def one(a, b):
        a16 = a.astype(BF16)
        hi = b.astype(BF16)
        if not two_pass:
            return f(a16, hi)
        return f(a16, hi) + f(a16, (b - hi.astype(F32)).astype(BF16))

    return _lift(one, a, b)


def _dot(a, b, two_pass=False):
    return _mm(a, b, ((1,), (0,)), two_pass)


def _dot_nt(a, b, two_pass=False):
    return _mm(a, b, ((1,), (1,)), two_pass)


def _dot_tn(a, b):
    return _mm(a, b, ((0,), (0,)), False)


def _chunk_masks(reverse):
    row = lax.broadcasted_iota(jnp.int32, (SCAN_CHUNK, SCAN_CHUNK), 0)
    col = lax.broadcasted_iota(jnp.int32, (SCAN_CHUNK, SCAN_CHUNK), 1)
    earlier = (row < col) if reverse else (row > col)
    eye = row == col
    blk16 = (row >> 4) == (col >> 4)
    blk32 = (row >> 5) == (col >> 5)
    return dict(
        strict=earlier.astype(F32), incl=(earlier | eye).astype(F32), eye=eye.astype(F32),
        d16=blk16.astype(F32), off32=(blk32 & jnp.logical_not(blk16)).astype(F32),
        off64=jnp.logical_not(blk32).astype(F32))


def _scan_chains(heads):
    fwd, bwd = _chunk_masks(False), _chunk_masks(True)
    chains = [(b, d, h) for b in range(SCAN_ROWS) for d in range(2) for h in range(heads)]
    masks = {key: _Par([(bwd if d else fwd)[key] for _, d, _ in chains]) for key in fwd}
    return chains, [d == 1 for _, d, _ in chains], masks


def _last_row(cum, revs):
    return _Par([c[0:1] if rev else c[SCAN_CHUNK - 1:SCAN_CHUNK] for c, rev in zip(cum.xs, revs)])


def _unit_triangular_inverse(n, m):
    eye = m["eye"]
    nd = n * m["d16"]
    n2 = _dot(nd, nd)
    x = eye + nd
    x = x + _dot(x, n2)
    n4 = _dot(n2, n2)
    x = x + _dot(x, n4)
    n8 = _dot(n4, n4)
    x = x + _dot(x, n8)
    x = x + _dot(x, _dot(n * m["off32"], x))
    x = x + _dot(x, _dot(n * m["off64"], x))
    return x


def _rwkv_chunk(r, lw, k, v, a, b, h0, m, revs):
    cum = _dot(m["incl"], lw, two_pass=True)
    last = _last_row(cum, revs)
    g_inv = _exp(-cum)
    at = a * _exp(cum - lw)
    rt = r * _exp(cum)
    bt = b * g_inv
    kt = k * g_inv
    to_end = _exp(last - cum)
    bh = b * to_end
    kh = k * to_end
    ar = _stack_rows(at, rt)
    gb = _dot_nt(ar, bt)
    gk = _dot_nt(ar, kt)
    n_ab = gb[:SCAN_CHUNK] * m["strict"]
    l_ak = gk[:SCAN_CHUNK] * m["strict"]
    m_rb = gb[SCAN_CHUNK:] * m["incl"]
    m_rk = gk[SCAN_CHUNK:] * m["incl"]
    tinv = _unit_triangular_inverse(n_ab, m)
    p1 = _dot(tinv, at)
    u0 = _dot(tinv, _dot(l_ak, v))
    p2 = rt + _dot(m_rb, p1)
    o0 = _dot(m_rb, u0) + _dot(m_rk, v)
    a_c = m["eye"] * _exp(last) + _dot_tn(bh, p1)
    g_c = _dot_tn(bh, u0) + _dot_tn(kh, v)
    return _dot(p2, h0) + o0, _dot(a_c, h0) + g_c


def _rwkv_scan_kernel(*refs):
    ins, h0_ref = refs[:12], refs[12]
    o_refs, hout_ref, h_ref = refs[13:15], refs[15], refs[16]
    c = pl.program_id(1)

    @pl.when(c == 0)
    def _():
        h_ref[...] = h0_ref[...]

    chains, revs, m = _scan_chains(RW_HEADS)
    seqs = [_Par([_head_tile(ins[6 * d + i], b, h) for b, d, h in chains]) for i in range(6)]
    o, h_new = _rwkv_chunk(*seqs, _Par([h_ref[b, d, h] for b, d, h in chains]), m, revs)
    _store_heads(o_refs, o, chains)
    for i, (b, d, h) in enumerate(chains):
        h_ref[b, d, h] = h_new.xs[i]

    @pl.when(c == pl.num_programs(1) - 1)
    def _():
        hout_ref[...] = h_ref[...]


def _head_tile(ref, b, h):
    return ref[b, :, h * HEAD_DIM:(h + 1) * HEAD_DIM]


def _store_heads(o_refs, o, chains):
    for d in range(2):
        for b in range(SCAN_ROWS):
            tiles = [o.xs[i] for i, (bi, di, _) in enumerate(chains) if (bi, di) == (b, d)]
            o_refs[d][b] = jnp.concatenate(tiles, axis=1)


def _scan_specs(B, H, T, dh):
    n = T // SCAN_CHUNK
    fwd = pl.BlockSpec((SCAN_ROWS, SCAN_CHUNK, H * dh), lambda b, c: (b, c, 0))
    bwd = pl.BlockSpec((SCAN_ROWS, SCAN_CHUNK, H * dh), lambda b, c: (b, n - 1 - c, 0))
    state = pl.BlockSpec((SCAN_ROWS, 2, H, dh, dh), lambda b, c: (b, 0, 0, 0, 0))
    return (B // SCAN_ROWS, n), fwd, bwd, state


def _rwkv_scan_pallas(seqs, h0):
    B, T, _ = seqs[0][0].shape
    H, dh = RW_HEADS, HEAD_DIM
    grid, fwd, bwd, state_spec = _scan_specs(B, H, T, dh)
    seq = jax.ShapeDtypeStruct((B, T, H * dh), F32)
    return pl.pallas_call(
        _rwkv_scan_kernel,
        grid=grid,
        in_specs=[fwd] * 6 + [bwd] * 6 + [state_spec],
        out_specs=[fwd, bwd, state_spec],
        out_shape=[seq, seq, jax.ShapeDtypeStruct((B, 2, H, dh, dh), F32)],
        scratch_shapes=[pltpu.VMEM((SCAN_ROWS, 2, H, dh, dh), F32)],
        compiler_params=pltpu.CompilerParams(dimension_semantics=("arbitrary", "arbitrary"),
                                             vmem_limit_bytes=VMEM_LIMIT_BYTES),
        name="rwkv_scan",
    )(*seqs[0], *seqs[1], h0)


def _rwkv_mixer(p, lp, S0):
    B, T, _ = p.shape
    W, H = GROUP_W, RW_HEADS
    o0 = 3 * W
    o1 = o0 + 2 * RW_DECAY_RANK
    o2 = o1 + 2 * RW_AAA_RANK
    rkv = p[..., :o0]
    wd = p[..., o0:o1].reshape(B, T, 2, RW_DECAY_RANK)
    ad = p[..., o1:o2].reshape(B, T, 2, RW_AAA_RANK)
    gate = jax.nn.sigmoid(p[..., o2:]) @ lp["rw_g2"]
    heads = lambda a: a.astype(F32).reshape(B, T, H, HEAD_DIM)
    flat = lambda a: a.reshape(B, T, W)
    seqs, bonuses = [], []
    for d in range(2):
        base = jnp.concatenate([rkv, wd[:, :, d], ad[:, :, d]], axis=-1)
        if d == 0:
            shifted = jnp.pad(base[:, :-1], ((0, 0), (1, 0), (0, 0)))
        else:
            shifted = jnp.pad(base[:, 1:], ((0, 0), (0, 1), (0, 0)))
        xd = base + (shifted - base) * lp["rw_mu"][d]
        r, k, v = xd[..., :W], xd[..., W:2 * W], xd[..., 2 * W:3 * W]
        wl, al = xd[..., 3 * W:3 * W + RW_DECAY_RANK], xd[..., 3 * W + RW_DECAY_RANK:]
        w = -jax.nn.softplus(-(lp["rw_w0"][d] + jnp.tanh(wl) @ lp["rw_w2"][d])) - 0.5
        log_decay = -jnp.exp(w.astype(F32))
        a = jax.nn.sigmoid((lp["rw_a0"][d] + al @ lp["rw_a2"][d]).astype(F32))
        kf = k.astype(F32)
        kk = _l2_norm(heads(kf * lp["rw_k_k"].astype(F32)))
        k_eff = heads(kf * (1.0 + (a - 1.0) * lp["rw_k_a"].astype(F32)))
        rh, vh, ah = heads(r), heads(v), heads(a)
        seqs.append([r.astype(F32), log_decay, flat(k_eff), v.astype(F32), flat(-kk), flat(kk * ah)])
        bonuses.append(jnp.sum(rh * k_eff * lp["rw_r_k"].astype(F32), axis=-1, keepdims=True) * vh)
    o_f, o_b, h_fin = _rwkv_scan_pallas(seqs, jnp.swapaxes(S0.astype(F32), -1, -2))
    o = heads(o_f + o_b)
    finals = jnp.swapaxes(h_fin, -1, -2)
    mu = jnp.mean(o, axis=-1, keepdims=True)
    var = jnp.mean(jnp.square(o - mu), axis=-1, keepdims=True)
    y = ((o - mu) * lax.rsqrt(var + RW_LN_EPS)).reshape(B, T, W) * lp["rw_ln_g"].astype(F32) + lp["rw_ln_b"].astype(F32)
    y = y + (bonuses[0] + bonuses[1]).reshape(B, T, W)
    return y.astype(p.dtype) * gate, finals


def _delta_chunk(q, k, v, gb, betab, s0, m, revs):
    gc = _dot(m["incl"], gb, two_pass=True)
    last = _last_row(gc, revs)
    diff = gc - _dot_nt(m["eye"], gc, two_pass=True)
    dmat = _lift(lambda keep, d: jnp.where(keep > 0.0, jnp.exp(jnp.where(keep > 0.0, d, 0.0)), 0.0), m["incl"], diff)
    kb = k * betab
    gram = _dot_nt(_stack_rows(kb, q), k)
    a_mat = gram[:SCAN_CHUNK] * dmat * m["strict"]
    qk = gram[SCAN_CHUNK:] * dmat
    tm = _unit_triangular_inverse(-a_mat, m)
    u = _dot(tm, v * betab)
    w = _dot(tm, kb * _exp(gc))
    v_new = u - _dot(w, s0)
    o = _dot(q * _exp(gc), s0) + _dot(qk, v_new)
    return o, s0 * _exp(last) + _dot_tn(k * _exp(last - gc), v_new)


def _delta_scan_kernel(*refs):
    ins, s0_ref = refs[:10], refs[10]
    o_refs, sout_ref, s_ref = refs[11:13], refs[13], refs[14]
    c = pl.program_id(1)

    @pl.when(c == 0)
    def _():
        s_ref[...] = s0_ref[...]

    chains, revs, m = _scan_chains(DN_HEADS)
    seqs = [_Par([_head_tile(ins[5 * d + i], b, h) for b, d, h in chains]) for i in range(5)]
    o, s_new = _delta_chunk(*seqs, _Par([s_ref[b, d, h] for b, d, h in chains]), m, revs)
    _store_heads(o_refs, o, chains)
    for i, (b, d, h) in enumerate(chains):
        s_ref[b, d, h] = s_new.xs[i]

    @pl.when(c == pl.num_programs(1) - 1)
    def _():
        sout_ref[...] = s_ref[...]


def _delta_scan_pallas(q, k, v, g, beta, s0):
    assert SCAN_CHUNK == HEAD_DIM
    B, T, _ = q.shape
    H, dh = DN_HEADS, HEAD_DIM
    grid, fwd, bwd, state_spec = _scan_specs(B, H, T, dh)
    seq = jax.ShapeDtypeStruct((B, T, H * dh), F32)
    return pl.pallas_call(
        _delta_scan_kernel,
        grid=grid,
        in_specs=[fwd] * 5 + [bwd] * 5 + [state_spec],
        out_specs=[fwd, bwd, state_spec],
        out_shape=[seq, seq, jax.ShapeDtypeStruct((B, 2, H, dh, dh), F32)],
        scratch_shapes=[pltpu.VMEM((SCAN_ROWS, 2, H, dh, dh), F32)],
        compiler_params=pltpu.CompilerParams(dimension_semantics=("arbitrary", "arbitrary"),
                                             vmem_limit_bytes=VMEM_LIMIT_BYTES),
        name="delta_scan",
    )(q, k, v, g[0], beta[0], q, k, v, g[1], beta[1], s0)


def _deltanet_mixer(p, lp, S0):
    B, T, _ = p.shape
    W, H = GROUP_W, DN_HEADS
    xin = jnp.pad(p[..., :3 * W], ((0, 0), (DN_CONV // 2, DN_CONV // 2), (0, 0)))
    qkv = sum(xin[:, j:j + T] * lp["dn_conv"][j] for j in range(DN_CONV))
    qkv = jax.nn.silu(qkv)
    heads = lambda a: a.astype(F32).reshape(B, T, H, HEAD_DIM)
    flat = lambda a: a.reshape(B, T, W)
    q = _l2_norm(heads(qkv[..., :W])) * (HEAD_DIM ** -0.5)
    k = _l2_norm(heads(qkv[..., W:2 * W]))
    v = heads(qkv[..., 2 * W:])
    beta = jax.nn.sigmoid(p[..., 3 * W:3 * W + 2 * H].astype(F32)).reshape(B, T, 2, H)
    alpha = p[..., 3 * W + 2 * H:3 * W + 4 * H].astype(F32).reshape(B, T, 2, H)
    g = -jnp.exp(lp["dn_a_log"].astype(F32)) * jax.nn.softplus(alpha + lp["dn_dt_bias"].astype(F32))
    z = heads(p[..., 3 * W + 4 * H:])
    lanes = lambda a: [jnp.repeat(a[:, :, d], HEAD_DIM, axis=-1) for d in range(2)]
    o_f, o_b, s_fin = _delta_scan_pallas(flat(q), flat(k), flat(v), lanes(g), lanes(beta), S0.astype(F32))
    o = _rms_norm(heads(o_f + o_b), lp["dn_norm_g"]) * jax.nn.silu(z)
    return o.reshape(B, T, W).astype(p.dtype), s_fin


def _modulated_norm(x, gain, scale, shift):
    y = x * lax.rsqrt(jnp.mean(x * x, axis=-1, keepdims=True) + NORM_EPS)
    return (y * gain) * (1.0 + scale) + shift


def _proj_in_kernel(x_ref, mod_ref, gain_ref, w_ref, o_ref):
    h = _modulated_norm(x_ref[...], gain_ref[...], mod_ref[0, MOD_SCALE1:MOD_SCALE1 + 1], mod_ref[0, MOD_SHIFT1:MOD_SHIFT1 + 1])
    o_ref[...] = jnp.dot(h.astype(BF16), w_ref[...], preferred_element_type=F32)


def _proj_out_kernel(mix_ref, x_ref, mod_ref, gain_ref, w_ref, x_out_ref, h2_ref):
    y = jnp.dot(mix_ref[...].astype(BF16), w_ref[...], preferred_element_type=F32)
    x_new = x_ref[...] + mod_ref[0, MOD_GATE1:MOD_GATE1 + 1] * y
    x_out_ref[...] = x_new
    h2_ref[...] = _modulated_norm(x_new, gain_ref[...], mod_ref[0, MOD_SCALE2:MOD_SCALE2 + 1],
                                  mod_ref[0, MOD_SHIFT2:MOD_SHIFT2 + 1])


def _proj_in(x, mod, gain, w_bf, tokens_per_cond):
    n, cols = x.shape[0], w_bf.shape[1]
    tm = PROJ_TOKENS
    tok = pl.BlockSpec((tm, D_MODEL), lambda i: (i, 0))
    return pl.pallas_call(
        _proj_in_kernel,
        grid=(n // tm,),
        in_specs=[tok, _mod_spec(mod, tokens_per_cond, tm), pl.BlockSpec((1, D_MODEL), lambda i: (0, 0)),
                  pl.BlockSpec((D_MODEL, cols), lambda i: (0, 0))],
        out_specs=pl.BlockSpec((tm, cols), lambda i: (i, 0)),
        out_shape=jax.ShapeDtypeStruct((n, cols), F32),
        compiler_params=pltpu.CompilerParams(dimension_semantics=("arbitrary",), vmem_limit_bytes=VMEM_LIMIT_BYTES),
        name="proj_in",
    )(x, mod, gain, w_bf)


def _proj_out(mix, x, mod, gain, w_bf, tokens_per_cond):
    n = x.shape[0]
    tm = PROJ_TOKENS
    tok = pl.BlockSpec((tm, D_MODEL), lambda i: (i, 0))
    out = jax.ShapeDtypeStruct((n, D_MODEL), F32)
    return pl.pallas_call(
        _proj_out_kernel,
        grid=(n // tm,),
        in_specs=[tok, tok, _mod_spec(mod, tokens_per_cond, tm), pl.BlockSpec((1, D_MODEL), lambda i: (0, 0)),
                  pl.BlockSpec((D_MODEL, D_MODEL), lambda i: (0, 0))],
        out_specs=[tok, tok],
        out_shape=[out, out],
        compiler_params=pltpu.CompilerParams(dimension_semantics=("arbitrary",), vmem_limit_bytes=VMEM_LIMIT_BYTES),
        name="proj_out",
    )(mix, x, mod, gain, w_bf)


def _trunk_layer(x, cond, lp, ctx, final_gain):
    B, T, _ = x.shape
    mod = (jax.nn.silu(cond) @ lp["w_mod"] + lp["b_mod"]).reshape(-1, 6, D_MODEL)
    x = x.reshape(B * T, D_MODEL)
    proj = _proj_in(x, mod, lp["norm1_g"][None], lp["w_in_bf"], T).reshape(B, T, -1)
    pa, pb, pc, pd, _ = jnp.split(proj, [GQA_PROJ, GQA_PROJ + RW_PROJ, GQA_PROJ + RW_PROJ + NA_PROJ, IN_COLS], axis=-1)
    if ctx is None:
        ctx_a = None
        ctx_c = None
        s_rw0 = jnp.zeros((B, 2, RW_HEADS, HEAD_DIM, HEAD_DIM), F32)
        s_dn0 = jnp.zeros((B, 2, DN_HEADS, HEAD_DIM, HEAD_DIM), F32)
    else:
        ka_c, va_c, kc_c, vc_c, s_rw0, s_dn0 = ctx
        ctx_a = (ka_c, va_c)
        ctx_c = (kc_c, vc_c)
    oa, (ka, va) = _gqa_mixer(pa, lp, ctx_a)
    ob, s_rw = _rwkv_mixer(pb, lp, s_rw0)
    oc, (kc, vc) = _na_mixer(pc, lp, ctx_c)
    od, s_dn = _deltanet_mixer(pd, lp, s_dn0)
    mix = jnp.concatenate([oa, ob, oc, od], axis=-1).reshape(B * T, D_MODEL)
    x, h2 = _proj_out(mix, x, mod, lp["norm2_g"][None], lp["w_out_bf"], T)
    x = _peer_residual(x, h2, mod, lp, T, final_gain)
    return x.reshape(B, T, D_MODEL), (ka, va, kc, vc, s_rw, s_dn)


def kernel(x_prompt, x_sample, c, cache_gqa_k, cache_gqa_v, cache_na_k, cache_na_v, state_rwkv, state_delta, c_ctx, norm1_g, norm2_g, w_mod, b_mod, w_in, w_out, gqa_q_norm, gqa_k_norm, rw_mu, rw_w0, rw_w2, rw_a0, rw_a2, rw_g2, rw_k_k, rw_k_a, rw_r_k, rw_ln_g, rw_ln_b, na_bias, dn_conv, dn_a_log, dn_dt_bias, dn_norm_g, peer_wq, peer_keys, peer_u, peer_v, final_norm_g):
    depth = w_in.shape[0]

    def layer_params(l):
        return {
            "norm1_g": norm1_g[l], "norm2_g": norm2_g[l], "w_mod": w_mod[l], "b_mod": b_mod[l],
            "w_in_bf": jnp.pad(w_in[l].astype(BF16), ((0, 0), (0, IN_COLS_PADDED - IN_COLS))),
            "w_out_bf": w_out[l].astype(BF16), "gqa_q_norm": gqa_q_norm[l], "gqa_k_norm": gqa_k_norm[l],
            "rw_mu": rw_mu[l], "rw_w0": rw_w0[l], "rw_w2": rw_w2[l], "rw_a0": rw_a0[l], "rw_a2": rw_a2[l],
            "rw_g2": rw_g2[l], "rw_k_k": rw_k_k[l], "rw_k_a": rw_k_a[l], "rw_r_k": rw_r_k[l],
            "rw_ln_g": rw_ln_g[l], "rw_ln_b": rw_ln_b[l], "na_bias": na_bias[l],
            "dn_conv": dn_conv[l], "dn_a_log": dn_a_log[l], "dn_dt_bias": dn_dt_bias[l], "dn_norm_g": dn_norm_g[l],
            "peer_wq_bf": peer_wq[l].astype(BF16),
            "peer_keys_bf": peer_keys[l].reshape(2 * PEER_HEADS, PEER_NKEYS, PEER_DQ // 2).astype(BF16),
            "peer_u_bf": peer_u[l].astype(BF16),
            "peer_v_bf": peer_v[l].astype(BF16),
        }

    params = [layer_params(l) for l in range(depth)]

    xp = x_prompt
    cond_ctx = c_ctx[None, :]
    per_layer = []
    closing = lambda l: final_norm_g if l == depth - 1 else None
    for l in range(depth):
        xp, st = _trunk_layer(xp, cond_ctx, params[l], None, closing(l))
        per_layer.append(st)
    y_prompt = xp

    xs = x_sample
    for l in range(depth):
        ctx = (cache_gqa_k[:, l], cache_gqa_v[:, l], cache_na_k[:, l], cache_na_v[:, l],
               state_rwkv[:, l], state_delta[:, l])
        xs, _ = _trunk_layer(xs, c, params[l], ctx, closing(l))
    y_sample = xs

    dt = x_prompt.dtype
    new_gqa_k = jnp.stack([st[0] for st in per_layer], axis=1).astype(dt)
    new_gqa_v = jnp.stack([st[1] for st in per_layer], axis=1).astype(dt)
    new_na_k = jnp.stack([st[2] for st in per_layer], axis=1).astype(dt)
    new_na_v = jnp.stack([st[3] for st in per_layer], axis=1).astype(dt)
    new_state_rwkv = jnp.stack([st[4] for st in per_layer], axis=1).astype(dt)
    new_state_delta = jnp.stack([st[5] for st in per_layer], axis=1).astype(dt)
    return (y_prompt, y_sample, new_gqa_k, new_gqa_v, new_na_k, new_na_v, new_state_rwkv, new_state_delta)
```
